```python
import jax, jax.numpy as jnp
from jax import lax
import numpy as np

D_MODEL = 4096
BATCH = 1
SEQ = 8192
DEPTH = 2

GRID_W = 64
CTX_LEN = 256
D_POOL = D_MODEL // 4
POOL_WINDOWS = (2, 4, 8, 16)
N_POOL_GROUPS = 4
POOL_GC = D_POOL // N_POOL_GROUPS
HEAD_DIM = 128
D_ATTN = D_MODEL // 2
N_HEADS = D_ATTN // HEAD_DIM
WIN_ROWS = 8
WIN_COLS = 16
D_FNET = D_MODEL // 4
N_FNET_GROUPS = 4
FNET_GC = D_FNET // N_FNET_GROUPS
D_IN = D_POOL + 3 * D_ATTN + D_FNET
N_BRANCHES = 3
D_FF = 11008
CONV_W = 3
N_MOD = 6
EPS = 1e-6

kernel_name = 'hybrid_pool_natten_fnet_dit_block'


def rms_norm(x, w):
    x32 = x.astype(jnp.float32)
    y = x32 * lax.rsqrt(jnp.mean(x32 * x32, axis=-1, keepdims=True) + EPS)
    return y.astype(x.dtype) * w


def modulate(h, shift, scale):
    return h * (1 + scale) + shift


def to_heads(t):
    return t.reshape(t.shape[:-1] + (N_HEADS, HEAD_DIM))


def pool_mix(u, pool_w, pool_scale):
    B, S, _ = u.shape
    ug = u.reshape(B, S, N_POOL_GROUPS, POOL_GC)
    cs = jnp.pad(jnp.cumsum(ug.astype(jnp.float32), axis=1), ((0, 0), (1, 0), (0, 0), (0, 0)))
    t = jnp.arange(S)
    means = []
    for g, w in enumerate(POOL_WINDOWS):
        lo = jnp.clip(t - w // 2, 0, S - 1)
        hi = jnp.clip(t + (w - 1 - w // 2), 0, S - 1)
        cnt = (hi - lo + 1).astype(jnp.float32)[None, :, None]
        means.append((cs[:, hi + 1, g] - cs[:, lo, g]) / cnt)
    pooled = jnp.stack(means, axis=2)
    d = (pooled - ug.astype(jnp.float32)).astype(u.dtype)
    y = jnp.einsum('bsgc,gce->bsge', d, pool_w).reshape(B, S, D_POOL)
    return y * pool_scale


def fourier_mix(u, fnet_w):
    B, S, _ = u.shape
    ug = u.reshape(B, S, N_FNET_GROUPS, FNET_GC).astype(jnp.float32)
    f = jnp.fft.fft2(ug, axes=(1, 3), norm='ortho').real.astype(u.dtype)
    return jnp.einsum('bsgc,gce->bsge', f, fnet_w).reshape(B, S, D_FNET)


def neighborhood_attention(q, k, v, kc, vc, rpb):
    B, S, H, hd = q.shape
    rows = S // GRID_W
    kr = min(WIN_ROWS, rows)
    scale = HEAD_DIM ** -0.5
    qg = q.reshape(B, rows, GRID_W, H, hd)
    kg = k.reshape(B, rows, GRID_W, H, hd)
    vg = v.reshape(B, rows, GRID_W, H, hd)
    col = jnp.arange(GRID_W)
    col_start = jnp.clip(col - WIN_COLS // 2, 0, GRID_W - WIN_COLS)
    col_idx = col_start[:, None] + jnp.arange(WIN_COLS)[None, :]
    col_off = col_idx - col[:, None] + (WIN_COLS - 1)
    bias_cols = rpb[:, :, col_off]

    def row_step(args):
        r, q_row = args
        r_start = jnp.clip(r - kr // 2, 0, rows - kr)
        k_rows = lax.dynamic_slice_in_dim(kg, r_start, kr, axis=1)
        v_rows = lax.dynamic_slice_in_dim(vg, r_start, kr, axis=1)
        k_win = k_rows[:, :, col_idx]
        v_win = v_rows[:, :, col_idx]
        row_off = r_start + jnp.arange(kr) - r + (WIN_ROWS - 1)
        bias = bias_cols[:, row_off].transpose(0, 2, 1, 3)
        s_win = jnp.einsum('bwhd,brwchd->bhwrc', q_row, k_win) * scale + bias[None]
        s_ctx = jnp.einsum('bwhd,bnhd->bhwn', q_row, kc) * scale
        s = jnp.concatenate([s_win.reshape(B, H, GRID_W, kr * WIN_COLS), s_ctx], axis=-1)
        p = jax.nn.softmax(s.astype(jnp.float32), axis=-1).astype(v.dtype)
        p_win = p[..., :kr * WIN_COLS].reshape(B, H, GRID_W, kr, WIN_COLS)
        p_ctx = p[..., kr * WIN_COLS:]
        return (jnp.einsum('bhwrc,brwchd->bwhd', p_win, v_win)
                + jnp.einsum('bhwn,bnhd->bwhd', p_ctx, vc))

    out = lax.map(row_step, (jnp.arange(rows), qg.transpose(1, 0, 2, 3, 4)))
    return out.transpose(1, 0, 2, 3, 4).reshape(B, S, H * hd)


def context_attention(qc, kc, vc):
    B, N, H, hd = qc.shape
    s = jnp.einsum('bnhd,bmhd->bhnm', qc, kc) * (HEAD_DIM ** -0.5)
    p = jax.nn.softmax(s.astype(jnp.float32), axis=-1).astype(vc.dtype)
    return jnp.einsum('bhnm,bmhd->bnhd', p, vc).reshape(B, N, H * hd)


def merge_branches(h, y_pool, y_attn, y_fnet, w_gate, b_gate, w_br_pool, w_br_attn, w_br_fnet, w_o):
    g_pool, g_attn, g_fnet = jnp.split(jax.nn.sigmoid(h @ w_gate + b_gate), N_BRANCHES, axis=-1)
    m = g_pool * (y_pool @ w_br_pool) + g_attn * (y_attn @ w_br_attn) + g_fnet * (y_fnet @ w_br_fnet)
    return m @ w_o


def conv_ffn(h, w_up, conv_w, conv_b, w_down):
    u = h @ w_up
    up = jnp.pad(u, ((0, 0), (1, 1), (0, 0)))
    u = up[:, :-2] * conv_w[0] + up[:, 1:-1] * conv_w[1] + up[:, 2:] * conv_w[2] + conv_b
    gate, val = jnp.split(u, 2, axis=-1)
    return (jax.nn.silu(gate) * val) @ w_down


def setup_inputs(seed: int = 0) -> dict:
    key = jax.random.key(seed)
    ks = jax.random.split(key, 26)
    L, D = DEPTH, D_MODEL

    def nrm(k, shape, scale):
        return jax.random.normal(k, shape, jnp.float32) * scale

    return {
        'x': nrm(ks[0], (BATCH, SEQ, D), 1.0),
        'c': nrm(ks[1], (BATCH, D), 1.0),
        'ctx': nrm(ks[2], (BATCH, CTX_LEN, D), 1.0),
        'c_ctx': nrm(ks[3], (D,), 1.0),
        'w_ada': nrm(ks[4], (L, D, N_MOD * D), 0.5 * D ** -0.5),
        'b_ada': nrm(ks[5], (L, N_MOD * D), 0.02),
        'norm1_w': 1.0 + nrm(ks[6], (L, D), 0.02),
        'norm2_w': 1.0 + nrm(ks[7], (L, D), 0.02),
        'w_in': nrm(ks[8], (L, D, D_IN), D ** -0.5),
        'pool_w': nrm(ks[9], (L, N_POOL_GROUPS, POOL_GC, POOL_GC), POOL_GC ** -0.5),
        'pool_scale': 1.0 + nrm(ks[10], (L, D_POOL), 0.02),
        'q_norm_w': 1.0 + nrm(ks[11], (L, HEAD_DIM), 0.02),
        'k_norm_w': 1.0 + nrm(ks[12], (L, HEAD_DIM), 0.02),
        'rpb': nrm(ks[13], (L, N_HEADS, 2 * WIN_ROWS - 1, 2 * WIN_COLS - 1), 0.1),
        'fnet_w': nrm(ks[14], (L, N_FNET_GROUPS, FNET_GC, FNET_GC), FNET_GC ** -0.5),
        'w_gate': nrm(ks[15], (L, D, N_BRANCHES * D), D ** -0.5),
        'b_gate': nrm(ks[16], (L, N_BRANCHES * D), 0.02),
        'w_br_pool': nrm(ks[17], (L, D_POOL, D), D_POOL ** -0.5),
        'w_br_attn': nrm(ks[18], (L, D_ATTN, D), D_ATTN ** -0.5),
        'w_br_fnet': nrm(ks[19], (L, D_FNET, D), D_FNET ** -0.5),
        'w_o': nrm(ks[20], (L, D, D), D ** -0.5),
        'w_up': nrm(ks[21], (L, D, 2 * D_FF), D ** -0.5),
        'conv_w': nrm(ks[22], (L, CONV_W, 2 * D_FF), CONV_W ** -0.5),
        'conv_b': nrm(ks[23], (L, 2 * D_FF), 0.02),
        'w_down': nrm(ks[24], (L, D_FF, D), D_FF ** -0.5),
    }


def reference(x, c, ctx, c_ctx, w_ada, b_ada, norm1_w, norm2_w, w_in, pool_w, pool_scale,
              q_norm_w, k_norm_w, rpb, fnet_w, w_gate, b_gate, w_br_pool, w_br_attn, w_br_fnet,
              w_o, w_up, conv_w, conv_b, w_down):
    split_at = [D_POOL, D_POOL + D_ATTN, D_POOL + 2 * D_ATTN, D_POOL + 3 * D_ATTN]
    k_lo, k_hi = D_POOL + D_ATTN, D_POOL + 3 * D_ATTN
    xc = ctx
    s_lat = jax.nn.silu(c)
    s_ctx = jax.nn.silu(c_ctx)
    for l in range(DEPTH):
        last = l == DEPTH - 1
        mod = (s_lat @ w_ada[l] + b_ada[l])[:, None, :]
        mod_c = s_ctx @ w_ada[l] + b_ada[l]
        sh1, sc1, g1, sh2, sc2, g2 = jnp.split(mod, N_MOD, axis=-1)
        csh1, csc1, cg1, csh2, csc2, cg2 = jnp.split(mod_c, N_MOD, axis=-1)

        h = modulate(rms_norm(x, norm1_w[l]), sh1, sc1)
        hc = modulate(rms_norm(xc, norm1_w[l]), csh1, csc1)
        u_pool, q, k, v, u_fnet = jnp.split(h @ w_in[l], split_at, axis=-1)
        q = rms_norm(to_heads(q), q_norm_w[l])
        k = rms_norm(to_heads(k), k_norm_w[l])
        v = to_heads(v)
        if last:
            kc, vc = jnp.split(hc @ w_in[l][:, k_lo:k_hi], 2, axis=-1)
        else:
            uc_pool, qc, kc, vc, uc_fnet = jnp.split(hc @ w_in[l], split_at, axis=-1)
            qc = rms_norm(to_heads(qc), q_norm_w[l])
        kc = rms_norm(to_heads(kc), k_norm_w[l])
        vc = to_heads(vc)

        y_attn = neighborhood_attention(q, k, v, kc, vc, rpb[l])
        y_pool = pool_mix(u_pool, pool_w[l], pool_scale[l])
        y_fnet = fourier_mix(u_fnet, fnet_w[l])
        x = x + g1 * merge_branches(h, y_pool, y_attn, y_fnet, w_gate[l], b_gate[l],
                                    w_br_pool[l], w_br_attn[l], w_br_fnet[l], w_o[l])

        h2 = modulate(rms_norm(x, norm2_w[l]), sh2, sc2)
        x = x + g2 * conv_ffn(h2, w_up[l], conv_w[l], conv_b[l], w_down[l])

        if not last:
            yc_attn = context_attention(qc, kc, vc)
            yc_pool = pool_mix(uc_pool, pool_w[l], pool_scale[l])
            yc_fnet = fourier_mix(uc_fnet, fnet_w[l])
            xc = xc + cg1 * merge_branches(hc, yc_pool, yc_attn, yc_fnet, w_gate[l], b_gate[l],
                                           w_br_pool[l], w_br_attn[l], w_br_fnet[l], w_o[l])
            hc2 = modulate(rms_norm(xc, norm2_w[l]), csh2, csc2)
            xc = xc + cg2 * conv_ffn(hc2, w_up[l], conv_w[l], conv_b[l], w_down[l])
    return x
```

```python
import functools
import math

import numpy as np
import jax
import jax.numpy as jnp
from jax import lax
from jax.experimental import pallas as pl
from jax.experimental.pallas import tpu as pltpu

F32 = jnp.float32
BF16 = jnp.bfloat16

GRID_W = 64
HEAD_DIM = 128
WIN_ROWS = 8
WIN_COLS = 16
POOL_WINDOWS = (2, 4, 8, 16)
N_GROUPS = 4
N_MOD = 6
CONV_W = 3
EPS = 1e-6

LANES = 128
SUBLANES_F32 = 8
SUBLANES_BF16 = 16
VMEM_CAP_BYTES = 56 * 1024 * 1024

ATTN_ROWS = 4
ATTN_KEY_ROWS = ATTN_ROWS + WIN_ROWS
NEG_BIAS = -1e30


def _round_up(n, m):
    return (n + m - 1) // m * m


def _tile(n, pref):
    if n <= pref:
        return n
    t = pref
    while t >= LANES:
        if n % t == 0 and t % LANES == 0:
            return t
        t -= LANES
    return n


def _params(n_axes, vmem_bytes):
    return pltpu.CompilerParams(
        dimension_semantics=("arbitrary",) * n_axes,
        vmem_limit_bytes=int(min(VMEM_CAP_BYTES, max(vmem_bytes, 16 * 1024 * 1024))),
    )


def _dot(a, b):
    return jnp.dot(a, b, preferred_element_type=F32)


def _dot_nt(a, b):
    return lax.dot_general(a, b, (((1,), (1,)), ((), ())), preferred_element_type=F32)


def _rms_mod(x, nw, sh, sc):
    ms = jnp.mean(x * x, axis=-1, keepdims=True)
    return (x * lax.rsqrt(ms + EPS) * nw) * (1.0 + sc) + sh


def _ada_kernel(s_ref, w_ref, b_ref, o_ref):
    s = s_ref[...]
    a = (s * jax.nn.sigmoid(s)).astype(BF16)
    o_ref[0] = _dot(a, w_ref[0].astype(BF16)) + b_ref[0]


def _ada(cond, w_ada, b_ada):
    n_layers, d, n = w_ada.shape
    tn = _tile(n, 1024)
    return pl.pallas_call(
        _ada_kernel,
        grid=(n_layers, n // tn),
        in_specs=[
            pl.BlockSpec((SUBLANES_F32, d), lambda l, j: (0, 0)),
            pl.BlockSpec((1, d, tn), lambda l, j: (l, 0, j)),
            pl.BlockSpec((1, 1, tn), lambda l, j: (l, 0, j)),
        ],
        out_specs=pl.BlockSpec((1, SUBLANES_F32, tn), lambda l, j: (l, 0, j)),
        out_shape=jax.ShapeDtypeStruct((n_layers, SUBLANES_F32, n), F32),
        compiler_params=_params(2, 2 * d * tn * 4 + 8 * 1024 * 1024),
        name="ada",
    )(cond, w_ada, b_ada.reshape(n_layers, 1, n))


def _norm_kernel(x_ref, nw_ref, sh_ref, sc_ref, o_ref):
    o_ref[...] = _rms_mod(x_ref[...], nw_ref[...], sh_ref[...], sc_ref[...]).astype(o_ref.dtype)


def _norm_mod(x, nw, sh, sc):
    m, d = x.shape
    tm = min(m, 256)
    vec = pl.BlockSpec((1, d), lambda i: (0, 0))
    return pl.pallas_call(
        _norm_kernel,
        grid=(m // tm,),
        in_specs=[pl.BlockSpec((tm, d), lambda i: (i, 0)), vec, vec, vec],
        out_specs=pl.BlockSpec((tm, d), lambda i: (i, 0)),
        out_shape=jax.ShapeDtypeStruct((m, d), BF16),
        compiler_params=_params(1, 6 * tm * d * 4),
        name="norm_mod",
    )(x, nw, sh, sc)


def _halo_kernel(xp_ref, xn_ref, nw_ref, sh_ref, sc_ref, o_ref):
    i = pl.program_id(0)
    n = pl.num_programs(0)
    hp = _rms_mod(xp_ref[...], nw_ref[...], sh_ref[...], sc_ref[...])
    hn = _rms_mod(xn_ref[...], nw_ref[...], sh_ref[...], sc_ref[...])
    hp = jnp.where(i > 0, hp, 0.0)
    hn = jnp.where(i < n - 1, hn, 0.0)
    o_ref[0] = jnp.concatenate([hp, hn], axis=0).astype(o_ref.dtype)


def _halo_norm(x, nw, sh, sc, tm):
    m, d = x.shape
    nblk = m // tm
    r = SUBLANES_F32
    per = tm // r
    last = m // r - 1
    vec = pl.BlockSpec((1, d), lambda i: (0, 0))
    return pl.pallas_call(
        _halo_kernel,
        grid=(nblk,),
        in_specs=[
            pl.BlockSpec((r, d), lambda i: (jnp.maximum(i * per - 1, 0), 0)),
            pl.BlockSpec((r, d), lambda i: (jnp.minimum((i + 1) * per, last), 0)),
            vec, vec, vec,
        ],
        out_specs=pl.BlockSpec((1, 2 * r, d), lambda i: (i, 0, 0)),
        out_shape=jax.ShapeDtypeStruct((nblk, 2 * r, d), BF16),
        compiler_params=_params(1, 16 * 1024 * 1024),
        name="halo_norm",
    )(x, x, nw, sh, sc)


def _inproj_kernel(a_ref, b_ref, nw_ref, o_ref, *, j_off, qk_lo, qk_hi):
    acc = _dot(a_ref[...], b_ref[...])
    j = pl.program_id(1) + j_off
    is_qk = jnp.logical_and(j >= qk_lo, j < qk_hi)

    @pl.when(is_qk)
    def _():
        for h in range(acc.shape[1] // HEAD_DIM):
            sl = slice(h * HEAD_DIM, (h + 1) * HEAD_DIM)
            xh = acc[:, sl]
            ms = jnp.mean(xh * xh, axis=-1, keepdims=True)
            o_ref[:, sl] = (xh * lax.rsqrt(ms + EPS) * nw_ref[0, :, sl]).astype(o_ref.dtype)

    @pl.when(jnp.logical_not(is_qk))
    def _():
        o_ref[...] = acc.astype(o_ref.dtype)


def _in_proj(h, w_in, nw_tiles, tn, j_off, n_tiles, qk_lo, qk_hi):
    m, d = h.shape
    tm = min(m, 1024)
    return pl.pallas_call(
        functools.partial(_inproj_kernel, j_off=j_off, qk_lo=qk_lo, qk_hi=qk_hi),
        grid=(m // tm, n_tiles),
        in_specs=[
            pl.BlockSpec((tm, d), lambda i, j: (i, 0)),
            pl.BlockSpec((d, tn), lambda i, j: (0, j + j_off)),
            pl.BlockSpec((1, 1, tn), lambda i, j: (j + j_off, 0, 0)),
        ],
        out_specs=pl.BlockSpec((tm, tn), lambda i, j: (i, j)),
        out_shape=jax.ShapeDtypeStruct((m, n_tiles * tn), BF16),
        compiler_params=_params(2, 4 * tm * d + 4 * d * tn + 4 * tm * tn + 3 * tm * tn * 4),
        name="in_proj",
    )(h, w_in, nw_tiles)


def _nat_kernel(q_ref, k_ref, v_ref, kc_ref, vc_ref, bias_ref, o_ref, *, grid_rows, scale):
    b = pl.program_id(1)
    ks = jnp.clip(b * ATTN_ROWS - WIN_ROWS // 2, 0, grid_rows - ATTN_KEY_ROWS)
    start = pl.multiple_of(ks * GRID_W, GRID_W)
    nkeys = ATTN_KEY_ROWS * GRID_W
    q = q_ref[...]
    kw = k_ref[pl.ds(start, nkeys), :]
    vw = v_ref[pl.ds(start, nkeys), :]
    s = _dot_nt(q, kw) * scale + bias_ref[0, 0]
    sc = _dot_nt(q, kc_ref[...]) * scale
    m = jnp.maximum(jnp.max(s, axis=-1, keepdims=True), jnp.max(sc, axis=-1, keepdims=True))
    p = jnp.exp(s - m)
    pc = jnp.exp(sc - m)
    l = jnp.sum(p, axis=-1, keepdims=True) + jnp.sum(pc, axis=-1, keepdims=True)
    o = _dot(p.astype(BF16), vw) + _dot(pc.astype(BF16), vc_ref[...])
    o_ref[...] = (o / l).astype(o_ref.dtype)


def _nat_bias_index(grid_rows):
    nq = ATTN_ROWS * GRID_W
    nk = ATTN_KEY_ROWS * GRID_W
    nb = grid_rows // ATTN_ROWS
    r_loc = (np.arange(nq) // GRID_W)[:, None]
    c = (np.arange(nq) % GRID_W)[:, None]
    kr_loc = (np.arange(nk) // GRID_W)[None, :]
    kc = (np.arange(nk) % GRID_W)[None, :]
    c_start = np.clip(c - WIN_COLS // 2, 0, GRID_W - WIN_COLS)
    col_ok = (kc >= c_start) & (kc < c_start + WIN_COLS)
    col_off = kc - c + (WIN_COLS - 1)
    ro, co, ok = [], [], []
    for b in (0, 1, nb - 1):
        ks = int(np.clip(b * ATTN_ROWS - WIN_ROWS // 2, 0, grid_rows - ATTN_KEY_ROWS))
        r = b * ATTN_ROWS + r_loc
        kr = ks + kr_loc
        r_start = np.clip(r - WIN_ROWS // 2, 0, grid_rows - WIN_ROWS)
        row_ok = (kr >= r_start) & (kr < r_start + WIN_ROWS)
        valid = row_ok & col_ok
        ro.append(np.where(valid, kr - r + (WIN_ROWS - 1), 0))
        co.append(np.where(valid, col_off, 0))
        ok.append(valid)
    return np.stack(ro), np.stack(co), np.stack(ok)


def _nat_bias(rpb, grid_rows):
    ro, co, ok = _nat_bias_index(grid_rows)
    bias = rpb[:, ro, co]
    bias = jnp.where(ok[None], bias, NEG_BIAS)
    return jnp.transpose(bias, (1, 0, 2, 3))


def _nat_attention(p_lat, p_ctx, bias, n_heads, q_blk, k_blk, v_blk, kc_blk, vc_blk):
    s_len = p_lat.shape[0]
    n_ctx = p_ctx.shape[0]
    grid_rows = s_len // GRID_W
    nb = grid_rows // ATTN_ROWS
    nq = ATTN_ROWS * GRID_W
    nk = ATTN_KEY_ROWS * GRID_W
    assert grid_rows % ATTN_ROWS == 0 and nb >= 3 and grid_rows >= ATTN_KEY_ROWS

    def variant(b):
        return jnp.where(b == 0, 0, jnp.where(b == nb - 1, 2, 1))

    return pl.pallas_call(
        functools.partial(_nat_kernel, grid_rows=grid_rows, scale=HEAD_DIM ** -0.5),
        grid=(n_heads, nb),
        in_specs=[
            pl.BlockSpec((nq, HEAD_DIM), lambda h, b: (b, q_blk + h)),
            pl.BlockSpec((s_len, HEAD_DIM), lambda h, b: (0, k_blk + h)),
            pl.BlockSpec((s_len, HEAD_DIM), lambda h, b: (0, v_blk + h)),
            pl.BlockSpec((n_ctx, HEAD_DIM), lambda h, b: (0, kc_blk + h)),
            pl.BlockSpec((n_ctx, HEAD_DIM), lambda h, b: (0, vc_blk + h)),
            pl.BlockSpec((1, 1, nq, nk), lambda h, b: (variant(b), h, 0, 0)),
        ],
        out_specs=pl.BlockSpec((nq, HEAD_DIM), lambda h, b: (b, h)),
        out_shape=jax.ShapeDtypeStruct((s_len, n_heads * HEAD_DIM), BF16),
        compiler_params=_params(2, 8 * s_len * HEAD_DIM + 2 * nq * nk * 4 + 16 * nq * (nk + n_ctx) * 4),
        name="nat_attention",
    )(p_lat, p_lat, p_lat, p_ctx, p_ctx, bias)


def _ctx_attn_kernel(q_ref, k_ref, v_ref, o_ref, *, scale):
    s = _dot_nt(q_ref[...], k_ref[...]) * scale
    m = jnp.max(s, axis=-1, keepdims=True)
    p = jnp.exp(s - m)
    l = jnp.sum(p, axis=-1, keepdims=True)
    o_ref[...] = (_dot(p.astype(BF16), v_ref[...]) / l).astype(o_ref.dtype)


def _ctx_attention(p_ctx, n_heads, q_blk, k_blk, v_blk):
    n = p_ctx.shape[0]
    return pl.pallas_call(
        functools.partial(_ctx_attn_kernel, scale=HEAD_DIM ** -0.5),
        grid=(n_heads,),
        in_specs=[
            pl.BlockSpec((n, HEAD_DIM), lambda h: (0, q_blk + h)),
            pl.BlockSpec((n, HEAD_DIM), lambda h: (0, k_blk + h)),
            pl.BlockSpec((n, HEAD_DIM), lambda h: (0, v_blk + h)),
        ],
        out_specs=pl.BlockSpec((n, HEAD_DIM), lambda h: (0, h)),
        out_shape=jax.ShapeDtypeStruct((n, n_heads * HEAD_DIM), BF16),
        compiler_params=_params(1, 16 * 1024 * 1024),
        name="ctx_attention",
    )(p_ctx, p_ctx, p_ctx)


def _pool_kernel(up_ref, um_ref, un_ref, pw_ref, ps_ref, o_ref, *, seq_len, gc):
    i = pl.program_id(0)
    n = pl.num_programs(0)
    tp = um_ref.shape[0]
    halo = up_ref.shape[0]
    um = um_ref[...]
    up = jnp.where(i > 0, up_ref[...], jnp.zeros_like(up_ref[...]))
    un = jnp.where(i < n - 1, un_ref[...], jnp.zeros_like(un_ref[...]))
    ue = jnp.concatenate([up, um, un], axis=0)
    trow = lax.broadcasted_iota(jnp.int32, (tp, tp + 2 * halo), 0)
    jcol = lax.broadcasted_iota(jnp.int32, (tp, tp + 2 * halo), 1)
    off = jcol - trow - halo
    t_abs = i * tp + lax.broadcasted_iota(jnp.int32, (tp, gc), 0)
    for g, w in enumerate(POOL_WINDOWS):
        lo_off = -(w // 2)
        hi_off = w - 1 - w // 2
        band = jnp.where(off >= lo_off, jnp.where(off <= hi_off, 1.0, 0.0), 0.0).astype(BF16)
        sl = slice(g * gc, (g + 1) * gc)
        wsum = _dot(band, ue[:, sl])
        lo = jnp.maximum(t_abs + lo_off, 0)
        hi = jnp.minimum(t_abs + hi_off, seq_len - 1)
        cnt = (hi - lo + 1).astype(F32)
        d = (wsum / cnt - um[:, sl].astype(F32)).astype(BF16)
        o_ref[:, sl] = (_dot(d, pw_ref[g]) * ps_ref[:, sl]).astype(o_ref.dtype)


def _pool_mix(p, col_blk, pool_w, pool_scale):
    m = p.shape[0]
    gc = pool_w.shape[-1]
    width = N_GROUPS * gc
    tp = min(m, 512)
    halo = SUBLANES_BF16
    per = tp // halo
    last = m // halo - 1
    return pl.pallas_call(
        functools.partial(_pool_kernel, seq_len=m, gc=gc),
        grid=(m // tp,),
        in_specs=[
            pl.BlockSpec((halo, width), lambda i: (jnp.maximum(i * per - 1, 0), col_blk)),
            pl.BlockSpec((tp, width), lambda i: (i, col_blk)),
            pl.BlockSpec((halo, width), lambda i: (jnp.minimum((i + 1) * per, last), col_blk)),
            pl.BlockSpec((N_GROUPS, gc, gc), lambda i: (0, 0, 0)),
            pl.BlockSpec((1, width), lambda i: (0, 0)),
        ],
        out_specs=pl.BlockSpec((tp, width), lambda i: (i, 0)),
        out_shape=jax.ShapeDtypeStruct((m, width), BF16),
        compiler_params=_params(1, 32 * 1024 * 1024),
        name="pool_mix",
    )(p, p, p, pool_w, pool_scale)


def _dft_tables(n):
    n2 = LANES if n % LANES == 0 else n
    n1 = n // n2
    k = jnp.arange(n, dtype=jnp.int32)[:, None]
    ia = (k * jnp.arange(n1, dtype=jnp.int32)[None, :]) % n1
    ib = (k * jnp.arange(n2, dtype=jnp.int32)[None, :]) % n
    ang_a = ia.astype(F32) * (2.0 * math.pi / n1)
    ang_b = ib.astype(F32) * (2.0 * math.pi / n)
    ca, sa = jnp.cos(ang_a)[:, :, None], jnp.sin(ang_a)[:, :, None]
    cb, sb = jnp.cos(ang_b)[:, None, :], jnp.sin(ang_b)[:, None, :]
    c = (ca * cb - sa * sb).reshape(n, n)
    s = (sa * cb + ca * sb).reshape(n, n)
    return c, s


def _fnet_chan_kernel(u_ref, cs_ref, o_ref, *, gc):
    for g in range(N_GROUPS):
        sl = slice(g * gc, (g + 1) * gc)
        o_ref[:, sl] = _dot(u_ref[:, sl], cs_ref[0]).astype(o_ref.dtype)


def _fnet_pos_kernel(t_ref, ab_ref, fw_ref, o_ref, acc_ref, *, scale, gc):
    k = pl.program_id(1)
    d = _dot(t_ref[...], ab_ref[...])

    @pl.when(k == 0)
    def _():
        acc_ref[...] = d

    @pl.when(k > 0)
    def _():
        acc_ref[...] += d

    @pl.when(k == pl.num_programs(1) - 1)
    def _():
        for g in range(N_GROUPS):
            sl = slice(g * gc, (g + 1) * gc)
            f = (acc_ref[:, sl] * scale).astype(BF16)
            o_ref[:, sl] = _dot(f, fw_ref[g]).astype(o_ref.dtype)


def _fourier_mix(p, col_blk, t_pos, cs_chan, fnet_w):
    m = p.shape[0]
    gc = fnet_w.shape[-1]
    width = N_GROUPS * gc
    tm = min(m, 1024)
    nblk = m // tm
    ab = pl.pallas_call(
        functools.partial(_fnet_chan_kernel, gc=gc),
        grid=(2, nblk),
        in_specs=[
            pl.BlockSpec((tm, width), lambda c, i: (i, col_blk)),
            pl.BlockSpec((1, gc, gc), lambda c, i: (c, 0, 0)),
        ],
        out_specs=pl.BlockSpec((tm, width), lambda c, i: (c * nblk + i, 0)),
        out_shape=jax.ShapeDtypeStruct((2 * m, width), BF16),
        compiler_params=_params(2, 32 * 1024 * 1024),
        name="fnet_chan",
    )(p, cs_chan)
    tk = min(2 * m, 2048)
    return pl.pallas_call(
        functools.partial(_fnet_pos_kernel, scale=1.0 / math.sqrt(m * gc), gc=gc),
        grid=(nblk, 2 * m // tk),
        in_specs=[
            pl.BlockSpec((tm, tk), lambda i, k: (i, k)),
            pl.BlockSpec((tk, width), lambda i, k: (k, 0)),
            pl.BlockSpec((N_GROUPS, gc, gc), lambda i, k: (0, 0, 0)),
        ],
        out_specs=pl.BlockSpec((tm, width), lambda i, k: (i, 0)),
        out_shape=jax.ShapeDtypeStruct((m, width), BF16),
        scratch_shapes=[pltpu.VMEM((tm, width), F32)],
        compiler_params=_params(2, 4 * tm * tk + 4 * tk * width + 5 * tm * width * 4),
        name="fnet_pos",
    )(t_pos, ab, fnet_w)


def _merge_kernel(h_ref, yp_ref, ya_ref, yf_ref, wg0, wg1, wg2, bg0, bg1, bg2, wp, wa, wf, o_ref):
    h = h_ref[...]

    def gate(w_ref, b_ref):
        return jax.nn.sigmoid(_dot(h, w_ref[...]) + b_ref[...])

    m = gate(wg0, bg0) * _dot(yp_ref[...], wp[...])
    m = m + gate(wg1, bg1) * _dot(ya_ref[...], wa[...])
    m = m + gate(wg2, bg2) * _dot(yf_ref[...], wf[...])
    o_ref[...] = m.astype(o_ref.dtype)


def _merge(h, y_pool, y_attn, y_fnet, w_gate, b_gate, w_bp, w_ba, w_bf):
    m, d = h.shape
    dp, da, df = y_pool.shape[1], y_attn.shape[1], y_fnet.shape[1]
    tm = min(m, 512)
    tn = _tile(d, 256)
    nj = d // tn

    def act(width):
        return pl.BlockSpec((tm, width), lambda i, j: (i, 0))

    def gate_w(b):
        return pl.BlockSpec((d, tn), lambda i, j: (0, b * nj + j))

    def gate_b(b):
        return pl.BlockSpec((1, tn), lambda i, j: (0, b * nj + j))

    def br_w(k):
        return pl.BlockSpec((k, tn), lambda i, j: (0, j))

    vmem = 4 * tm * (d + dp + da + df) + 4 * tn * (3 * d + dp + da + df) + 12 * tm * tn * 4
    return pl.pallas_call(
        _merge_kernel,
        grid=(m // tm, nj),
        in_specs=[act(d), act(dp), act(da), act(df), gate_w(0), gate_w(1), gate_w(2),
                  gate_b(0), gate_b(1), gate_b(2), br_w(dp), br_w(da), br_w(df)],
        out_specs=pl.BlockSpec((tm, tn), lambda i, j: (i, j)),
        out_shape=jax.ShapeDtypeStruct((m, d), BF16),
        compiler_params=_params(2, vmem),
        name="merge",
    )(h, y_pool, y_attn, y_fnet, w_gate, w_gate, w_gate, b_gate, b_gate, b_gate, w_bp, w_ba, w_bf)


def _resid_ksplit_kernel(a_ref, w_ref, x_ref, g_ref, o_ref, acc_ref):
    k = pl.program_id(2)
    d = _dot(a_ref[...], w_ref[...])

    @pl.when(k == 0)
    def _():
        acc_ref[...] = d

    @pl.when(k > 0)
    def _():
        acc_ref[...] += d

    @pl.when(k == pl.num_programs(2) - 1)
    def _():
        o_ref[...] = x_ref[...] + g_ref[...] * acc_ref[...]


def _resid_proj_ksplit(a, w, x, g, tk):
    m, kk = a.shape
    n = w.shape[1]
    tm = min(m, 1024)
    tn = _tile(n, 1024)
    return pl.pallas_call(
        _resid_ksplit_kernel,
        grid=(m // tm, n // tn, kk // tk),
        in_specs=[
            pl.BlockSpec((tm, tk), lambda i, j, k: (i, k)),
            pl.BlockSpec((tk, tn), lambda i, j, k: (k, j)),
            pl.BlockSpec((tm, tn), lambda i, j, k: (i, j)),
            pl.BlockSpec((1, tn), lambda i, j, k: (0, j)),
        ],
        out_specs=pl.BlockSpec((tm, tn), lambda i, j, k: (i, j)),
        out_shape=jax.ShapeDtypeStruct((m, n), F32),
        scratch_shapes=[pltpu.VMEM((tm, tn), F32)],
        compiler_params=_params(3, 4 * tm * tk + 4 * tk * tn + 7 * tm * tn * 4),
        name="resid_proj_ksplit",
    )(a, w, x, g)


def _ffn_up_kernel(a_ref, halo_ref, wg_ref, wv_ref, cwg_ref, cwv_ref, cbg_ref, cbv_ref, o_ref, a_ext):
    tm = a_ref.shape[0]

    @pl.when(pl.program_id(1) == 0)
    def _():
        a_ext[:tm, :] = a_ref[...]
        a_ext[tm:, :] = halo_ref[0]

    a = a_ext[...]
    row = lax.broadcasted_iota(jnp.int32, (tm, wg_ref.shape[1]), 0)

    def conv(w_ref, cw_ref, cb_ref):
        u = _dot(a, w_ref[...])
        um = u[:tm]
        prev_row = u[tm + SUBLANES_F32 - 1:tm + SUBLANES_F32]
        next_row = u[tm + SUBLANES_F32:tm + SUBLANES_F32 + 1]
        u_dn = jnp.where(row == 0, prev_row, pltpu.roll(um, 1, 0))
        u_up = jnp.where(row == tm - 1, next_row, pltpu.roll(um, tm - 1, 0))
        return u_dn * cw_ref[0:1] + um * cw_ref[1:2] + u_up * cw_ref[2:3] + cb_ref[...]

    gate = conv(wg_ref, cwg_ref, cbg_ref)
    val = conv(wv_ref, cwv_ref, cbv_ref)
    o_ref[...] = (gate * jax.nn.sigmoid(gate) * val).astype(o_ref.dtype)


def _ffn_up(h2, halo, w_up, conv_w, conv_b, f_pad):
    m, d = h2.shape
    tm = m // halo.shape[0]
    ext = halo.shape[1]
    tn = _tile(f_pad, 256)
    nj = f_pad // tn
    vmem = 4 * tm * d + 2 * (tm + ext) * d + 8 * d * tn + 4 * tm * tn + 16 * (tm + ext) * tn * 4
    return pl.pallas_call(
        _ffn_up_kernel,
        grid=(m // tm, nj),
        in_specs=[
            pl.BlockSpec((tm, d), lambda i, j: (i, 0)),
            pl.BlockSpec((1, ext, d), lambda i, j: (i, 0, 0)),
            pl.BlockSpec((d, tn), lambda i, j: (0, j)),
            pl.BlockSpec((d, tn), lambda i, j: (0, nj + j)),
            pl.BlockSpec((CONV_W, tn), lambda i, j: (0, j)),
            pl.BlockSpec((CONV_W, tn), lambda i, j: (0, nj + j)),
            pl.BlockSpec((1, tn), lambda i, j: (0, j)),
            pl.BlockSpec((1, tn), lambda i, j: (0, nj + j)),
        ],
        out_specs=pl.BlockSpec((tm, tn), lambda i, j: (i, j)),
        out_shape=jax.ShapeDtypeStruct((m, f_pad), BF16),
        scratch_shapes=[pltpu.VMEM((tm + ext, d), BF16)],
        compiler_params=_params(2, vmem),
        name="ffn_up",
    )(h2, halo, w_up, w_up, conv_w, conv_w, conv_b, conv_b)


def _pad_cols(w, f, f_pad):
    lead = w.shape[:-1]
    w = w.reshape(lead + (2, f))
    w = jnp.pad(w, [(0, 0)] * len(lead) + [(0, 0), (0, f_pad - f)])
    return w.reshape(lead + (2 * f_pad,))


def kernel(x, c, ctx, c_ctx, w_ada, b_ada, norm1_w, norm2_w, w_in, pool_w, pool_scale, q_norm_w, k_norm_w, rpb, fnet_w, w_gate, b_gate, w_br_pool, w_br_attn, w_br_fnet, w_o, w_up, conv_w, conv_b, w_down):
    n_layers, d, d_in = w_in.shape
    batch, s_len, _ = x.shape
    n_ctx = ctx.shape[1]
    assert batch == 1
    d_pool = w_br_pool.shape[1]
    d_attn = w_br_attn.shape[1]
    d_fnet = w_br_fnet.shape[1]
    n_heads = d_attn // HEAD_DIM
    f = w_down.shape[1]
    f_pad = _round_up(f, 1024)
    assert d_in == d_pool + 3 * d_attn + d_fnet and s_len % GRID_W == 0

    tn_in = _tile(d_pool, 1024)
    assert d_pool % tn_in == 0 and d_attn % tn_in == 0 and d_fnet == d_pool
    q_lo, k_lo, v_lo, f_lo = d_pool, d_pool + d_attn, d_pool + 2 * d_attn, d_pool + 3 * d_attn
    qk_tiles = (q_lo // tn_in, v_lo // tn_in)

    w_in_b = w_in.astype(BF16)
    w_gate_b = w_gate.astype(BF16)
    w_bp_b = w_br_pool.astype(BF16)
    w_ba_b = w_br_attn.astype(BF16)
    w_bf_b = w_br_fnet.astype(BF16)
    w_o_b = w_o.astype(BF16)
    w_up_b = _pad_cols(w_up.astype(BF16), f, f_pad)
    w_down_b = jnp.pad(w_down.astype(BF16), ((0, 0), (0, f_pad - f), (0, 0)))
    conv_w_p = _pad_cols(conv_w, f, f_pad)
    conv_b_p = _pad_cols(conv_b, f, f_pad)[:, None, :]
    pool_w_b = pool_w.astype(BF16)
    fnet_w_b = fnet_w.astype(BF16)
    ones_p = jnp.ones((d_pool,), F32)
    ones_vf = jnp.ones((d_attn + d_fnet,), F32)

    gc_f = fnet_w.shape[-1]
    cc, sc_ = _dft_tables(gc_f)
    cs_chan = jnp.stack([cc, sc_]).astype(BF16)

    def pos_table(n):
        cpos, spos = _dft_tables(n)
        return jnp.concatenate([cpos, -spos], axis=1).astype(BF16)

    t_lat = pos_table(s_len)
    t_ctx = pos_table(n_ctx)

    cond = jnp.concatenate([c.reshape(1, d), c_ctx.reshape(1, d), jnp.zeros((SUBLANES_F32 - 2, d), F32)], axis=0)
    mod = _ada(cond, w_ada, b_ada)

    tk_down = f_pad // 4 if (f_pad // 4) % LANES == 0 else f_pad
    tk_o = d // 2 if (d // 2) % LANES == 0 else d
    tm_ffn = min(s_len, 1024)

    xl = x.reshape(s_len, d)
    xc = ctx.reshape(n_ctx, d)
    for l in range(n_layers):
        last = l == n_layers - 1
        sh1, sc1, g1, sh2, sc2, g2 = [mod[l, 0:1, i * d:(i + 1) * d] for i in range(N_MOD)]
        csh1, csc1, cg1, csh2, csc2, cg2 = [mod[l, 1:2, i * d:(i + 1) * d] for i in range(N_MOD)]
        n1w = norm1_w[l].reshape(1, d)
        n2w = norm2_w[l].reshape(1, d)
        nw_tiles = jnp.concatenate(
            [ones_p, jnp.tile(q_norm_w[l], n_heads), jnp.tile(k_norm_w[l], n_heads), ones_vf]
        ).reshape(d_in // tn_in, 1, tn_in)
        bias = _nat_bias(rpb[l], s_len // GRID_W)
        b_gate_l = b_gate[l].reshape(1, -1)
        pool_scale_l = pool_scale[l].reshape(1, -1)

        h = _norm_mod(xl, n1w, sh1, sc1)
        hc = _norm_mod(xc, n1w, csh1, csc1)
        p = _in_proj(h, w_in_b[l], nw_tiles, tn_in, 0, d_in // tn_in, *qk_tiles)
        blk = lambda col: col // HEAD_DIM
        if last:
            j0 = k_lo // tn_in
            pc = _in_proj(hc, w_in_b[l], nw_tiles, tn_in, j0, (f_lo - k_lo) // tn_in, *qk_tiles)
            kc_blk, vc_blk = 0, blk(d_attn)
        else:
            pc = _in_proj(hc, w_in_b[l], nw_tiles, tn_in, 0, d_in // tn_in, *qk_tiles)
            kc_blk, vc_blk = blk(k_lo), blk(v_lo)
        y_attn = _nat_attention(p, pc, bias, n_heads, blk(q_lo), blk(k_lo), blk(v_lo), kc_blk, vc_blk)
        y_pool = _pool_mix(p, 0, pool_w_b[l], pool_scale_l)
        y_fnet = _fourier_mix(p, f_lo // d_fnet, t_lat, cs_chan, fnet_w_b[l])
        m = _merge(h, y_pool, y_attn, y_fnet, w_gate_b[l], b_gate_l, w_bp_b[l], w_ba_b[l], w_bf_b[l])
        xl = _resid_proj_ksplit(m, w_o_b[l], xl, g1, tk_o)

        h2 = _norm_mod(xl, n2w, sh2, sc2)
        halo = _halo_norm(xl, n2w, sh2, sc2, tm_ffn)
        t = _ffn_up(h2, halo, w_up_b[l], conv_w_p[l], conv_b_p[l], f_pad)
        xl = _resid_proj_ksplit(t, w_down_b[l], xl, g2, tk_down)

        if not last:
            yc_attn = _ctx_attention(pc, n_heads, blk(q_lo), blk(k_lo), blk(v_lo))
            yc_pool = _pool_mix(pc, 0, pool_w_b[l], pool_scale_l)
            yc_fnet = _fourier_mix(pc, f_lo // d_fnet, t_ctx, cs_chan, fnet_w_b[l])
            mc = _merge(hc, yc_pool, yc_attn, yc_fnet, w_gate_b[l], b_gate_l, w_bp_b[l], w_ba_b[l], w_bf_b[l])
            xc = _resid_proj_ksplit(mc, w_o_b[l], xc, cg1, tk_o)
            hc2 = _norm_mod(xc, n2w, csh2, csc2)
            halo_c = _halo_norm(xc, n2w, csh2, csc2, n_ctx)
            tc = _ffn_up(hc2, halo_c, w_up_b[l], conv_w_p[l], conv_b_p[l], f_pad)
            xc = _resid_proj_ksplit(tc, w_down_b[l], xc, cg2, tk_down)
    return xl.reshape(batch, s_len, d)
```

```python
import functools
import math

import numpy as np
import jax
import jax.numpy as jnp
from jax import lax
from jax.experimental import pallas as pl
from jax.experimental.pallas import tpu as pltpu

F32 = jnp.float32
BF16 = jnp.bfloat16

GRID_W = 64
HEAD_DIM = 128
WIN_ROWS = 8
WIN_COLS = 16
POOL_WINDOWS = (2, 4, 8, 16)
N_GROUPS = 4
N_MOD = 6
CONV_W = 3
EPS = 1e-6

LANES = 128
SUBLANES_F32 = 8
SUBLANES_BF16 = 16
VMEM_CAP_BYTES = 56 * 1024 * 1024

ATTN_ROWS = 4
ATTN_KEY_ROWS = ATTN_ROWS + WIN_ROWS
NEG_BIAS = -1e30
FFT_MIN_LEN = 1024


def _tile(n, pref):
    if n <= pref:
        return n
    t = pref - pref % LANES
    while t >= LANES:
        if n % t == 0:
            return t
        t -= LANES
    return n


def _params(n_axes, vmem_bytes):
    return pltpu.CompilerParams(
        dimension_semantics=("arbitrary",) * n_axes,
        vmem_limit_bytes=int(min(VMEM_CAP_BYTES, max(vmem_bytes, 16 * 1024 * 1024))),
    )


def _dot(a, b):
    return jnp.dot(a, b, preferred_element_type=F32)


def _dot_nt(a, b):
    return lax.dot_general(a, b, (((1,), (1,)), ((), ())), preferred_element_type=F32)


def _rms_mod(x, nw, sh, sc):
    ms = jnp.mean(x * x, axis=-1, keepdims=True)
    return (x * lax.rsqrt(ms + EPS) * nw) * (1.0 + sc) + sh


def _ada_kernel(s_ref, w_ref, b_ref, o_ref):
    s = s_ref[...]
    a = (s * jax.nn.sigmoid(s)).astype(BF16)
    o_ref[0] = _dot(a, w_ref[0].astype(BF16)) + b_ref[0]


def _ada(cond, w_ada, b_ada):
    n_layers, d, n = w_ada.shape
    tn = _tile(n, 1024)
    return pl.pallas_call(
        _ada_kernel,
        grid=(n_layers, n // tn),
        in_specs=[
            pl.BlockSpec((SUBLANES_F32, d), lambda l, j: (0, 0)),
            pl.BlockSpec((1, d, tn), lambda l, j: (l, 0, j)),
            pl.BlockSpec((1, 1, tn), lambda l, j: (l, 0, j)),
        ],
        out_specs=pl.BlockSpec((1, SUBLANES_F32, tn), lambda l, j: (l, 0, j)),
        out_shape=jax.ShapeDtypeStruct((n_layers, SUBLANES_F32, n), F32),
        compiler_params=_params(2, 2 * d * tn * 4 + 8 * 1024 * 1024),
        name="ada",
    )(cond, w_ada, b_ada.reshape(n_layers, 1, n))


def _norm_kernel(x_ref, nw_ref, sh_ref, sc_ref, o_ref):
    o_ref[...] = _rms_mod(x_ref[...], nw_ref[...], sh_ref[...], sc_ref[...]).astype(o_ref.dtype)


def _norm_mod(x, nw, sh, sc):
    m, d = x.shape
    tm = min(m, 512)
    vec = pl.BlockSpec((1, d), lambda i: (0, 0))
    return pl.pallas_call(
        _norm_kernel,
        grid=(m // tm,),
        in_specs=[pl.BlockSpec((tm, d), lambda i: (i, 0)), vec, vec, vec],
        out_specs=pl.BlockSpec((tm, d), lambda i: (i, 0)),
        out_shape=jax.ShapeDtypeStruct((m, d), BF16),
        compiler_params=_params(1, 6 * tm * d * 4),
        name="norm_mod",
    )(x, nw, sh, sc)


def _halo_kernel(xp_ref, xn_ref, nw_ref, sh_ref, sc_ref, o_ref):
    i = pl.program_id(0)
    n = pl.num_programs(0)
    hp = _rms_mod(xp_ref[...], nw_ref[...], sh_ref[...], sc_ref[...])
    hn = _rms_mod(xn_ref[...], nw_ref[...], sh_ref[...], sc_ref[...])
    hp = jnp.where(i > 0, hp, 0.0)
    hn = jnp.where(i < n - 1, hn, 0.0)
    o_ref[0] = jnp.concatenate([hp, hn], axis=0).astype(o_ref.dtype)


def _halo_norm(x, nw, sh, sc, tm):
    m, d = x.shape
    nblk = m // tm
    r = SUBLANES_F32
    per = tm // r
    last = m // r - 1
    vec = pl.BlockSpec((1, d), lambda i: (0, 0))
    return pl.pallas_call(
        _halo_kernel,
        grid=(nblk,),
        in_specs=[
            pl.BlockSpec((r, d), lambda i: (jnp.maximum(i * per - 1, 0), 0)),
            pl.BlockSpec((r, d), lambda i: (jnp.minimum((i + 1) * per, last), 0)),
            vec, vec, vec,
        ],
        out_specs=pl.BlockSpec((1, 2 * r, d), lambda i: (i, 0, 0)),
        out_shape=jax.ShapeDtypeStruct((nblk, 2 * r, d), BF16),
        compiler_params=_params(1, 16 * 1024 * 1024),
        name="halo_norm",
    )(x, x, nw, sh, sc)


def _inproj_kernel(a_ref, b_ref, nw_ref, o_ref, *, j_off, qk_lo, qk_hi):
    acc = _dot(a_ref[...], b_ref[...])
    j = pl.program_id(1) + j_off
    is_qk = jnp.logical_and(j >= qk_lo, j < qk_hi)

    @pl.when(is_qk)
    def _():
        for h in range(acc.shape[1] // HEAD_DIM):
            sl = slice(h * HEAD_DIM, (h + 1) * HEAD_DIM)
            xh = acc[:, sl]
            ms = jnp.mean(xh * xh, axis=-1, keepdims=True)
            o_ref[:, sl] = (xh * lax.rsqrt(ms + EPS) * nw_ref[0, :, sl]).astype(o_ref.dtype)

    @pl.when(jnp.logical_not(is_qk))
    def _():
        o_ref[...] = acc.astype(o_ref.dtype)


def _in_proj(h, w_in, layer, nw_tiles, tn, j_off, n_tiles, qk_lo, qk_hi):
    m, d = h.shape
    tm = min(m, 1024)
    return pl.pallas_call(
        functools.partial(_inproj_kernel, j_off=j_off, qk_lo=qk_lo, qk_hi=qk_hi),
        grid=(m // tm, n_tiles),
        in_specs=[
            pl.BlockSpec((tm, d), lambda i, j: (i, 0)),
            pl.BlockSpec((None, d, tn), lambda i, j: (layer, 0, j + j_off)),
            pl.BlockSpec((1, 1, tn), lambda i, j: (j + j_off, 0, 0)),
        ],
        out_specs=pl.BlockSpec((tm, tn), lambda i, j: (i, j)),
        out_shape=jax.ShapeDtypeStruct((m, n_tiles * tn), BF16),
        compiler_params=_params(2, 4 * tm * d + 4 * d * tn + 4 * tm * tn + 3 * tm * tn * 4),
        name="in_proj",
    )(h, w_in, nw_tiles)


def _nat_kernel(q_ref, k_ref, v_ref, kc_ref, vc_ref, bias_ref, o_ref, *, grid_rows, scale):
    b = pl.program_id(1)
    ks = jnp.clip(b * ATTN_ROWS - WIN_ROWS // 2, 0, grid_rows - ATTN_KEY_ROWS)
    start = pl.multiple_of(ks * GRID_W, GRID_W)
    nkeys = ATTN_KEY_ROWS * GRID_W
    q = q_ref[...]
    kw = k_ref[pl.ds(start, nkeys), :]
    vw = v_ref[pl.ds(start, nkeys), :]
    s = _dot_nt(q, kw) * scale + bias_ref[0, 0]
    sc = _dot_nt(q, kc_ref[...]) * scale
    m = jnp.maximum(jnp.max(s, axis=-1, keepdims=True), jnp.max(sc, axis=-1, keepdims=True))
    p = jnp.exp(s - m)
    pc = jnp.exp(sc - m)
    l = jnp.sum(p, axis=-1, keepdims=True) + jnp.sum(pc, axis=-1, keepdims=True)
    o = _dot(p.astype(BF16), vw) + _dot(pc.astype(BF16), vc_ref[...])
    o_ref[...] = (o / l).astype(o_ref.dtype)


def _nat_bias(rpb, grid_rows):
    n_heads = rpb.shape[0]
    nb = grid_rows // ATTN_ROWS
    c = np.arange(GRID_W)[:, None]
    kc = np.arange(GRID_W)[None, :]
    c_start = np.clip(c - WIN_COLS // 2, 0, GRID_W - WIN_COLS)
    col_ok = (kc >= c_start) & (kc < c_start + WIN_COLS)
    dc = kc - c + (WIN_COLS - 1)
    n_dc = 2 * WIN_COLS - 1
    expand = ((dc[None] == np.arange(n_dc)[:, None, None]) & col_ok[None]).astype(np.float32)
    blocks = jnp.einsum("hrd,dn->hrn", rpb, expand.reshape(n_dc, GRID_W * GRID_W),
                        precision=lax.Precision.HIGHEST)
    blocks = blocks.reshape(n_heads, 2 * WIN_ROWS - 1, GRID_W, GRID_W)
    blocks = jnp.where(col_ok, blocks, NEG_BIAS)
    masked = jnp.full((n_heads, GRID_W, GRID_W), NEG_BIAS, F32)
    variants = []
    for b in (0, 1, nb - 1):
        ks = int(np.clip(b * ATTN_ROWS - WIN_ROWS // 2, 0, grid_rows - ATTN_KEY_ROWS))
        q_rows = []
        for r_loc in range(ATTN_ROWS):
            r = b * ATTN_ROWS + r_loc
            r_start = int(np.clip(r - WIN_ROWS // 2, 0, grid_rows - WIN_ROWS))
            row = []
            for kr in range(ks, ks + ATTN_KEY_ROWS):
                ok = r_start <= kr < r_start + WIN_ROWS
                row.append(blocks[:, kr - r + (WIN_ROWS - 1)] if ok else masked)
            q_rows.append(jnp.concatenate(row, axis=-1))
        variants.append(jnp.concatenate(q_rows, axis=-2))
    return jnp.stack(variants)


def _nat_attention(p_lat, p_ctx, bias, n_heads, q_blk, k_blk, v_blk, kc_blk, vc_blk):
    s_len = p_lat.shape[0]
    n_ctx = p_ctx.shape[0]
    grid_rows = s_len // GRID_W
    nb = grid_rows // ATTN_ROWS
    nq = ATTN_ROWS * GRID_W
    nk = ATTN_KEY_ROWS * GRID_W
    assert grid_rows % ATTN_ROWS == 0 and nb >= 3 and grid_rows >= ATTN_KEY_ROWS

    def variant(b):
        return jnp.where(b == 0, 0, jnp.where(b == nb - 1, 2, 1))

    return pl.pallas_call(
        functools.partial(_nat_kernel, grid_rows=grid_rows, scale=HEAD_DIM ** -0.5),
        grid=(n_heads, nb),
        in_specs=[
            pl.BlockSpec((nq, HEAD_DIM), lambda h, b: (b, q_blk + h)),
            pl.BlockSpec((s_len, HEAD_DIM), lambda h, b: (0, k_blk + h)),
            pl.BlockSpec((s_len, HEAD_DIM), lambda h, b: (0, v_blk + h)),
            pl.BlockSpec((n_ctx, HEAD_DIM), lambda h, b: (0, kc_blk + h)),
            pl.BlockSpec((n_ctx, HEAD_DIM), lambda h, b: (0, vc_blk + h)),
            pl.BlockSpec((1, 1, nq, nk), lambda h, b: (variant(b), h, 0, 0)),
        ],
        out_specs=pl.BlockSpec((nq, HEAD_DIM), lambda h, b: (b, h)),
        out_shape=jax.ShapeDtypeStruct((s_len, n_heads * HEAD_DIM), BF16),
        compiler_params=_params(2, 8 * s_len * HEAD_DIM + 2 * nq * nk * 4 + 16 * nq * (nk + n_ctx) * 4),
        name="nat_attention",
    )(p_lat, p_lat, p_lat, p_ctx, p_ctx, bias)


def _ctx_attn_kernel(q_ref, k_ref, v_ref, o_ref, *, scale):
    s = _dot_nt(q_ref[...], k_ref[...]) * scale
    m = jnp.max(s, axis=-1, keepdims=True)
    p = jnp.exp(s - m)
    l = jnp.sum(p, axis=-1, keepdims=True)
    o_ref[...] = (_dot(p.astype(BF16), v_ref[...]) / l).astype(o_ref.dtype)


def _ctx_attention(p_ctx, n_heads, q_blk, k_blk, v_blk):
    n = p_ctx.shape[0]
    return pl.pallas_call(
        functools.partial(_ctx_attn_kernel, scale=HEAD_DIM ** -0.5),
        grid=(n_heads,),
        in_specs=[
            pl.BlockSpec((n, HEAD_DIM), lambda h: (0, q_blk + h)),
            pl.BlockSpec((n, HEAD_DIM), lambda h: (0, k_blk + h)),
            pl.BlockSpec((n, HEAD_DIM), lambda h: (0, v_blk + h)),
        ],
        out_specs=pl.BlockSpec((n, HEAD_DIM), lambda h: (0, h)),
        out_shape=jax.ShapeDtypeStruct((n, n_heads * HEAD_DIM), BF16),
        compiler_params=_params(1, 16 * 1024 * 1024),
        name="ctx_attention",
    )(p_ctx, p_ctx, p_ctx)


def _pool_kernel(up_ref, um_ref, un_ref, pw_ref, ps_ref, o_ref, *, seq_len, gc):
    i = pl.program_id(0)
    n = pl.num_programs(0)
    tp = um_ref.shape[0]
    halo = up_ref.shape[0]
    um = um_ref[...]
    up = jnp.where(i > 0, up_ref[...], jnp.zeros_like(up_ref[...]))
    un = jnp.where(i < n - 1, un_ref[...], jnp.zeros_like(un_ref[...]))
    ue = jnp.concatenate([up, um, un], axis=0)
    trow = lax.broadcasted_iota(jnp.int32, (tp, tp + 2 * halo), 0)
    jcol = lax.broadcasted_iota(jnp.int32, (tp, tp + 2 * halo), 1)
    off = jcol - trow - halo
    t_abs = i * tp + lax.broadcasted_iota(jnp.int32, (tp, gc), 0)
    for g, w in enumerate(POOL_WINDOWS):
        lo_off = -(w // 2)
        hi_off = w - 1 - w // 2
        band = jnp.where(off >= lo_off, jnp.where(off <= hi_off, 1.0, 0.0), 0.0).astype(BF16)
        sl = slice(g * gc, (g + 1) * gc)
        wsum = _dot(band, ue[:, sl])
        lo = jnp.maximum(t_abs + lo_off, 0)
        hi = jnp.minimum(t_abs + hi_off, seq_len - 1)
        cnt = (hi - lo + 1).astype(F32)
        d = (wsum / cnt - um[:, sl].astype(F32)).astype(BF16)
        o_ref[:, sl] = (_dot(d, pw_ref[g]) * ps_ref[:, sl]).astype(o_ref.dtype)


def _pool_mix(p, col_blk, pool_w, pool_scale):
    m = p.shape[0]
    gc = pool_w.shape[-1]
    width = N_GROUPS * gc
    tp = min(m, 512)
    halo = SUBLANES_BF16
    per = tp // halo
    last = m // halo - 1
    return pl.pallas_call(
        functools.partial(_pool_kernel, seq_len=m, gc=gc),
        grid=(m // tp,),
        in_specs=[
            pl.BlockSpec((halo, width), lambda i: (jnp.maximum(i * per - 1, 0), col_blk)),
            pl.BlockSpec((tp, width), lambda i: (i, col_blk)),
            pl.BlockSpec((halo, width), lambda i: (jnp.minimum((i + 1) * per, last), col_blk)),
            pl.BlockSpec((N_GROUPS, gc, gc), lambda i: (0, 0, 0)),
            pl.BlockSpec((1, width), lambda i: (0, 0)),
        ],
        out_specs=pl.BlockSpec((tp, width), lambda i: (i, 0)),
        out_shape=jax.ShapeDtypeStruct((m, width), BF16),
        compiler_params=_params(1, 32 * 1024 * 1024),
        name="pool_mix",
    )(p, p, p, pool_w, pool_scale)


def _cos_sin(num, den):
    ang = (num % den).astype(F32) * (2.0 * math.pi / den)
    return jnp.cos(ang), jnp.sin(ang)


def _dft_tables(n):
    idx = jnp.arange(n, dtype=jnp.int32)
    return _cos_sin(idx[:, None] * idx[None, :], n)


def _fnet_chan_kernel(u_ref, cs_ref, o_ref, *, gc):
    for g in range(N_GROUPS):
        sl = slice(g * gc, (g + 1) * gc)
        o_ref[:, sl] = _dot(u_ref[:, sl], cs_ref[0]).astype(o_ref.dtype)


def _fnet_pos_kernel(t_ref, ab_ref, fw_ref, o_ref, *, scale, gc):
    acc = _dot(t_ref[...], ab_ref[...])
    for g in range(N_GROUPS):
        sl = slice(g * gc, (g + 1) * gc)
        f = (acc[:, sl] * scale).astype(BF16)
        o_ref[:, sl] = _dot(f, fw_ref[g]).astype(o_ref.dtype)


def _fourier_mix_dense(p, col_blk, cs_chan, fnet_w):
    m = p.shape[0]
    gc = fnet_w.shape[-1]
    width = N_GROUPS * gc
    cpos, spos = _dft_tables(m)
    t_pos = jnp.concatenate([cpos, -spos], axis=1).astype(BF16)
    ab = pl.pallas_call(
        functools.partial(_fnet_chan_kernel, gc=gc),
        grid=(2,),
        in_specs=[
            pl.BlockSpec((m, width), lambda c: (0, col_blk)),
            pl.BlockSpec((1, gc, gc), lambda c: (c, 0, 0)),
        ],
        out_specs=pl.BlockSpec((m, width), lambda c: (c, 0)),
        out_shape=jax.ShapeDtypeStruct((2 * m, width), BF16),
        compiler_params=_params(1, 32 * 1024 * 1024),
        name="fnet_chan",
    )(p, cs_chan)
    return pl.pallas_call(
        functools.partial(_fnet_pos_kernel, scale=1.0 / math.sqrt(m * gc), gc=gc),
        grid=(1,),
        in_specs=[
            pl.BlockSpec((m, 2 * m), lambda i: (0, 0)),
            pl.BlockSpec((2 * m, width), lambda i: (0, 0)),
            pl.BlockSpec((N_GROUPS, gc, gc), lambda i: (0, 0, 0)),
        ],
        out_specs=pl.BlockSpec((m, width), lambda i: (0, 0)),
        out_shape=jax.ShapeDtypeStruct((m, width), BF16),
        compiler_params=_params(1, 32 * 1024 * 1024),
        name="fnet_pos",
    )(t_pos, ab, fnet_w)


def _fft_stage1_kernel(f1_ref, u_ref, tc_ref, ts_ref, zr_ref, zi_ref, *, n1):
    y = _dot(f1_ref[...], u_ref[...])
    yr, yi = y[:n1], y[n1:]
    reps = u_ref.shape[1] // LANES
    tc = jnp.tile(tc_ref[...], (1, reps))
    ts = jnp.tile(ts_ref[...], (1, reps))
    zr_ref[...] = (yr * tc + yi * ts).astype(zr_ref.dtype)
    zi_ref[...] = (yi * tc - yr * ts).astype(zi_ref.dtype)


def _fft_stage2_kernel(fa_ref, fb_ref, zr_ref, zi_ref, cc_ref, sc_ref, fw_ref, o_ref, *, n2, gc, chunks, scale):
    width = N_GROUPS * gc
    p = _dot(fa_ref[...], zr_ref[...]) + _dot(fb_ref[...], zi_ref[...])
    for g in range(N_GROUPS):
        cols = [slice(ch * width + g * gc, ch * width + (g + 1) * gc) for ch in range(chunks)]
        pr = jnp.concatenate([p[:n2, c] for c in cols], axis=0).astype(BF16)
        pi = jnp.concatenate([p[n2:, c] for c in cols], axis=0).astype(BF16)
        f = ((_dot(pr, cc_ref[...]) + _dot(pi, sc_ref[...])) * scale).astype(BF16)
        y = _dot(f, fw_ref[g]).astype(o_ref.dtype)
        for ch, c in enumerate(cols):
            o_ref[:, c] = y[ch * n2:(ch + 1) * n2]


def _fourier_mix_fft(p, col_blk, cs_chan, fnet_w):
    m, d_in = p.shape
    gc = fnet_w.shape[-1]
    width = N_GROUPS * gc
    n2 = LANES
    n1 = m // n2
    i1 = jnp.arange(n1, dtype=jnp.int32)
    i2 = jnp.arange(n2, dtype=jnp.int32)
    c1, s1 = _cos_sin(i1[:, None] * i1[None, :], n1)
    f1 = jnp.concatenate([c1, -s1], axis=0).astype(BF16)
    tc, ts = _cos_sin(i1[:, None] * i2[None, :], m)
    tc = jnp.repeat(tc, LANES, axis=1)
    ts = jnp.repeat(ts, LANES, axis=1)
    c2, s2 = _cos_sin(i2[:, None] * i2[None, :], n2)
    fa = jnp.concatenate([c2, -s2], axis=0).astype(BF16)
    fb = jnp.concatenate([s2, c2], axis=0).astype(BF16)
    per_row = d_in // width
    z_shape = jax.ShapeDtypeStruct((n2 * n1, width), BF16)
    zr, zi = pl.pallas_call(
        functools.partial(_fft_stage1_kernel, n1=n1),
        grid=(n2,),
        in_specs=[
            pl.BlockSpec((2 * n1, n1), lambda j: (0, 0)),
            pl.BlockSpec((n1, width), lambda j: (0, j * per_row + col_blk)),
            pl.BlockSpec((n1, LANES), lambda j: (0, j)),
            pl.BlockSpec((n1, LANES), lambda j: (0, j)),
        ],
        out_specs=[pl.BlockSpec((n1, width), lambda j: (j, 0))] * 2,
        out_shape=[z_shape, z_shape],
        compiler_params=_params(1, 16 * 1024 * 1024),
        name="fft_stage1",
    )(f1, p.reshape(n1, n2 * d_in), tc, ts)
    chunks = min(n1, 8)
    tn = chunks * width
    y = pl.pallas_call(
        functools.partial(_fft_stage2_kernel, n2=n2, gc=gc, chunks=chunks, scale=1.0 / math.sqrt(m * gc)),
        grid=(n1 // chunks,),
        in_specs=[
            pl.BlockSpec((2 * n2, n2), lambda j: (0, 0)),
            pl.BlockSpec((2 * n2, n2), lambda j: (0, 0)),
            pl.BlockSpec((n2, tn), lambda j: (0, j)),
            pl.BlockSpec((n2, tn), lambda j: (0, j)),
            pl.BlockSpec((None, gc, gc), lambda j: (0, 0, 0)),
            pl.BlockSpec((None, gc, gc), lambda j: (1, 0, 0)),
            pl.BlockSpec((N_GROUPS, gc, gc), lambda j: (0, 0, 0)),
        ],
        out_specs=pl.BlockSpec((n2, tn), lambda j: (0, j)),
        out_shape=jax.ShapeDtypeStruct((n2, n1 * width), BF16),
        compiler_params=_params(1, 40 * 1024 * 1024),
        name="fft_stage2",
    )(fa, fb, zr.reshape(n2, n1 * width), zi.reshape(n2, n1 * width), cs_chan, cs_chan, fnet_w)
    return y.reshape(m, width)


def _fourier_mix(p, col_blk, cs_chan, fnet_w):
    if p.shape[0] >= FFT_MIN_LEN and p.shape[0] % (LANES * SUBLANES_F32) == 0:
        return _fourier_mix_fft(p, col_blk, cs_chan, fnet_w)
    return _fourier_mix_dense(p, col_blk, cs_chan, fnet_w)


def _merge_kernel(h_ref, yp_ref, ya_ref, yf_ref, wg0, wg1, wg2, bg0, bg1, bg2, wp, wa, wf, o_ref):
    h = h_ref[...]

    def gate(w_ref, b_ref):
        return jax.nn.sigmoid(_dot(h, w_ref[...]) + b_ref[...])

    m = gate(wg0, bg0) * _dot(yp_ref[...], wp[...])
    m = m + gate(wg1, bg1) * _dot(ya_ref[...], wa[...])
    m = m + gate(wg2, bg2) * _dot(yf_ref[...], wf[...])
    o_ref[...] = m.astype(o_ref.dtype)


def _merge(h, y_pool, y_attn, y_fnet, layer, w_gate, b_gate, w_bp, w_ba, w_bf):
    m, d = h.shape
    dp, da, df = y_pool.shape[1], y_attn.shape[1], y_fnet.shape[1]
    tm = min(m, 512)
    tn = _tile(d, 256)
    nj = d // tn

    def act(width):
        return pl.BlockSpec((tm, width), lambda i, j: (i, 0))

    def gate_w(b):
        return pl.BlockSpec((None, d, tn), lambda i, j: (layer, 0, b * nj + j))

    def gate_b(b):
        return pl.BlockSpec((None, 1, tn), lambda i, j: (layer, 0, b * nj + j))

    def br_w(k):
        return pl.BlockSpec((None, k, tn), lambda i, j: (layer, 0, j))

    vmem = 4 * tm * (d + dp + da + df) + 4 * tn * (3 * d + dp + da + df) + 12 * tm * tn * 4
    return pl.pallas_call(
        _merge_kernel,
        grid=(m // tm, nj),
        in_specs=[act(d), act(dp), act(da), act(df), gate_w(0), gate_w(1), gate_w(2),
                  gate_b(0), gate_b(1), gate_b(2), br_w(dp), br_w(da), br_w(df)],
        out_specs=pl.BlockSpec((tm, tn), lambda i, j: (i, j)),
        out_shape=jax.ShapeDtypeStruct((m, d), BF16),
        compiler_params=_params(2, vmem),
        name="merge",
    )(h, y_pool, y_attn, y_fnet, w_gate, w_gate, w_gate, b_gate, b_gate, b_gate, w_bp, w_ba, w_bf)


def _resid_kernel(a_ref, w_ref, x_ref, g_ref, o_ref, acc_ref):
    k = pl.program_id(2)
    d = _dot(a_ref[...], w_ref[...])

    @pl.when(k == 0)
    def _():
        acc_ref[...] = d

    @pl.when(k > 0)
    def _():
        acc_ref[...] += d

    @pl.when(k == pl.num_programs(2) - 1)
    def _():
        o_ref[...] = x_ref[...] + g_ref[...] * acc_ref[...]


def _resid_proj(a, w, layer, x, g, tk, tn_pref):
    m, kk = a.shape
    n = w.shape[2]
    tm = min(m, 1024)
    tn = _tile(n, tn_pref)
    return pl.pallas_call(
        _resid_kernel,
        grid=(m // tm, n // tn, kk // tk),
        in_specs=[
            pl.BlockSpec((tm, tk), lambda i, j, k: (i, k)),
            pl.BlockSpec((None, tk, tn), lambda i, j, k: (layer, k, j)),
            pl.BlockSpec((tm, tn), lambda i, j, k: (i, j)),
            pl.BlockSpec((1, tn), lambda i, j, k: (0, j)),
        ],
        out_specs=pl.BlockSpec((tm, tn), lambda i, j, k: (i, j)),
        out_shape=jax.ShapeDtypeStruct((m, n), F32),
        scratch_shapes=[pltpu.VMEM((tm, tn), F32)],
        compiler_params=_params(3, 4 * tm * tk + 4 * tk * tn + 7 * tm * tn * 4),
        name="resid_proj",
    )(a, w, x, g)


def _ffn_up_kernel(a_ref, halo_ref, wg_ref, wv_ref, cwg_ref, cwv_ref, cbg_ref, cbv_ref, o_ref, a_ext):
    tm = a_ref.shape[0]

    @pl.when(pl.program_id(1) == 0)
    def _():
        a_ext[:tm, :] = a_ref[...]
        a_ext[tm:, :] = halo_ref[0]

    a = a_ext[...]
    row = lax.broadcasted_iota(jnp.int32, (tm, wg_ref.shape[1]), 0)

    def conv(w_ref, cw_ref, cb_ref):
        u = _dot(a, w_ref[...])
        um = u[:tm]
        prev_row = u[tm + SUBLANES_F32 - 1:tm + SUBLANES_F32]
        next_row = u[tm + SUBLANES_F32:tm + SUBLANES_F32 + 1]
        u_dn = jnp.where(row == 0, prev_row, pltpu.roll(um, 1, 0))
        u_up = jnp.where(row == tm - 1, next_row, pltpu.roll(um, tm - 1, 0))
        return u_dn * cw_ref[0:1] + um * cw_ref[1:2] + u_up * cw_ref[2:3] + cb_ref[...]

    gate = conv(wg_ref, cwg_ref, cbg_ref)
    val = conv(wv_ref, cwv_ref, cbv_ref)
    o_ref[...] = (gate * jax.nn.sigmoid(gate) * val).astype(o_ref.dtype)


def _ffn_up(h2, halo, layer, w_up, conv_w, conv_b):
    m, d = h2.shape
    f = w_up.shape[2] // 2
    tm = m // halo.shape[0]
    ext = halo.shape[1]
    tn = _tile(f, 256)
    nj = f // tn
    vmem = 4 * tm * d + 2 * (tm + ext) * d + 8 * d * tn + 4 * tm * tn + 16 * (tm + ext) * tn * 4

    def cols(rows, half):
        return pl.BlockSpec((None, rows, tn), lambda i, j: (layer, 0, half * nj + j))

    return pl.pallas_call(
        _ffn_up_kernel,
        grid=(m // tm, nj),
        in_specs=[
            pl.BlockSpec((tm, d), lambda i, j: (i, 0)),
            pl.BlockSpec((1, ext, d), lambda i, j: (i, 0, 0)),
            cols(d, 0), cols(d, 1), cols(CONV_W, 0), cols(CONV_W, 1), cols(1, 0), cols(1, 1),
        ],
        out_specs=pl.BlockSpec((tm, tn), lambda i, j: (i, j)),
        out_shape=jax.ShapeDtypeStruct((m, f), BF16),
        scratch_shapes=[pltpu.VMEM((tm + ext, d), BF16)],
        compiler_params=_params(2, vmem),
        name="ffn_up",
    )(h2, halo, w_up, w_up, conv_w, conv_w, conv_b, conv_b)


def kernel(x, c, ctx, c_ctx, w_ada, b_ada, norm1_w, norm2_w, w_in, pool_w, pool_scale, q_norm_w, k_norm_w, rpb, fnet_w, w_gate, b_gate, w_br_pool, w_br_attn, w_br_fnet, w_o, w_up, conv_w, conv_b, w_down):
    n_layers, d, d_in = w_in.shape
    batch, s_len, _ = x.shape
    n_ctx = ctx.shape[1]
    assert batch == 1
    d_pool = w_br_pool.shape[1]
    d_attn = w_br_attn.shape[1]
    d_fnet = w_br_fnet.shape[1]
    n_heads = d_attn // HEAD_DIM
    f = w_down.shape[1]
    assert d_in == d_pool + 3 * d_attn + d_fnet and s_len % GRID_W == 0

    tn_in = _tile(d_pool, 1024)
    assert d_pool % tn_in == 0 and d_attn % tn_in == 0 and d_fnet == d_pool
    q_lo, k_lo, v_lo, f_lo = d_pool, d_pool + d_attn, d_pool + 2 * d_attn, d_pool + 3 * d_attn
    qk_tiles = (q_lo // tn_in, v_lo // tn_in)

    w_in_b = w_in.astype(BF16)
    w_gate_b = w_gate.astype(BF16)
    w_bp_b = w_br_pool.astype(BF16)
    w_ba_b = w_br_attn.astype(BF16)
    w_bf_b = w_br_fnet.astype(BF16)
    w_o_b = w_o.astype(BF16)
    w_up_b = w_up.astype(BF16)
    w_down_b = w_down.astype(BF16)
    pool_w_b = pool_w.astype(BF16)
    fnet_w_b = fnet_w.astype(BF16)
    b_gate_r = b_gate.reshape(n_layers, 1, -1)
    conv_b_r = conv_b.reshape(n_layers, 1, -1)
    ones_p = jnp.ones((d_pool,), F32)
    ones_vf = jnp.ones((d_attn + d_fnet,), F32)

    cs_chan = jnp.stack(_dft_tables(fnet_w.shape[-1])).astype(BF16)

    cond = jnp.concatenate([c.reshape(1, d), c_ctx.reshape(1, d), jnp.zeros((SUBLANES_F32 - 2, d), F32)], axis=0)
    mod = _ada(cond, w_ada, b_ada)

    tk_o = _tile(d, d // 2)
    tk_down = _tile(f, 5632)
    tm_ffn = min(s_len, 1024)
    blk = lambda col: col // HEAD_DIM

    xl = x.reshape(s_len, d)
    xc = ctx.reshape(n_ctx, d)
    for l in range(n_layers):
        last = l == n_layers - 1
        sh1, sc1, g1, sh2, sc2, g2 = [mod[l, 0:1, i * d:(i + 1) * d] for i in range(N_MOD)]
        csh1, csc1, cg1, csh2, csc2, cg2 = [mod[l, 1:2, i * d:(i + 1) * d] for i in range(N_MOD)]
        n1w = norm1_w[l].reshape(1, d)
        n2w = norm2_w[l].reshape(1, d)
        nw_tiles = jnp.concatenate(
            [ones_p, jnp.tile(q_norm_w[l], n_heads), jnp.tile(k_norm_w[l], n_heads), ones_vf]
        ).reshape(d_in // tn_in, 1, tn_in)
        bias = _nat_bias(rpb[l], s_len // GRID_W)
        pool_scale_l = pool_scale[l].reshape(1, -1)

        h = _norm_mod(xl, n1w, sh1, sc1)
        hc = _norm_mod(xc, n1w, csh1, csc1)
        p = _in_proj(h, w_in_b, l, nw_tiles, tn_in, 0, d_in // tn_in, *qk_tiles)
        if last:
            pc = _in_proj(hc, w_in_b, l, nw_tiles, tn_in, k_lo // tn_in, (f_lo - k_lo) // tn_in, *qk_tiles)
            kc_blk, vc_blk = 0, blk(d_attn)
        else:
            pc = _in_proj(hc, w_in_b, l, nw_tiles, tn_in, 0, d_in // tn_in, *qk_tiles)
            kc_blk, vc_blk = blk(k_lo), blk(v_lo)
        y_attn = _nat_attention(p, pc, bias, n_heads, blk(q_lo), blk(k_lo), blk(v_lo), kc_blk, vc_blk)
        y_pool = _pool_mix(p, 0, pool_w_b[l], pool_scale_l)
        y_fnet = _fourier_mix(p, f_lo // d_fnet, cs_chan, fnet_w_b[l])
        m = _merge(h, y_pool, y_attn, y_fnet, l, w_gate_b, b_gate_r, w_bp_b, w_ba_b, w_bf_b)
        xl = _resid_proj(m, w_o_b, l, xl, g1, tk_o, 1024)

        h2 = _norm_mod(xl, n2w, sh2, sc2)
        halo = _halo_norm(xl, n2w, sh2, sc2, tm_ffn)
        t = _ffn_up(h2, halo, l, w_up_b, conv_w, conv_b_r)
        xl = _resid_proj(t, w_down_b, l, xl, g2, tk_down, 512)

        if not last:
            yc_attn = _ctx_attention(pc, n_heads, blk(q_lo), blk(k_lo), blk(v_lo))
            yc_pool = _pool_mix(pc, 0, pool_w_b[l], pool_scale_l)
            yc_fnet = _fourier_mix(pc, f_lo // d_fnet, cs_chan, fnet_w_b[l])
            mc = _merge(hc, yc_pool, yc_attn, yc_fnet, l, w_gate_b, b_gate_r, w_bp_b, w_ba_b, w_bf_b)
            xc = _resid_proj(mc, w_o_b, l, xc, cg1, tk_o, 1024)
            hc2 = _norm_mod(xc, n2w, csh2, csc2)
            halo_c = _halo_norm(xc, n2w, csh2, csc2, n_ctx)
            tc = _ffn_up(hc2, halo_c, l, w_up_b, conv_w, conv_b_r)
            xc = _resid_proj(tc, w_down_b, l, xc, cg2, tk_down, 512)
    return xl.reshape(batch, s_len, d)
```

```python
import functools
import math

import numpy as np
import jax
import jax.numpy as jnp
from jax import lax
from jax.experimental import pallas as pl
from jax.experimental.pallas import tpu as pltpu

F32 = jnp.float32
BF16 = jnp.bfloat16

GRID_W = 64
HEAD_DIM = 128
WIN_ROWS = 8
WIN_COLS = 16
POOL_WINDOWS = (2, 4, 8, 16)
N_GROUPS = 4
N_MOD = 6
CONV_W = 3
EPS = 1e-6

LANES = 128
SUBLANES_F32 = 8
SUBLANES_BF16 = 16
VMEM_CAP_BYTES = 56 * 1024 * 1024

ATTN_ROWS = 4
ATTN_KEY_ROWS = ATTN_ROWS + WIN_ROWS
ATTN_HEADS = 2
FFN_EPILOGUE_ROWS = 64
EPILOGUE_ROWS = 256
NEG_BIAS = -1e30
FFT_MIN_LEN = 1024


def _tile(n, pref):
    if n <= pref:
        return n
    t = pref - pref % LANES
    while t >= LANES:
        if n % t == 0:
            return t
        t -= LANES
    return n


def _params(n_axes, vmem_bytes):
    return pltpu.CompilerParams(
        dimension_semantics=("arbitrary",) * n_axes,
        vmem_limit_bytes=int(min(VMEM_CAP_BYTES, max(vmem_bytes, 16 * 1024 * 1024))),
    )


def _dot(a, b):
    return jnp.dot(a, b, preferred_element_type=F32)


def _dot_nt(a, b):
    return lax.dot_general(a, b, (((1,), (1,)), ((), ())), preferred_element_type=F32)


def _rms_mod(x, nw, sh, sc):
    ms = jnp.mean(x * x, axis=-1, keepdims=True)
    return (x * lax.rsqrt(ms + EPS) * nw) * (1.0 + sc) + sh


def _ada_kernel(s_ref, w_ref, b_ref, o_ref):
    s = s_ref[...]
    a = (s * jax.nn.sigmoid(s)).astype(BF16)
    o_ref[0] = _dot(a, w_ref[0].astype(BF16)) + b_ref[0]


def _ada(cond, w_ada, b_ada):
    n_layers, d, n = w_ada.shape
    tn = _tile(n, 1024)
    return pl.pallas_call(
        _ada_kernel,
        grid=(n_layers, n // tn),
        in_specs=[
            pl.BlockSpec((SUBLANES_F32, d), lambda l, j: (0, 0)),
            pl.BlockSpec((1, d, tn), lambda l, j: (l, 0, j)),
            pl.BlockSpec((1, 1, tn), lambda l, j: (l, 0, j)),
        ],
        out_specs=pl.BlockSpec((1, SUBLANES_F32, tn), lambda l, j: (l, 0, j)),
        out_shape=jax.ShapeDtypeStruct((n_layers, SUBLANES_F32, n), F32),
        compiler_params=_params(2, 2 * d * tn * 4 + 8 * 1024 * 1024),
        name="ada",
    )(cond, w_ada, b_ada.reshape(n_layers, 1, n))


def _norm_kernel(x_ref, nw_ref, sh_ref, sc_ref, o_ref):
    o_ref[...] = _rms_mod(x_ref[...], nw_ref[...], sh_ref[...], sc_ref[...]).astype(o_ref.dtype)


def _norm_mod(x, nw, sh, sc):
    m, d = x.shape
    tm = min(m, 512)
    vec = pl.BlockSpec((1, d), lambda i: (0, 0))
    return pl.pallas_call(
        _norm_kernel,
        grid=(m // tm,),
        in_specs=[pl.BlockSpec((tm, d), lambda i: (i, 0)), vec, vec, vec],
        out_specs=pl.BlockSpec((tm, d), lambda i: (i, 0)),
        out_shape=jax.ShapeDtypeStruct((m, d), BF16),
        compiler_params=_params(1, 6 * tm * d * 4),
        name="norm_mod",
    )(x, nw, sh, sc)


def _halo_kernel(xp_ref, xn_ref, nw_ref, sh_ref, sc_ref, o_ref):
    i = pl.program_id(0)
    n = pl.num_programs(0)
    hp = _rms_mod(xp_ref[...], nw_ref[...], sh_ref[...], sc_ref[...])
    hn = _rms_mod(xn_ref[...], nw_ref[...], sh_ref[...], sc_ref[...])
    hp = jnp.where(i > 0, hp, 0.0)
    hn = jnp.where(i < n - 1, hn, 0.0)
    o_ref[0] = jnp.concatenate([hn, hp], axis=0).astype(o_ref.dtype)


def _halo_norm(x, nw, sh, sc, tm):
    m, d = x.shape
    nblk = m // tm
    r = SUBLANES_F32
    per = tm // r
    last = m // r - 1
    vec = pl.BlockSpec((1, d), lambda i: (0, 0))
    return pl.pallas_call(
        _halo_kernel,
        grid=(nblk,),
        in_specs=[
            pl.BlockSpec((r, d), lambda i: (jnp.maximum(i * per - 1, 0), 0)),
            pl.BlockSpec((r, d), lambda i: (jnp.minimum((i + 1) * per, last), 0)),
            vec, vec, vec,
        ],
        out_specs=pl.BlockSpec((1, 2 * r, d), lambda i: (i, 0, 0)),
        out_shape=jax.ShapeDtypeStruct((nblk, 2 * r, d), BF16),
        compiler_params=_params(1, 16 * 1024 * 1024),
        name="halo_norm",
    )(x, x, nw, sh, sc)


def _inproj_kernel(a_ref, b_ref, nw_ref, o_ref, acc0, acc1, *, j_off, qk_lo, qk_hi):
    tm = a_ref.shape[0]
    j = pl.program_id(1)
    nj = pl.num_programs(1) - 1
    tile = j - 1 + j_off
    is_qk = jnp.logical_and(tile >= qk_lo, tile < qk_hi)
    rc = min(tm, EPILOGUE_ROWS)

    def matmul(acc):
        acc[...] = _dot(a_ref[...], b_ref[...].astype(BF16))

    def epilogue(acc):
        for r0 in range(0, tm, rc):
            for h in range(o_ref.shape[1] // HEAD_DIM):
                sl = slice(h * HEAD_DIM, (h + 1) * HEAD_DIM)
                xh = acc[r0:r0 + rc, sl]
                ms = jnp.mean(xh * xh, axis=-1, keepdims=True)
                normed = xh * lax.rsqrt(ms + EPS) * nw_ref[0, :, sl]
                o_ref[r0:r0 + rc, sl] = jnp.where(is_qk, normed, xh).astype(o_ref.dtype)

    odd = lax.rem(j, 2) == 1
    mid = jnp.logical_and(j > 0, j < nj)

    @pl.when(j == 0)
    def _():
        matmul(acc0)

    @pl.when(jnp.logical_and(mid, odd))
    def _():
        epilogue(acc0)
        matmul(acc1)

    @pl.when(jnp.logical_and(mid, jnp.logical_not(odd)))
    def _():
        epilogue(acc1)
        matmul(acc0)

    @pl.when(jnp.logical_and(j == nj, odd))
    def _():
        epilogue(acc0)

    @pl.when(jnp.logical_and(j == nj, jnp.logical_not(odd)))
    def _():
        epilogue(acc1)


def _in_proj(h, w_in, layer, nw_tiles, tn, j_off, n_tiles, qk_lo, qk_hi):
    m, d = h.shape
    tm = min(m, 1024)
    acc = pltpu.VMEM((tm, tn), F32)
    return pl.pallas_call(
        functools.partial(_inproj_kernel, j_off=j_off, qk_lo=qk_lo, qk_hi=qk_hi),
        grid=(m // tm, n_tiles + 1),
        in_specs=[
            pl.BlockSpec((tm, d), lambda i, j: (i, 0)),
            pl.BlockSpec((None, d, tn), lambda i, j: (layer, 0, jnp.minimum(j, n_tiles - 1) + j_off)),
            pl.BlockSpec((1, 1, tn), lambda i, j: (jnp.maximum(j - 1, 0) + j_off, 0, 0)),
        ],
        out_specs=pl.BlockSpec((tm, tn), lambda i, j: (i, jnp.maximum(j - 1, 0))),
        out_shape=jax.ShapeDtypeStruct((m, n_tiles * tn), BF16),
        scratch_shapes=[acc, acc],
        compiler_params=_params(2, 4 * tm * d + 10 * d * tn + 4 * tm * tn + 6 * tm * tn * 4),
        name="in_proj",
    )(h, w_in, nw_tiles)


def _nat_kernel(q_ref, k_ref, v_ref, kc_ref, vc_ref, bias_ref, o_ref, *, grid_rows, scale):
    b = pl.program_id(1)
    ks = jnp.clip(b * ATTN_ROWS - WIN_ROWS // 2, 0, grid_rows - ATTN_KEY_ROWS)
    start = pl.multiple_of(ks * GRID_W, GRID_W)
    nkeys = ATTN_KEY_ROWS * GRID_W
    for h in range(ATTN_HEADS):
        sl = slice(h * HEAD_DIM, (h + 1) * HEAD_DIM)
        q = q_ref[:, sl]
        kw = k_ref[pl.ds(start, nkeys), sl]
        vw = v_ref[pl.ds(start, nkeys), sl]
        s = _dot_nt(q, kw) * scale + bias_ref[0, h]
        sc = _dot_nt(q, kc_ref[:, sl]) * scale
        m = jnp.maximum(jnp.max(s, axis=-1, keepdims=True), jnp.max(sc, axis=-1, keepdims=True))
        p = jnp.exp(s - m)
        pc = jnp.exp(sc - m)
        l = jnp.sum(p, axis=-1, keepdims=True) + jnp.sum(pc, axis=-1, keepdims=True)
        o = _dot(p.astype(BF16), vw) + _dot(pc.astype(BF16), vc_ref[:, sl])
        o_ref[:, sl] = (o / l).astype(o_ref.dtype)


def _nat_bias(rpb, grid_rows):
    n_heads = rpb.shape[0]
    nb = grid_rows // ATTN_ROWS
    c = np.arange(GRID_W)[:, None]
    kc = np.arange(GRID_W)[None, :]
    c_start = np.clip(c - WIN_COLS // 2, 0, GRID_W - WIN_COLS)
    col_ok = (kc >= c_start) & (kc < c_start + WIN_COLS)
    dc = kc - c + (WIN_COLS - 1)
    n_dc = 2 * WIN_COLS - 1
    expand = ((dc[None] == np.arange(n_dc)[:, None, None]) & col_ok[None]).astype(np.float32)
    blocks = jnp.einsum("hrd,dn->hrn", rpb, expand.reshape(n_dc, GRID_W * GRID_W),
                        precision=lax.Precision.HIGHEST)
    blocks = blocks.reshape(n_heads, 2 * WIN_ROWS - 1, GRID_W, GRID_W)
    blocks = jnp.where(col_ok, blocks, NEG_BIAS)
    masked = jnp.full((n_heads, GRID_W, GRID_W), NEG_BIAS, F32)
    variants = []
    for b in (0, 1, nb - 1):
        ks = int(np.clip(b * ATTN_ROWS - WIN_ROWS // 2, 0, grid_rows - ATTN_KEY_ROWS))
        q_rows = []
        for r_loc in range(ATTN_ROWS):
            r = b * ATTN_ROWS + r_loc
            r_start = int(np.clip(r - WIN_ROWS // 2, 0, grid_rows - WIN_ROWS))
            row = []
            for kr in range(ks, ks + ATTN_KEY_ROWS):
                ok = r_start <= kr < r_start + WIN_ROWS
                row.append(blocks[:, kr - r + (WIN_ROWS - 1)] if ok else masked)
            q_rows.append(jnp.concatenate(row, axis=-1))
        variants.append(jnp.concatenate(q_rows, axis=-2))
    return jnp.stack(variants)


def _nat_attention(p_lat, p_ctx, bias, n_heads, q_blk, k_blk, v_blk, kc_blk, vc_blk):
    s_len = p_lat.shape[0]
    n_ctx = p_ctx.shape[0]
    grid_rows = s_len // GRID_W
    nb = grid_rows // ATTN_ROWS
    nq = ATTN_ROWS * GRID_W
    nk = ATTN_KEY_ROWS * GRID_W
    assert grid_rows % ATTN_ROWS == 0 and nb >= 3 and grid_rows >= ATTN_KEY_ROWS
    hs = ATTN_HEADS
    width = hs * HEAD_DIM
    assert n_heads % hs == 0 and all(blk % hs == 0 for blk in (q_blk, k_blk, v_blk, kc_blk, vc_blk))

    def variant(b):
        return jnp.where(b == 0, 0, jnp.where(b == nb - 1, 2, 1))

    def cols(rows, blk):
        return pl.BlockSpec((rows, width), lambda h, b: (0, blk // hs + h))

    return pl.pallas_call(
        functools.partial(_nat_kernel, grid_rows=grid_rows, scale=HEAD_DIM ** -0.5),
        grid=(n_heads // hs, nb),
        in_specs=[
            pl.BlockSpec((nq, width), lambda h, b: (b, q_blk // hs + h)),
            cols(s_len, k_blk), cols(s_len, v_blk), cols(n_ctx, kc_blk), cols(n_ctx, vc_blk),
            pl.BlockSpec((1, hs, nq, nk), lambda h, b: (variant(b), h, 0, 0)),
        ],
        out_specs=pl.BlockSpec((nq, width), lambda h, b: (b, h)),
        out_shape=jax.ShapeDtypeStruct((s_len, n_heads * HEAD_DIM), BF16),
        compiler_params=_params(2, 8 * s_len * width + hs * (2 * nq * nk * 4 + 16 * nq * (nk + n_ctx) * 4)),
        name="nat_attention",
    )(p_lat, p_lat, p_lat, p_ctx, p_ctx, bias)


def _ctx_attn_kernel(q_ref, k_ref, v_ref, o_ref, *, scale):
    s = _dot_nt(q_ref[...], k_ref[...]) * scale
    m = jnp.max(s, axis=-1, keepdims=True)
    p = jnp.exp(s - m)
    l = jnp.sum(p, axis=-1, keepdims=True)
    o_ref[...] = (_dot(p.astype(BF16), v_ref[...]) / l).astype(o_ref.dtype)


def _ctx_attention(p_ctx, n_heads, q_blk, k_blk, v_blk):
    n = p_ctx.shape[0]
    return pl.pallas_call(
        functools.partial(_ctx_attn_kernel, scale=HEAD_DIM ** -0.5),
        grid=(n_heads,),
        in_specs=[
            pl.BlockSpec((n, HEAD_DIM), lambda h: (0, q_blk + h)),
            pl.BlockSpec((n, HEAD_DIM), lambda h: (0, k_blk + h)),
            pl.BlockSpec((n, HEAD_DIM), lambda h: (0, v_blk + h)),
        ],
        out_specs=pl.BlockSpec((n, HEAD_DIM), lambda h: (0, h)),
        out_shape=jax.ShapeDtypeStruct((n, n_heads * HEAD_DIM), BF16),
        compiler_params=_params(1, 16 * 1024 * 1024),
        name="ctx_attention",
    )(p_ctx, p_ctx, p_ctx)


def _pool_kernel(up_ref, um_ref, un_ref, pw_ref, ps_ref, o_ref, *, seq_len, gc):
    i = pl.program_id(0)
    n = pl.num_programs(0)
    tp = um_ref.shape[0]
    halo = up_ref.shape[0]
    um = um_ref[...]
    up = jnp.where(i > 0, up_ref[...], jnp.zeros_like(up_ref[...]))
    un = jnp.where(i < n - 1, un_ref[...], jnp.zeros_like(un_ref[...]))
    ue = jnp.concatenate([up, um, un], axis=0)
    trow = lax.broadcasted_iota(jnp.int32, (tp, tp + 2 * halo), 0)
    jcol = lax.broadcasted_iota(jnp.int32, (tp, tp + 2 * halo), 1)
    off = jcol - trow - halo
    t_abs = i * tp + lax.broadcasted_iota(jnp.int32, (tp, gc), 0)
    for g, w in enumerate(POOL_WINDOWS):
        lo_off = -(w // 2)
        hi_off = w - 1 - w // 2
        band = jnp.where(off >= lo_off, jnp.where(off <= hi_off, 1.0, 0.0), 0.0).astype(BF16)
        sl = slice(g * gc, (g + 1) * gc)
        wsum = _dot(band, ue[:, sl])
        lo = jnp.maximum(t_abs + lo_off, 0)
        hi = jnp.minimum(t_abs + hi_off, seq_len - 1)
        cnt = (hi - lo + 1).astype(F32)
        d = (wsum / cnt - um[:, sl].astype(F32)).astype(BF16)
        o_ref[:, sl] = (_dot(d, pw_ref[g]) * ps_ref[:, sl]).astype(o_ref.dtype)


def _pool_mix(p, col_blk, pool_w, pool_scale):
    m = p.shape[0]
    gc = pool_w.shape[-1]
    width = N_GROUPS * gc
    tp = min(m, 512)
    halo = SUBLANES_BF16
    per = tp // halo
    last = m // halo - 1
    return pl.pallas_call(
        functools.partial(_pool_kernel, seq_len=m, gc=gc),
        grid=(m // tp,),
        in_specs=[
            pl.BlockSpec((halo, width), lambda i: (jnp.maximum(i * per - 1, 0), col_blk)),
            pl.BlockSpec((tp, width), lambda i: (i, col_blk)),
            pl.BlockSpec((halo, width), lambda i: (jnp.minimum((i + 1) * per, last), col_blk)),
            pl.BlockSpec((N_GROUPS, gc, gc), lambda i: (0, 0, 0)),
            pl.BlockSpec((1, width), lambda i: (0, 0)),
        ],
        out_specs=pl.BlockSpec((tp, width), lambda i: (i, 0)),
        out_shape=jax.ShapeDtypeStruct((m, width), BF16),
        compiler_params=_params(1, 32 * 1024 * 1024),
        name="pool_mix",
    )(p, p, p, pool_w, pool_scale)


def _cos_sin(num, den):
    ang = (num % den).astype(F32) * (2.0 * math.pi / den)
    return jnp.cos(ang), jnp.sin(ang)


def _dft_tables(n):
    idx = jnp.arange(n, dtype=jnp.int32)
    return _cos_sin(idx[:, None] * idx[None, :], n)


def _fnet_chan_kernel(u_ref, cs_ref, o_ref, *, gc):
    for g in range(N_GROUPS):
        sl = slice(g * gc, (g + 1) * gc)
        o_ref[:, sl] = _dot(u_ref[:, sl], cs_ref[0]).astype(o_ref.dtype)


def _fnet_pos_kernel(t_ref, ab_ref, fw_ref, o_ref, *, scale, gc):
    acc = _dot(t_ref[...], ab_ref[...])
    for g in range(N_GROUPS):
        sl = slice(g * gc, (g + 1) * gc)
        f = (acc[:, sl] * scale).astype(BF16)
        o_ref[:, sl] = _dot(f, fw_ref[g]).astype(o_ref.dtype)


def _fourier_mix_dense(p, col_blk, cs_chan, fnet_w):
    m = p.shape[0]
    gc = fnet_w.shape[-1]
    width = N_GROUPS * gc
    cpos, spos = _dft_tables(m)
    t_pos = jnp.concatenate([cpos, -spos], axis=1).astype(BF16)
    ab = pl.pallas_call(
        functools.partial(_fnet_chan_kernel, gc=gc),
        grid=(2,),
        in_specs=[
            pl.BlockSpec((m, width), lambda c: (0, col_blk)),
            pl.BlockSpec((1, gc, gc), lambda c: (c, 0, 0)),
        ],
        out_specs=pl.BlockSpec((m, width), lambda c: (c, 0)),
        out_shape=jax.ShapeDtypeStruct((2 * m, width), BF16),
        compiler_params=_params(1, 32 * 1024 * 1024),
        name="fnet_chan",
    )(p, cs_chan)
    return pl.pallas_call(
        functools.partial(_fnet_pos_kernel, scale=1.0 / math.sqrt(m * gc), gc=gc),
        grid=(1,),
        in_specs=[
            pl.BlockSpec((m, 2 * m), lambda i: (0, 0)),
            pl.BlockSpec((2 * m, width), lambda i: (0, 0)),
            pl.BlockSpec((N_GROUPS, gc, gc), lambda i: (0, 0, 0)),
        ],
        out_specs=pl.BlockSpec((m, width), lambda i: (0, 0)),
        out_shape=jax.ShapeDtypeStruct((m, width), BF16),
        compiler_params=_params(1, 32 * 1024 * 1024),
        name="fnet_pos",
    )(t_pos, ab, fnet_w)


def _fft_stage1_kernel(f1_ref, u_ref, tc_ref, ts_ref, zr_ref, zi_ref, *, n1):
    y = _dot(f1_ref[...], u_ref[...])
    yr, yi = y[:n1], y[n1:]
    reps = u_ref.shape[1] // LANES
    tc = jnp.tile(tc_ref[...], (1, reps))
    ts = jnp.tile(ts_ref[...], (1, reps))
    zr_ref[...] = (yr * tc + yi * ts).astype(zr_ref.dtype)
    zi_ref[...] = (yi * tc - yr * ts).astype(zi_ref.dtype)


def _fft_stage2_kernel(fa_ref, fb_ref, zr_ref, zi_ref, cc_ref, sc_ref, fw_ref, o_ref, *, n2, gc, chunks, scale):
    width = N_GROUPS * gc
    p = _dot(fa_ref[...], zr_ref[...]) + _dot(fb_ref[...], zi_ref[...])
    for g in range(N_GROUPS):
        cols = [slice(ch * width + g * gc, ch * width + (g + 1) * gc) for ch in range(chunks)]
        pr = jnp.concatenate([p[:n2, c] for c in cols], axis=0).astype(BF16)
        pi = jnp.concatenate([p[n2:, c] for c in cols], axis=0).astype(BF16)
        f = ((_dot(pr, cc_ref[...]) + _dot(pi, sc_ref[...])) * scale).astype(BF16)
        y = _dot(f, fw_ref[g]).astype(o_ref.dtype)
        for ch, c in enumerate(cols):
            o_ref[:, c] = y[ch * n2:(ch + 1) * n2]


def _fourier_mix_fft(p, col_blk, cs_chan, fnet_w):
    m, d_in = p.shape
    gc = fnet_w.shape[-1]
    width = N_GROUPS * gc
    n2 = LANES
    n1 = m // n2
    i1 = jnp.arange(n1, dtype=jnp.int32)
    i2 = jnp.arange(n2, dtype=jnp.int32)
    c1, s1 = _cos_sin(i1[:, None] * i1[None, :], n1)
    f1 = jnp.concatenate([c1, -s1], axis=0).astype(BF16)
    tc, ts = _cos_sin(i1[:, None] * i2[None, :], m)
    tc = jnp.repeat(tc, LANES, axis=1)
    ts = jnp.repeat(ts, LANES, axis=1)
    c2, s2 = _cos_sin(i2[:, None] * i2[None, :], n2)
    fa = jnp.concatenate([c2, -s2], axis=0).astype(BF16)
    fb = jnp.concatenate([s2, c2], axis=0).astype(BF16)
    u = p[:, col_blk * width:(col_blk + 1) * width].reshape(n1, n2 * width)
    z_shape = jax.ShapeDtypeStruct((n2 * n1, width), BF16)
    zr, zi = pl.pallas_call(
        functools.partial(_fft_stage1_kernel, n1=n1),
        grid=(n2,),
        in_specs=[
            pl.BlockSpec((2 * n1, n1), lambda j: (0, 0)),
            pl.BlockSpec((n1, width), lambda j: (0, j)),
            pl.BlockSpec((n1, LANES), lambda j: (0, j)),
            pl.BlockSpec((n1, LANES), lambda j: (0, j)),
        ],
        out_specs=[pl.BlockSpec((n1, width), lambda j: (j, 0))] * 2,
        out_shape=[z_shape, z_shape],
        compiler_params=_params(1, 16 * 1024 * 1024),
        name="fft_stage1",
    )(f1, u, tc, ts)
    chunks = min(n1, 8)
    tn = chunks * width
    y = pl.pallas_call(
        functools.partial(_fft_stage2_kernel, n2=n2, gc=gc, chunks=chunks, scale=1.0 / math.sqrt(m * gc)),
        grid=(n1 // chunks,),
        in_specs=[
            pl.BlockSpec((2 * n2, n2), lambda j: (0, 0)),
            pl.BlockSpec((2 * n2, n2), lambda j: (0, 0)),
            pl.BlockSpec((n2, tn), lambda j: (0, j)),
            pl.BlockSpec((n2, tn), lambda j: (0, j)),
            pl.BlockSpec((None, gc, gc), lambda j: (0, 0, 0)),
            pl.BlockSpec((None, gc, gc), lambda j: (1, 0, 0)),
            pl.BlockSpec((N_GROUPS, gc, gc), lambda j: (0, 0, 0)),
        ],
        out_specs=pl.BlockSpec((n2, tn), lambda j: (0, j)),
        out_shape=jax.ShapeDtypeStruct((n2, n1 * width), BF16),
        compiler_params=_params(1, 40 * 1024 * 1024),
        name="fft_stage2",
    )(fa, fb, zr.reshape(n2, n1 * width), zi.reshape(n2, n1 * width), cs_chan, cs_chan, fnet_w)
    return y.reshape(m, width)


def _fourier_mix(p, col_blk, cs_chan, fnet_w):
    if p.shape[0] >= FFT_MIN_LEN and p.shape[0] % (LANES * SUBLANES_F32) == 0:
        return _fourier_mix_fft(p, col_blk, cs_chan, fnet_w)
    return _fourier_mix_dense(p, col_blk, cs_chan, fnet_w)


def _merge_kernel(h_ref, yp_ref, ya_ref, yf_ref, wg0, wg1, wg2, bg0, bg1, bg2, wp, wa, wf, o_ref):
    h = h_ref[...]

    def gate(w_ref, b_ref):
        return jax.nn.sigmoid(_dot(h, w_ref[...]) + b_ref[...])

    m = gate(wg0, bg0) * _dot(yp_ref[...], wp[...])
    m = m + gate(wg1, bg1) * _dot(ya_ref[...], wa[...])
    m = m + gate(wg2, bg2) * _dot(yf_ref[...], wf[...])
    o_ref[...] = m.astype(o_ref.dtype)


def _merge(h, y_pool, y_attn, y_fnet, layer, w_gate, b_gate, w_bp, w_ba, w_bf):
    m, d = h.shape
    dp, da, df = y_pool.shape[1], y_attn.shape[1], y_fnet.shape[1]
    tm = min(m, 512)
    tn = _tile(d, 256)
    nj = d // tn

    def act(width):
        return pl.BlockSpec((tm, width), lambda i, j: (i, 0))

    def gate_w(b):
        return pl.BlockSpec((None, d, tn), lambda i, j: (layer, 0, b * nj + j))

    def gate_b(b):
        return pl.BlockSpec((None, 1, tn), lambda i, j: (layer, 0, b * nj + j))

    def br_w(k):
        return pl.BlockSpec((None, k, tn), lambda i, j: (layer, 0, j))

    vmem = 4 * tm * (d + dp + da + df) + 4 * tn * (3 * d + dp + da + df) + 12 * tm * tn * 4
    return pl.pallas_call(
        _merge_kernel,
        grid=(m // tm, nj),
        in_specs=[act(d), act(dp), act(da), act(df), gate_w(0), gate_w(1), gate_w(2),
                  gate_b(0), gate_b(1), gate_b(2), br_w(dp), br_w(da), br_w(df)],
        out_specs=pl.BlockSpec((tm, tn), lambda i, j: (i, j)),
        out_shape=jax.ShapeDtypeStruct((m, d), BF16),
        compiler_params=_params(2, vmem),
        name="merge",
    )(h, y_pool, y_attn, y_fnet, w_gate, w_gate, w_gate, b_gate, b_gate, b_gate, w_bp, w_ba, w_bf)


def _resid_kernel(a_ref, w_ref, x_ref, g_ref, o_ref, acc_ref):
    k = pl.program_id(2)
    d = _dot(a_ref[...], w_ref[...])

    @pl.when(k == 0)
    def _():
        acc_ref[...] = d

    @pl.when(k > 0)
    def _():
        acc_ref[...] += d

    @pl.when(k == pl.num_programs(2) - 1)
    def _():
        o_ref[...] = x_ref[...] + g_ref[...] * acc_ref[...]


def _resid_full_kernel(a_ref, w_ref, x_ref, g_ref, o_ref):
    o_ref[...] = x_ref[...] + g_ref[...] * _dot(a_ref[...], w_ref[...])


def _resid_proj(a, w, layer, x, g, tk, tn_pref):
    m, kk = a.shape
    n = w.shape[2]
    tm = min(m, 1024)
    tn = _tile(n, tn_pref)
    if tk == kk:
        return pl.pallas_call(
            _resid_full_kernel,
            grid=(m // tm, n // tn),
            in_specs=[
                pl.BlockSpec((tm, kk), lambda i, j: (i, 0)),
                pl.BlockSpec((None, kk, tn), lambda i, j: (layer, 0, j)),
                pl.BlockSpec((tm, tn), lambda i, j: (i, j)),
                pl.BlockSpec((1, tn), lambda i, j: (0, j)),
            ],
            out_specs=pl.BlockSpec((tm, tn), lambda i, j: (i, j)),
            out_shape=jax.ShapeDtypeStruct((m, n), F32),
            compiler_params=_params(2, 4 * tm * kk + 4 * kk * tn + 7 * tm * tn * 4),
            name="resid_proj_full",
        )(a, w, x, g)
    return pl.pallas_call(
        _resid_kernel,
        grid=(m // tm, n // tn, kk // tk),
        in_specs=[
            pl.BlockSpec((tm, tk), lambda i, j, k: (i, k)),
            pl.BlockSpec((None, tk, tn), lambda i, j, k: (layer, k, j)),
            pl.BlockSpec((tm, tn), lambda i, j, k: (i, j)),
            pl.BlockSpec((1, tn), lambda i, j, k: (0, j)),
        ],
        out_specs=pl.BlockSpec((tm, tn), lambda i, j, k: (i, j)),
        out_shape=jax.ShapeDtypeStruct((m, n), F32),
        scratch_shapes=[pltpu.VMEM((tm, tn), F32)],
        compiler_params=_params(3, 4 * tm * tk + 4 * tk * tn + 7 * tm * tn * 4),
        name="resid_proj",
    )(a, w, x, g)


def _ffn_up_kernel(a_ref, halo_ref, wg_ref, wv_ref, cwg_ref, cwv_ref, cbg_ref, cbv_ref, o_ref,
                   a_ext, ug0, uv0, ug1, uv1):
    tm = a_ref.shape[0]
    j = pl.program_id(1)
    nj = pl.num_programs(1) - 1

    @pl.when(j == 0)
    def _():
        a_ext[:tm, :] = a_ref[...]
        a_ext[tm:, :] = halo_ref[0]

    def matmul(ug, uv):
        a = a_ext[...]
        ug[...] = _dot(a, wg_ref[...])
        uv[...] = _dot(a, wv_ref[...])

    rc = min(tm, FFN_EPILOGUE_ROWS)

    def conv(u_ref, cw_ref, cb_ref, r0):
        um = u_ref[r0:r0 + rc, :]
        u_up = u_ref[r0 + 1:r0 + rc + 1, :]
        if r0 == 0:
            row = lax.broadcasted_iota(jnp.int32, um.shape, 0)
            prev_row = u_ref[tm + 2 * SUBLANES_F32 - 1:tm + 2 * SUBLANES_F32, :]
            u_dn = jnp.where(row == 0, prev_row, pltpu.roll(um, 1, 0))
        else:
            u_dn = u_ref[r0 - 1:r0 + rc - 1, :]
        return u_dn * cw_ref[0:1] + um * cw_ref[1:2] + u_up * cw_ref[2:3] + cb_ref[...]

    def epilogue(ug, uv):
        for r0 in range(0, tm, rc):
            gate = conv(ug, cwg_ref, cbg_ref, r0)
            val = conv(uv, cwv_ref, cbv_ref, r0)
            o_ref[r0:r0 + rc, :] = (gate * jax.nn.sigmoid(gate) * val).astype(o_ref.dtype)

    odd = lax.rem(j, 2) == 1

    @pl.when(j == 0)
    def _():
        matmul(ug0, uv0)

    @pl.when(jnp.logical_and(jnp.logical_and(j > 0, j < nj), odd))
    def _():
        epilogue(ug0, uv0)
        matmul(ug1, uv1)

    @pl.when(jnp.logical_and(jnp.logical_and(j > 0, j < nj), jnp.logical_not(odd)))
    def _():
        epilogue(ug1, uv1)
        matmul(ug0, uv0)

    @pl.when(jnp.logical_and(j == nj, odd))
    def _():
        epilogue(ug0, uv0)

    @pl.when(jnp.logical_and(j == nj, jnp.logical_not(odd)))
    def _():
        epilogue(ug1, uv1)


def _ffn_up(h2, halo, layer, w_up, conv_w, conv_b):
    m, d = h2.shape
    f = w_up.shape[2] // 2
    tm = m // halo.shape[0]
    ext = halo.shape[1]
    tn = _tile(f, 256)
    nj = f // tn
    vmem = 2 * tm * d + 2 * (tm + ext) * d + 8 * d * tn + 4 * tm * tn + 20 * (tm + ext) * tn * 4

    def w_cols(half):
        return pl.BlockSpec((None, d, tn), lambda i, j: (layer, 0, half * nj + jnp.minimum(j, nj - 1)))

    def ep_cols(rows, half):
        return pl.BlockSpec((None, rows, tn), lambda i, j: (layer, 0, half * nj + jnp.maximum(j - 1, 0)))

    u_scratch = pltpu.VMEM((tm + ext, tn), F32)
    return pl.pallas_call(
        _ffn_up_kernel,
        grid=(m // tm, nj + 1),
        in_specs=[
            pl.BlockSpec((tm, d), lambda i, j: (i, 0), pipeline_mode=pl.Buffered(1)),
            pl.BlockSpec((1, ext, d), lambda i, j: (i, 0, 0)),
            w_cols(0), w_cols(1), ep_cols(CONV_W, 0), ep_cols(CONV_W, 1), ep_cols(1, 0), ep_cols(1, 1),
        ],
        out_specs=pl.BlockSpec((tm, tn), lambda i, j: (i, jnp.maximum(j - 1, 0))),
        out_shape=jax.ShapeDtypeStruct((m, f), BF16),
        scratch_shapes=[pltpu.VMEM((tm + ext, d), BF16), u_scratch, u_scratch, u_scratch, u_scratch],
        compiler_params=_params(2, vmem),
        name="ffn_up",
    )(h2, halo, w_up, w_up, conv_w, conv_w, conv_b, conv_b)


def kernel(x, c, ctx, c_ctx, w_ada, b_ada, norm1_w, norm2_w, w_in, pool_w, pool_scale, q_norm_w, k_norm_w, rpb, fnet_w, w_gate, b_gate, w_br_pool, w_br_attn, w_br_fnet, w_o, w_up, conv_w, conv_b, w_down):
    n_layers, d, d_in = w_in.shape
    batch, s_len, _ = x.shape
    n_ctx = ctx.shape[1]
    assert batch == 1
    d_pool = w_br_pool.shape[1]
    d_attn = w_br_attn.shape[1]
    d_fnet = w_br_fnet.shape[1]
    n_heads = d_attn // HEAD_DIM
    f = w_down.shape[1]
    assert d_in == d_pool + 3 * d_attn + d_fnet and s_len % GRID_W == 0

    tn_in = _tile(d_pool, 512)
    assert d_pool % tn_in == 0 and d_attn % tn_in == 0 and d_fnet == d_pool
    q_lo, k_lo, v_lo, f_lo = d_pool, d_pool + d_attn, d_pool + 2 * d_attn, d_pool + 3 * d_attn
    qk_tiles = (q_lo // tn_in, v_lo // tn_in)

    w_gate_b = w_gate.astype(BF16)
    w_bp_b = w_br_pool.astype(BF16)
    w_ba_b = w_br_attn.astype(BF16)
    w_bf_b = w_br_fnet.astype(BF16)
    w_o_b = w_o.astype(BF16)
    w_up_b = w_up.astype(BF16)
    w_down_b = w_down.astype(BF16)
    pool_w_b = pool_w.astype(BF16)
    fnet_w_b = fnet_w.astype(BF16)
    b_gate_r = b_gate.reshape(n_layers, 1, -1)
    conv_b_r = conv_b.reshape(n_layers, 1, -1)
    ones_p = jnp.ones((d_pool,), F32)
    ones_vf = jnp.ones((d_attn + d_fnet,), F32)

    cs_chan = jnp.stack(_dft_tables(fnet_w.shape[-1])).astype(BF16)

    cond = jnp.concatenate([c.reshape(1, d), c_ctx.reshape(1, d), jnp.zeros((SUBLANES_F32 - 2, d), F32)], axis=0)
    mod = _ada(cond, w_ada, b_ada)

    tk_o = d
    tk_down = _tile(f, 5632)
    tm_ffn = min(s_len, 1024)
    blk = lambda col: col // HEAD_DIM

    xl = x.reshape(s_len, d)
    xc = ctx.reshape(n_ctx, d)
    for l in range(n_layers):
        last = l == n_layers - 1
        sh1, sc1, g1, sh2, sc2, g2 = [mod[l, 0:1, i * d:(i + 1) * d] for i in range(N_MOD)]
        csh1, csc1, cg1, csh2, csc2, cg2 = [mod[l, 1:2, i * d:(i + 1) * d] for i in range(N_MOD)]
        n1w = norm1_w[l].reshape(1, d)
        n2w = norm2_w[l].reshape(1, d)
        nw_tiles = jnp.concatenate(
            [ones_p, jnp.tile(q_norm_w[l], n_heads), jnp.tile(k_norm_w[l], n_heads), ones_vf]
        ).reshape(d_in // tn_in, 1, tn_in)
        bias = _nat_bias(rpb[l], s_len // GRID_W)
        pool_scale_l = pool_scale[l].reshape(1, -1)

        h = _norm_mod(xl, n1w, sh1, sc1)
        hc = _norm_mod(xc, n1w, csh1, csc1)
        p = _in_proj(h, w_in, l, nw_tiles, tn_in, 0, d_in // tn_in, *qk_tiles)
        if last:
            pc = _in_proj(hc, w_in, l, nw_tiles, tn_in, k_lo // tn_in, (f_lo - k_lo) // tn_in, *qk_tiles)
            kc_blk, vc_blk = 0, blk(d_attn)
        else:
            pc = _in_proj(hc, w_in, l, nw_tiles, tn_in, 0, d_in // tn_in, *qk_tiles)
            kc_blk, vc_blk = blk(k_lo), blk(v_lo)
        y_attn = _nat_attention(p, pc, bias, n_heads, blk(q_lo), blk(k_lo), blk(v_lo), kc_blk, vc_blk)
        y_pool = _pool_mix(p, 0, pool_w_b[l], pool_scale_l)
        y_fnet = _fourier_mix(p, f_lo // d_fnet, cs_chan, fnet_w_b[l])
        m = _merge(h, y_pool, y_attn, y_fnet, l, w_gate_b, b_gate_r, w_bp_b, w_ba_b, w_bf_b)
        xl = _resid_proj(m, w_o_b, l, xl, g1, tk_o, 512)

        h2 = _norm_mod(xl, n2w, sh2, sc2)
        halo = _halo_norm(xl, n2w, sh2, sc2, tm_ffn)
        t = _ffn_up(h2, halo, l, w_up_b, conv_w, conv_b_r)
        xl = _resid_proj(t, w_down_b, l, xl, g2, tk_down, 512)

        if not last:
            yc_attn = _ctx_attention(pc, n_heads, blk(q_lo), blk(k_lo), blk(v_lo))
            yc_pool = _pool_mix(pc, 0, pool_w_b[l], pool_scale_l)
            yc_fnet = _fourier_mix(pc, f_lo // d_fnet, cs_chan, fnet_w_b[l])
            mc = _merge(hc, yc_pool, yc_attn, yc_fnet, l, w_gate_b, b_gate_r, w_bp_b, w_ba_b, w_bf_b)
            xc = _resid_proj(mc, w_o_b, l, xc, cg1, tk_o, 512)
            hc2 = _norm_mod(xc, n2w, csh2, csc2)
            halo_c = _halo_norm(xc, n2w, csh2, csc2, n_ctx)
            tc = _ffn_up(hc2, halo_c, l, w_up_b, conv_w, conv_b_r)
            xc = _resid_proj(tc, w_down_b, l, xc, cg2, tk_down, 512)
    return xl.reshape(batch, s_len, d)
```

```python
import functools
import math

import numpy as np
import jax
import jax.numpy as jnp
from jax import lax
from jax.experimental import pallas as pl
from jax.experimental.pallas import tpu as pltpu

F32 = jnp.float32
BF16 = jnp.bfloat16

GRID_W = 64
HEAD_DIM = 128
WIN_ROWS = 8
WIN_COLS = 16
POOL_WINDOWS = (2, 4, 8, 16)
N_GROUPS = 4
N_MOD = 6
CONV_W = 3
EPS = 1e-6

LANES = 128
SUBLANES_F32 = 8
SUBLANES_BF16 = 16
VMEM_CAP_BYTES = 56 * 1024 * 1024

ATTN_ROWS = 4
ATTN_KEY_ROWS = ATTN_ROWS + WIN_ROWS
ATTN_HEADS = 2
FFN_EPILOGUE_ROWS = 64
EPILOGUE_ROWS = 256
NEG_BIAS = -1e30
FFT_MIN_LEN = 1024


def _tile(n, pref):
    if n <= pref:
        return n
    t = pref - pref % LANES
    while t >= LANES:
        if n % t == 0:
            return t
        t -= LANES
    return n


def _params(n_axes, vmem_bytes):
    return pltpu.CompilerParams(
        dimension_semantics=("arbitrary",) * n_axes,
        vmem_limit_bytes=int(min(VMEM_CAP_BYTES, max(vmem_bytes, 16 * 1024 * 1024))),
    )


def _dot(a, b):
    return jnp.dot(a, b, preferred_element_type=F32)


def _dot_nt(a, b):
    return lax.dot_general(a, b, (((1,), (1,)), ((), ())), preferred_element_type=F32)


def _rms_mod(x, nw, sh, sc):
    ms = jnp.mean(x * x, axis=-1, keepdims=True)
    return (x * lax.rsqrt(ms + EPS) * nw) * (1.0 + sc) + sh


def _ada_kernel(s_ref, w_ref, b_ref, o_ref):
    s = s_ref[...]
    a = (s * jax.nn.sigmoid(s)).astype(BF16)
    o_ref[0] = _dot(a, w_ref[0].astype(BF16)) + b_ref[0]


def _ada(cond, w_ada, b_ada):
    n_layers, d, n = w_ada.shape
    tn = _tile(n, 1024)
    return pl.pallas_call(
        _ada_kernel,
        grid=(n_layers, n // tn),
        in_specs=[
            pl.BlockSpec((SUBLANES_F32, d), lambda l, j: (0, 0)),
            pl.BlockSpec((1, d, tn), lambda l, j: (l, 0, j)),
            pl.BlockSpec((1, 1, tn), lambda l, j: (l, 0, j)),
        ],
        out_specs=pl.BlockSpec((1, SUBLANES_F32, tn), lambda l, j: (l, 0, j)),
        out_shape=jax.ShapeDtypeStruct((n_layers, SUBLANES_F32, n), F32),
        compiler_params=_params(2, 2 * d * tn * 4 + 8 * 1024 * 1024),
        name="ada",
    )(cond, w_ada, b_ada.reshape(n_layers, 1, n))


def _norm_kernel(x_ref, nw_ref, sh_ref, sc_ref, o_ref):
    o_ref[...] = _rms_mod(x_ref[...], nw_ref[...], sh_ref[...], sc_ref[...]).astype(o_ref.dtype)


def _norm_mod(x, nw, sh, sc):
    m, d = x.shape
    tm = min(m, 512)
    vec = pl.BlockSpec((1, d), lambda i: (0, 0))
    return pl.pallas_call(
        _norm_kernel,
        grid=(m // tm,),
        in_specs=[pl.BlockSpec((tm, d), lambda i: (i, 0)), vec, vec, vec],
        out_specs=pl.BlockSpec((tm, d), lambda i: (i, 0)),
        out_shape=jax.ShapeDtypeStruct((m, d), BF16),
        compiler_params=_params(1, 6 * tm * d * 4),
        name="norm_mod",
    )(x, nw, sh, sc)


def _halo_kernel(xp_ref, xn_ref, nw_ref, sh_ref, sc_ref, o_ref):
    i = pl.program_id(0)
    n = pl.num_programs(0)
    hp = _rms_mod(xp_ref[...], nw_ref[...], sh_ref[...], sc_ref[...])
    hn = _rms_mod(xn_ref[...], nw_ref[...], sh_ref[...], sc_ref[...])
    hp = jnp.where(i > 0, hp, 0.0)
    hn = jnp.where(i < n - 1, hn, 0.0)
    o_ref[0] = jnp.concatenate([hn, hp], axis=0).astype(o_ref.dtype)


def _halo_norm(x, nw, sh, sc, tm):
    m, d = x.shape
    nblk = m // tm
    r = SUBLANES_F32
    per = tm // r
    last = m // r - 1
    vec = pl.BlockSpec((1, d), lambda i: (0, 0))
    return pl.pallas_call(
        _halo_kernel,
        grid=(nblk,),
        in_specs=[
            pl.BlockSpec((r, d), lambda i: (jnp.maximum(i * per - 1, 0), 0)),
            pl.BlockSpec((r, d), lambda i: (jnp.minimum((i + 1) * per, last), 0)),
            vec, vec, vec,
        ],
        out_specs=pl.BlockSpec((1, 2 * r, d), lambda i: (i, 0, 0)),
        out_shape=jax.ShapeDtypeStruct((nblk, 2 * r, d), BF16),
        compiler_params=_params(1, 16 * 1024 * 1024),
        name="halo_norm",
    )(x, x, nw, sh, sc)


def _inproj_kernel(a_ref, b_ref, nw_ref, o_ref, acc0, acc1, *, j_off, qk_lo, qk_hi):
    tm = a_ref.shape[0]
    j = pl.program_id(1)
    nj = pl.num_programs(1) - 1
    tile = j - 1 + j_off
    is_qk = jnp.logical_and(tile >= qk_lo, tile < qk_hi)
    rc = min(tm, EPILOGUE_ROWS)

    def matmul(acc):
        acc[...] = _dot(a_ref[...], b_ref[...].astype(BF16))

    def epilogue(acc):
        for r0 in range(0, tm, rc):
            for h in range(o_ref.shape[1] // HEAD_DIM):
                sl = slice(h * HEAD_DIM, (h + 1) * HEAD_DIM)
                xh = acc[r0:r0 + rc, sl]
                ms = jnp.mean(xh * xh, axis=-1, keepdims=True)
                normed = xh * lax.rsqrt(ms + EPS) * nw_ref[0, :, sl]
                o_ref[r0:r0 + rc, sl] = jnp.where(is_qk, normed, xh).astype(o_ref.dtype)

    odd = lax.rem(j, 2) == 1
    mid = jnp.logical_and(j > 0, j < nj)

    @pl.when(j == 0)
    def _():
        matmul(acc0)

    @pl.when(jnp.logical_and(mid, odd))
    def _():
        epilogue(acc0)
        matmul(acc1)

    @pl.when(jnp.logical_and(mid, jnp.logical_not(odd)))
    def _():
        epilogue(acc1)
        matmul(acc0)

    @pl.when(jnp.logical_and(j == nj, odd))
    def _():
        epilogue(acc0)

    @pl.when(jnp.logical_and(j == nj, jnp.logical_not(odd)))
    def _():
        epilogue(acc1)


def _inproj_plain_kernel(a_ref, b_ref, nw_ref, o_ref, *, j_off, qk_lo, qk_hi):
    acc = _dot(a_ref[...], b_ref[...])
    j = pl.program_id(1) + j_off
    is_qk = jnp.logical_and(j >= qk_lo, j < qk_hi)

    @pl.when(is_qk)
    def _():
        for h in range(acc.shape[1] // HEAD_DIM):
            sl = slice(h * HEAD_DIM, (h + 1) * HEAD_DIM)
            xh = acc[:, sl]
            ms = jnp.mean(xh * xh, axis=-1, keepdims=True)
            o_ref[:, sl] = (xh * lax.rsqrt(ms + EPS) * nw_ref[0, :, sl]).astype(o_ref.dtype)

    @pl.when(jnp.logical_not(is_qk))
    def _():
        o_ref[...] = acc.astype(o_ref.dtype)


def _in_proj(h, w_in, layer, nw_tiles, tn, j_off, n_tiles, qk_lo, qk_hi, lag):
    m, d = h.shape
    tm = min(m, 1024)
    if not lag:
        return pl.pallas_call(
            functools.partial(_inproj_plain_kernel, j_off=j_off, qk_lo=qk_lo, qk_hi=qk_hi),
            grid=(m // tm, n_tiles),
            in_specs=[
                pl.BlockSpec((tm, d), lambda i, j: (i, 0)),
                pl.BlockSpec((None, d, tn), lambda i, j: (layer, 0, j + j_off)),
                pl.BlockSpec((1, 1, tn), lambda i, j: (j + j_off, 0, 0)),
            ],
            out_specs=pl.BlockSpec((tm, tn), lambda i, j: (i, j)),
            out_shape=jax.ShapeDtypeStruct((m, n_tiles * tn), BF16),
            compiler_params=_params(2, 4 * tm * d + 4 * d * tn + 4 * tm * tn + 3 * tm * tn * 4),
            name="in_proj_plain",
        )(h, w_in, nw_tiles)
    acc = pltpu.VMEM((tm, tn), F32)
    return pl.pallas_call(
        functools.partial(_inproj_kernel, j_off=j_off, qk_lo=qk_lo, qk_hi=qk_hi),
        grid=(m // tm, n_tiles + 1),
        in_specs=[
            pl.BlockSpec((tm, d), lambda i, j: (i, 0)),
            pl.BlockSpec((None, d, tn), lambda i, j: (layer, 0, jnp.minimum(j, n_tiles - 1) + j_off)),
            pl.BlockSpec((1, 1, tn), lambda i, j: (jnp.maximum(j - 1, 0) + j_off, 0, 0)),
        ],
        out_specs=pl.BlockSpec((tm, tn), lambda i, j: (i, jnp.maximum(j - 1, 0))),
        out_shape=jax.ShapeDtypeStruct((m, n_tiles * tn), BF16),
        scratch_shapes=[acc, acc],
        compiler_params=_params(2, 4 * tm * d + 10 * d * tn + 4 * tm * tn + 6 * tm * tn * 4),
        name="in_proj",
    )(h, w_in, nw_tiles)


def _nat_kernel(q_ref, k_ref, v_ref, kc_ref, vc_ref, bias_ref, o_ref, *, grid_rows, scale, heads):
    b = pl.program_id(1)
    ks = jnp.clip(b * ATTN_ROWS - WIN_ROWS // 2, 0, grid_rows - ATTN_KEY_ROWS)
    start = pl.multiple_of(ks * GRID_W, GRID_W)
    nkeys = ATTN_KEY_ROWS * GRID_W
    for h in range(heads):
        sl = slice(h * HEAD_DIM, (h + 1) * HEAD_DIM)
        q = q_ref[:, sl]
        kw = k_ref[pl.ds(start, nkeys), sl]
        vw = v_ref[pl.ds(start, nkeys), sl]
        s = _dot_nt(q, kw) * scale + bias_ref[0, h]
        sc = _dot_nt(q, kc_ref[:, sl]) * scale
        m = jnp.maximum(jnp.max(s, axis=-1, keepdims=True), jnp.max(sc, axis=-1, keepdims=True))
        p = jnp.exp(s - m)
        pc = jnp.exp(sc - m)
        l = jnp.sum(p, axis=-1, keepdims=True) + jnp.sum(pc, axis=-1, keepdims=True)
        o = _dot(p.astype(BF16), vw) + _dot(pc.astype(BF16), vc_ref[:, sl])
        o_ref[:, sl] = (o / l).astype(o_ref.dtype)


def _nat_bias(rpb, grid_rows):
    n_heads = rpb.shape[0]
    nb = grid_rows // ATTN_ROWS
    c = np.arange(GRID_W)[:, None]
    kc = np.arange(GRID_W)[None, :]
    c_start = np.clip(c - WIN_COLS // 2, 0, GRID_W - WIN_COLS)
    col_ok = (kc >= c_start) & (kc < c_start + WIN_COLS)
    dc = kc - c + (WIN_COLS - 1)
    n_dc = 2 * WIN_COLS - 1
    expand = ((dc[None] == np.arange(n_dc)[:, None, None]) & col_ok[None]).astype(np.float32)
    blocks = jnp.einsum("hrd,dn->hrn", rpb, expand.reshape(n_dc, GRID_W * GRID_W),
                        precision=lax.Precision.HIGHEST)
    blocks = blocks.reshape(n_heads, 2 * WIN_ROWS - 1, GRID_W, GRID_W)
    blocks = jnp.where(col_ok, blocks, NEG_BIAS)
    masked = jnp.full((n_heads, GRID_W, GRID_W), NEG_BIAS, F32)
    variants = []
    for b in (0, 1, nb - 1):
        ks = int(np.clip(b * ATTN_ROWS - WIN_ROWS // 2, 0, grid_rows - ATTN_KEY_ROWS))
        q_rows = []
        for r_loc in range(ATTN_ROWS):
            r = b * ATTN_ROWS + r_loc
            r_start = int(np.clip(r - WIN_ROWS // 2, 0, grid_rows - WIN_ROWS))
            row = []
            for kr in range(ks, ks + ATTN_KEY_ROWS):
                ok = r_start <= kr < r_start + WIN_ROWS
                row.append(blocks[:, kr - r + (WIN_ROWS - 1)] if ok else masked)
            q_rows.append(jnp.concatenate(row, axis=-1))
        variants.append(jnp.concatenate(q_rows, axis=-2))
    return jnp.stack(variants)


def _nat_attention(p_lat, p_ctx, bias, n_heads, q_blk, k_blk, v_blk, kc_blk, vc_blk, hs):
    s_len = p_lat.shape[0]
    n_ctx = p_ctx.shape[0]
    grid_rows = s_len // GRID_W
    nb = grid_rows // ATTN_ROWS
    nq = ATTN_ROWS * GRID_W
    nk = ATTN_KEY_ROWS * GRID_W
    assert grid_rows % ATTN_ROWS == 0 and nb >= 3 and grid_rows >= ATTN_KEY_ROWS
    width = hs * HEAD_DIM
    assert n_heads % hs == 0 and all(blk % hs == 0 for blk in (q_blk, k_blk, v_blk, kc_blk, vc_blk))

    def variant(b):
        return jnp.where(b == 0, 0, jnp.where(b == nb - 1, 2, 1))

    def cols(rows, blk):
        mode = pl.Buffered(1 if (rows == s_len and hs > 2) else 2)
        return pl.BlockSpec((rows, width), lambda h, b: (0, blk // hs + h), pipeline_mode=mode)

    return pl.pallas_call(
        functools.partial(_nat_kernel, grid_rows=grid_rows, scale=HEAD_DIM ** -0.5, heads=hs),
        grid=(n_heads // hs, nb),
        in_specs=[
            pl.BlockSpec((nq, width), lambda h, b: (b, q_blk // hs + h)),
            cols(s_len, k_blk), cols(s_len, v_blk), cols(n_ctx, kc_blk), cols(n_ctx, vc_blk),
            pl.BlockSpec((1, hs, nq, nk), lambda h, b: (variant(b), h, 0, 0)),
        ],
        out_specs=pl.BlockSpec((nq, width), lambda h, b: (b, h)),
        out_shape=jax.ShapeDtypeStruct((s_len, n_heads * HEAD_DIM), BF16),
        compiler_params=_params(2, 8 * s_len * width + hs * (2 * nq * nk * 4 + 16 * nq * (nk + n_ctx) * 4)),
        name="nat_attention",
    )(p_lat, p_lat, p_lat, p_ctx, p_ctx, bias)


def _ctx_attn_kernel(q_ref, k_ref, v_ref, o_ref, *, scale):
    s = _dot_nt(q_ref[...], k_ref[...]) * scale
    m = jnp.max(s, axis=-1, keepdims=True)
    p = jnp.exp(s - m)
    l = jnp.sum(p, axis=-1, keepdims=True)
    o_ref[...] = (_dot(p.astype(BF16), v_ref[...]) / l).astype(o_ref.dtype)


def _ctx_attention(p_ctx, n_heads, q_blk, k_blk, v_blk):
    n = p_ctx.shape[0]
    return pl.pallas_call(
        functools.partial(_ctx_attn_kernel, scale=HEAD_DIM ** -0.5),
        grid=(n_heads,),
        in_specs=[
            pl.BlockSpec((n, HEAD_DIM), lambda h: (0, q_blk + h)),
            pl.BlockSpec((n, HEAD_DIM), lambda h: (0, k_blk + h)),
            pl.BlockSpec((n, HEAD_DIM), lambda h: (0, v_blk + h)),
        ],
        out_specs=pl.BlockSpec((n, HEAD_DIM), lambda h: (0, h)),
        out_shape=jax.ShapeDtypeStruct((n, n_heads * HEAD_DIM), BF16),
        compiler_params=_params(1, 16 * 1024 * 1024),
        name="ctx_attention",
    )(p_ctx, p_ctx, p_ctx)


def _pool_kernel(up_ref, um_ref, un_ref, pw_ref, ps_ref, o_ref, *, seq_len, gc):
    i = pl.program_id(0)
    n = pl.num_programs(0)
    tp = um_ref.shape[0]
    halo = up_ref.shape[0]
    um = um_ref[...]
    up = jnp.where(i > 0, up_ref[...], jnp.zeros_like(up_ref[...]))
    un = jnp.where(i < n - 1, un_ref[...], jnp.zeros_like(un_ref[...]))
    ue = jnp.concatenate([up, um, un], axis=0)
    trow = lax.broadcasted_iota(jnp.int32, (tp, tp + 2 * halo), 0)
    jcol = lax.broadcasted_iota(jnp.int32, (tp, tp + 2 * halo), 1)
    off = jcol - trow - halo
    t_abs = i * tp + lax.broadcasted_iota(jnp.int32, (tp, gc), 0)
    for g, w in enumerate(POOL_WINDOWS):
        lo_off = -(w // 2)
        hi_off = w - 1 - w // 2
        band = jnp.where(off >= lo_off, jnp.where(off <= hi_off, 1.0, 0.0), 0.0).astype(BF16)
        sl = slice(g * gc, (g + 1) * gc)
        wsum = _dot(band, ue[:, sl])
        lo = jnp.maximum(t_abs + lo_off, 0)
        hi = jnp.minimum(t_abs + hi_off, seq_len - 1)
        cnt = (hi - lo + 1).astype(F32)
        d = (wsum / cnt - um[:, sl].astype(F32)).astype(BF16)
        o_ref[:, sl] = (_dot(d, pw_ref[g]) * ps_ref[:, sl]).astype(o_ref.dtype)


def _pool_mix(p, col_blk, pool_w, pool_scale):
    m = p.shape[0]
    gc = pool_w.shape[-1]
    width = N_GROUPS * gc
    tp = min(m, 512)
    halo = SUBLANES_BF16
    per = tp // halo
    last = m // halo - 1
    return pl.pallas_call(
        functools.partial(_pool_kernel, seq_len=m, gc=gc),
        grid=(m // tp,),
        in_specs=[
            pl.BlockSpec((halo, width), lambda i: (jnp.maximum(i * per - 1, 0), col_blk)),
            pl.BlockSpec((tp, width), lambda i: (i, col_blk)),
            pl.BlockSpec((halo, width), lambda i: (jnp.minimum((i + 1) * per, last), col_blk)),
            pl.BlockSpec((N_GROUPS, gc, gc), lambda i: (0, 0, 0)),
            pl.BlockSpec((1, width), lambda i: (0, 0)),
        ],
        out_specs=pl.BlockSpec((tp, width), lambda i: (i, 0)),
        out_shape=jax.ShapeDtypeStruct((m, width), BF16),
        compiler_params=_params(1, 32 * 1024 * 1024),
        name="pool_mix",
    )(p, p, p, pool_w, pool_scale)


def _cos_sin(num, den):
    ang = (num % den).astype(F32) * (2.0 * math.pi / den)
    return jnp.cos(ang), jnp.sin(ang)


def _dft_tables(n):
    idx = jnp.arange(n, dtype=jnp.int32)
    return _cos_sin(idx[:, None] * idx[None, :], n)


def _fnet_chan_kernel(u_ref, cs_ref, o_ref, *, gc):
    for g in range(N_GROUPS):
        sl = slice(g * gc, (g + 1) * gc)
        o_ref[:, sl] = _dot(u_ref[:, sl], cs_ref[0]).astype(o_ref.dtype)


def _fnet_pos_kernel(t_ref, ab_ref, fw_ref, o_ref, *, scale, gc):
    acc = _dot(t_ref[...], ab_ref[...])
    for g in range(N_GROUPS):
        sl = slice(g * gc, (g + 1) * gc)
        f = (acc[:, sl] * scale).astype(BF16)
        o_ref[:, sl] = _dot(f, fw_ref[g]).astype(o_ref.dtype)


def _fourier_mix_dense(p, col_blk, cs_chan, fnet_w):
    m = p.shape[0]
    gc = fnet_w.shape[-1]
    width = N_GROUPS * gc
    cpos, spos = _dft_tables(m)
    t_pos = jnp.concatenate([cpos, -spos], axis=1).astype(BF16)
    ab = pl.pallas_call(
        functools.partial(_fnet_chan_kernel, gc=gc),
        grid=(2,),
        in_specs=[
            pl.BlockSpec((m, width), lambda c: (0, col_blk)),
            pl.BlockSpec((1, gc, gc), lambda c: (c, 0, 0)),
        ],
        out_specs=pl.BlockSpec((m, width), lambda c: (c, 0)),
        out_shape=jax.ShapeDtypeStruct((2 * m, width), BF16),
        compiler_params=_params(1, 32 * 1024 * 1024),
        name="fnet_chan",
    )(p, cs_chan)
    return pl.pallas_call(
        functools.partial(_fnet_pos_kernel, scale=1.0 / math.sqrt(m * gc), gc=gc),
        grid=(1,),
        in_specs=[
            pl.BlockSpec((m, 2 * m), lambda i: (0, 0)),
            pl.BlockSpec((2 * m, width), lambda i: (0, 0)),
            pl.BlockSpec((N_GROUPS, gc, gc), lambda i: (0, 0, 0)),
        ],
        out_specs=pl.BlockSpec((m, width), lambda i: (0, 0)),
        out_shape=jax.ShapeDtypeStruct((m, width), BF16),
        compiler_params=_params(1, 32 * 1024 * 1024),
        name="fnet_pos",
    )(t_pos, ab, fnet_w)


def _fft_stage1_kernel(f1_ref, u_ref, tc_ref, ts_ref, zr_ref, zi_ref, *, n1):
    y = _dot(f1_ref[...], u_ref[...])
    yr, yi = y[:n1], y[n1:]
    reps = u_ref.shape[1] // LANES
    tc = jnp.tile(tc_ref[...], (1, reps))
    ts = jnp.tile(ts_ref[...], (1, reps))
    zr_ref[...] = (yr * tc + yi * ts).astype(zr_ref.dtype)
    zi_ref[...] = (yi * tc - yr * ts).astype(zi_ref.dtype)


def _fft_stage2_kernel(fa_ref, fb_ref, zr_ref, zi_ref, cc_ref, sc_ref, fw_ref, o_ref, *, n2, gc, chunks, scale):
    width = N_GROUPS * gc
    p = _dot(fa_ref[...], zr_ref[...]) + _dot(fb_ref[...], zi_ref[...])
    for g in range(N_GROUPS):
        cols = [slice(ch * width + g * gc, ch * width + (g + 1) * gc) for ch in range(chunks)]
        pr = jnp.concatenate([p[:n2, c] for c in cols], axis=0).astype(BF16)
        pi = jnp.concatenate([p[n2:, c] for c in cols], axis=0).astype(BF16)
        f = ((_dot(pr, cc_ref[...]) + _dot(pi, sc_ref[...])) * scale).astype(BF16)
        y = _dot(f, fw_ref[g]).astype(o_ref.dtype)
        for ch, c in enumerate(cols):
            o_ref[:, c] = y[ch * n2:(ch + 1) * n2]


def _fourier_mix_fft(p, col_blk, cs_chan, fnet_w):
    m, d_in = p.shape
    gc = fnet_w.shape[-1]
    width = N_GROUPS * gc
    n2 = LANES
    n1 = m // n2
    i1 = jnp.arange(n1, dtype=jnp.int32)
    i2 = jnp.arange(n2, dtype=jnp.int32)
    c1, s1 = _cos_sin(i1[:, None] * i1[None, :], n1)
    f1 = jnp.concatenate([c1, -s1], axis=0).astype(BF16)
    tc, ts = _cos_sin(i1[:, None] * i2[None, :], m)
    tc = jnp.repeat(tc, LANES, axis=1)
    ts = jnp.repeat(ts, LANES, axis=1)
    c2, s2 = _cos_sin(i2[:, None] * i2[None, :], n2)
    fa = jnp.concatenate([c2, -s2], axis=0).astype(BF16)
    fb = jnp.concatenate([s2, c2], axis=0).astype(BF16)
    u = p[:, col_blk * width:(col_blk + 1) * width].reshape(n1, n2 * width)
    z_shape = jax.ShapeDtypeStruct((n2 * n1, width), BF16)
    zr, zi = pl.pallas_call(
        functools.partial(_fft_stage1_kernel, n1=n1),
        grid=(n2,),
        in_specs=[
            pl.BlockSpec((2 * n1, n1), lambda j: (0, 0)),
            pl.BlockSpec((n1, width), lambda j: (0, j)),
            pl.BlockSpec((n1, LANES), lambda j: (0, j)),
            pl.BlockSpec((n1, LANES), lambda j: (0, j)),
        ],
        out_specs=[pl.BlockSpec((n1, width), lambda j: (j, 0))] * 2,
        out_shape=[z_shape, z_shape],
        compiler_params=_params(1, 16 * 1024 * 1024),
        name="fft_stage1",
    )(f1, u, tc, ts)
    chunks = min(n1, 8)
    tn = chunks * width
    y = pl.pallas_call(
        functools.partial(_fft_stage2_kernel, n2=n2, gc=gc, chunks=chunks, scale=1.0 / math.sqrt(m * gc)),
        grid=(n1 // chunks,),
        in_specs=[
            pl.BlockSpec((2 * n2, n2), lambda j: (0, 0)),
            pl.BlockSpec((2 * n2, n2), lambda j: (0, 0)),
            pl.BlockSpec((n2, tn), lambda j: (0, j)),
            pl.BlockSpec((n2, tn), lambda j: (0, j)),
            pl.BlockSpec((None, gc, gc), lambda j: (0, 0, 0)),
            pl.BlockSpec((None, gc, gc), lambda j: (1, 0, 0)),
            pl.BlockSpec((N_GROUPS, gc, gc), lambda j: (0, 0, 0)),
        ],
        out_specs=pl.BlockSpec((n2, tn), lambda j: (0, j)),
        out_shape=jax.ShapeDtypeStruct((n2, n1 * width), BF16),
        compiler_params=_params(1, 40 * 1024 * 1024),
        name="fft_stage2",
    )(fa, fb, zr.reshape(n2, n1 * width), zi.reshape(n2, n1 * width), cs_chan, cs_chan, fnet_w)
    return y.reshape(m, width)


def _fourier_mix(p, col_blk, cs_chan, fnet_w):
    if p.shape[0] >= FFT_MIN_LEN and p.shape[0] % (LANES * SUBLANES_F32) == 0:
        return _fourier_mix_fft(p, col_blk, cs_chan, fnet_w)
    return _fourier_mix_dense(p, col_blk, cs_chan, fnet_w)


def _merge_kernel(h_ref, yp_ref, ya_ref, yf_ref, wg0, wg1, wg2, bg0, bg1, bg2, wp, wa, wf, o_ref):
    h = h_ref[...]

    def gate(w_ref, b_ref):
        return jax.nn.sigmoid(_dot(h, w_ref[...]) + b_ref[...])

    m = gate(wg0, bg0) * _dot(yp_ref[...], wp[...])
    m = m + gate(wg1, bg1) * _dot(ya_ref[...], wa[...])
    m = m + gate(wg2, bg2) * _dot(yf_ref[...], wf[...])
    o_ref[...] = m.astype(o_ref.dtype)


def _merge(h, y_pool, y_attn, y_fnet, layer, w_gate, b_gate, w_bp, w_ba, w_bf, big):
    m, d = h.shape
    dp, da, df = y_pool.shape[1], y_attn.shape[1], y_fnet.shape[1]
    tm = min(m, 1024 if big else 512)
    tn = _tile(d, 256)
    nj = d // tn

    def act(width):
        if big and width == d:
            return pl.BlockSpec((tm, width), lambda i, j: (i, 0), pipeline_mode=pl.Buffered(1))
        return pl.BlockSpec((tm, width), lambda i, j: (i, 0))

    def gate_w(b):
        return pl.BlockSpec((None, d, tn), lambda i, j: (layer, 0, b * nj + j))

    def gate_b(b):
        return pl.BlockSpec((None, 1, tn), lambda i, j: (layer, 0, b * nj + j))

    def br_w(k):
        return pl.BlockSpec((None, k, tn), lambda i, j: (layer, 0, j))

    vmem = 4 * tm * (d + dp + da + df) + 4 * tn * (3 * d + dp + da + df) + 12 * tm * tn * 4
    return pl.pallas_call(
        _merge_kernel,
        grid=(m // tm, nj),
        in_specs=[act(d), act(dp), act(da), act(df), gate_w(0), gate_w(1), gate_w(2),
                  gate_b(0), gate_b(1), gate_b(2), br_w(dp), br_w(da), br_w(df)],
        out_specs=pl.BlockSpec((tm, tn), lambda i, j: (i, j)),
        out_shape=jax.ShapeDtypeStruct((m, d), BF16),
        compiler_params=_params(2, vmem),
        name="merge",
    )(h, y_pool, y_attn, y_fnet, w_gate, w_gate, w_gate, b_gate, b_gate, b_gate, w_bp, w_ba, w_bf)


def _resid_kernel(a_ref, w_ref, x_ref, g_ref, o_ref, acc_ref):
    k = pl.program_id(2)
    d = _dot(a_ref[...], w_ref[...])

    @pl.when(k == 0)
    def _():
        acc_ref[...] = d

    @pl.when(k > 0)
    def _():
        acc_ref[...] += d

    @pl.when(k == pl.num_programs(2) - 1)
    def _():
        o_ref[...] = x_ref[...] + g_ref[...] * acc_ref[...]


def _resid_full_kernel(a_ref, w_ref, x_ref, g_ref, o_ref):
    o_ref[...] = x_ref[...] + g_ref[...] * _dot(a_ref[...], w_ref[...])


def _resid_proj(a, w, layer, x, g, tk, tn_pref, single_a=False, tm_pref=1024):
    m, kk = a.shape
    n = w.shape[2]
    tm = min(m, tm_pref)
    tn = _tile(n, tn_pref)
    if tk == kk:
        return pl.pallas_call(
            _resid_full_kernel,
            grid=(m // tm, n // tn),
            in_specs=[
                pl.BlockSpec((tm, kk), lambda i, j: (i, 0), pipeline_mode=pl.Buffered(1 if single_a else 2)),
                pl.BlockSpec((None, kk, tn), lambda i, j: (layer, 0, j)),
                pl.BlockSpec((tm, tn), lambda i, j: (i, j)),
                pl.BlockSpec((1, tn), lambda i, j: (0, j)),
            ],
            out_specs=pl.BlockSpec((tm, tn), lambda i, j: (i, j)),
            out_shape=jax.ShapeDtypeStruct((m, n), F32),
            compiler_params=_params(2, 4 * tm * kk + 4 * kk * tn + 7 * tm * tn * 4),
            name="resid_proj_full",
        )(a, w, x, g)
    return pl.pallas_call(
        _resid_kernel,
        grid=(m // tm, n // tn, kk // tk),
        in_specs=[
            pl.BlockSpec((tm, tk), lambda i, j, k: (i, k)),
            pl.BlockSpec((None, tk, tn), lambda i, j, k: (layer, k, j)),
            pl.BlockSpec((tm, tn), lambda i, j, k: (i, j)),
            pl.BlockSpec((1, tn), lambda i, j, k: (0, j)),
        ],
        out_specs=pl.BlockSpec((tm, tn), lambda i, j, k: (i, j)),
        out_shape=jax.ShapeDtypeStruct((m, n), F32),
        scratch_shapes=[pltpu.VMEM((tm, tn), F32)],
        compiler_params=_params(3, 4 * tm * tk + 4 * tk * tn + 7 * tm * tn * 4),
        name="resid_proj",
    )(a, w, x, g)


def _ffn_epilogue(ug, uv, cwg_ref, cwv_ref, cbg_ref, cbv_ref, o_ref, tm):
    rc = min(tm, FFN_EPILOGUE_ROWS)

    def conv(u_ref, cw_ref, cb_ref, r0):
        um = u_ref[r0:r0 + rc, :]
        u_up = u_ref[r0 + 1:r0 + rc + 1, :]
        if r0 == 0:
            row = lax.broadcasted_iota(jnp.int32, um.shape, 0)
            prev_row = u_ref[tm + 2 * SUBLANES_F32 - 1:tm + 2 * SUBLANES_F32, :]
            u_dn = jnp.where(row == 0, prev_row, pltpu.roll(um, 1, 0))
        else:
            u_dn = u_ref[r0 - 1:r0 + rc - 1, :]
        return u_dn * cw_ref[0:1] + um * cw_ref[1:2] + u_up * cw_ref[2:3] + cb_ref[...]

    for r0 in range(0, tm, rc):
        gate = conv(ug, cwg_ref, cbg_ref, r0)
        val = conv(uv, cwv_ref, cbv_ref, r0)
        o_ref[r0:r0 + rc, :] = (gate * jax.nn.sigmoid(gate) * val).astype(o_ref.dtype)


def _ffn_up_plain_kernel(a_ref, halo_ref, wg_ref, wv_ref, cwg_ref, cwv_ref, cbg_ref, cbv_ref, o_ref,
                         a_ext, ug, uv):
    tm = a_ref.shape[0]

    @pl.when(pl.program_id(1) == 0)
    def _():
        a_ext[:tm, :] = a_ref[...]
        a_ext[tm:, :] = halo_ref[0]

    a = a_ext[...]
    ug[...] = _dot(a, wg_ref[...])
    uv[...] = _dot(a, wv_ref[...])
    _ffn_epilogue(ug, uv, cwg_ref, cwv_ref, cbg_ref, cbv_ref, o_ref, tm)


def _ffn_up_kernel(a_ref, halo_ref, wg_ref, wv_ref, cwg_ref, cwv_ref, cbg_ref, cbv_ref, o_ref,
                   a_ext, ug0, uv0, ug1, uv1):
    tm = a_ref.shape[0]
    j = pl.program_id(1)
    nj = pl.num_programs(1) - 1

    @pl.when(j == 0)
    def _():
        a_ext[:tm, :] = a_ref[...]
        a_ext[tm:, :] = halo_ref[0]

    def matmul(ug, uv):
        a = a_ext[...]
        ug[...] = _dot(a, wg_ref[...])
        uv[...] = _dot(a, wv_ref[...])

    def epilogue(ug, uv):
        _ffn_epilogue(ug, uv, cwg_ref, cwv_ref, cbg_ref, cbv_ref, o_ref, tm)

    odd = lax.rem(j, 2) == 1

    @pl.when(j == 0)
    def _():
        matmul(ug0, uv0)

    @pl.when(jnp.logical_and(jnp.logical_and(j > 0, j < nj), odd))
    def _():
        epilogue(ug0, uv0)
        matmul(ug1, uv1)

    @pl.when(jnp.logical_and(jnp.logical_and(j > 0, j < nj), jnp.logical_not(odd)))
    def _():
        epilogue(ug1, uv1)
        matmul(ug0, uv0)

    @pl.when(jnp.logical_and(j == nj, odd))
    def _():
        epilogue(ug0, uv0)

    @pl.when(jnp.logical_and(j == nj, jnp.logical_not(odd)))
    def _():
        epilogue(ug1, uv1)


def _ffn_up(h2, halo, layer, w_up, conv_w, conv_b, lag):
    m, d = h2.shape
    f = w_up.shape[2] // 2
    tm = m // halo.shape[0]
    ext = halo.shape[1]
    tn = _tile(f, 256)
    nj = f // tn
    vmem = 2 * tm * d + 2 * (tm + ext) * d + 8 * d * tn + 4 * tm * tn + 20 * (tm + ext) * tn * 4
    a_spec = pl.BlockSpec((tm, d), lambda i, j: (i, 0), pipeline_mode=pl.Buffered(1))
    halo_spec = pl.BlockSpec((1, ext, d), lambda i, j: (i, 0, 0))
    if not lag:
        def cols(rows, half):
            return pl.BlockSpec((None, rows, tn), lambda i, j: (layer, 0, half * nj + j))

        u_scr = pltpu.VMEM((tm + ext, tn), F32)
        return pl.pallas_call(
            _ffn_up_plain_kernel,
            grid=(m // tm, nj),
            in_specs=[a_spec, halo_spec, cols(d, 0), cols(d, 1), cols(CONV_W, 0), cols(CONV_W, 1),
                      cols(1, 0), cols(1, 1)],
            out_specs=pl.BlockSpec((tm, tn), lambda i, j: (i, j)),
            out_shape=jax.ShapeDtypeStruct((m, f), BF16),
            scratch_shapes=[pltpu.VMEM((tm + ext, d), BF16), u_scr, u_scr],
            compiler_params=_params(2, vmem),
            name="ffn_up_plain",
        )(h2, halo, w_up, w_up, conv_w, conv_w, conv_b, conv_b)

    def w_cols(half):
        return pl.BlockSpec((None, d, tn), lambda i, j: (layer, 0, half * nj + jnp.minimum(j, nj - 1)))

    def ep_cols(rows, half):
        return pl.BlockSpec((None, rows, tn), lambda i, j: (layer, 0, half * nj + jnp.maximum(j - 1, 0)))

    u_scratch = pltpu.VMEM((tm + ext, tn), F32)
    return pl.pallas_call(
        _ffn_up_kernel,
        grid=(m // tm, nj + 1),
        in_specs=[
            pl.BlockSpec((tm, d), lambda i, j: (i, 0), pipeline_mode=pl.Buffered(1)),
            pl.BlockSpec((1, ext, d), lambda i, j: (i, 0, 0)),
            w_cols(0), w_cols(1), ep_cols(CONV_W, 0), ep_cols(CONV_W, 1), ep_cols(1, 0), ep_cols(1, 1),
        ],
        out_specs=pl.BlockSpec((tm, tn), lambda i, j: (i, jnp.maximum(j - 1, 0))),
        out_shape=jax.ShapeDtypeStruct((m, f), BF16),
        scratch_shapes=[pltpu.VMEM((tm + ext, d), BF16), u_scratch, u_scratch, u_scratch, u_scratch],
        compiler_params=_params(2, vmem),
        name="ffn_up",
    )(h2, halo, w_up, w_up, conv_w, conv_w, conv_b, conv_b)


def kernel(x, c, ctx, c_ctx, w_ada, b_ada, norm1_w, norm2_w, w_in, pool_w, pool_scale, q_norm_w, k_norm_w, rpb, fnet_w, w_gate, b_gate, w_br_pool, w_br_attn, w_br_fnet, w_o, w_up, conv_w, conv_b, w_down):
    n_layers, d, d_in = w_in.shape
    batch, s_len, _ = x.shape
    n_ctx = ctx.shape[1]
    assert batch == 1
    d_pool = w_br_pool.shape[1]
    d_attn = w_br_attn.shape[1]
    d_fnet = w_br_fnet.shape[1]
    n_heads = d_attn // HEAD_DIM
    f = w_down.shape[1]
    assert d_in == d_pool + 3 * d_attn + d_fnet and s_len % GRID_W == 0

    assert d_fnet == d_pool
    q_lo, k_lo, v_lo, f_lo = d_pool, d_pool + d_attn, d_pool + 2 * d_attn, d_pool + 3 * d_attn

    w_in_b = w_in.astype(BF16)
    w_gate_b = w_gate.astype(BF16)
    w_bp_b = w_br_pool.astype(BF16)
    w_ba_b = w_br_attn.astype(BF16)
    w_bf_b = w_br_fnet.astype(BF16)
    w_o_b = w_o.astype(BF16)
    w_up_b = w_up.astype(BF16)
    w_down_b = w_down.astype(BF16)
    pool_w_b = pool_w.astype(BF16)
    fnet_w_b = fnet_w.astype(BF16)
    b_gate_r = b_gate.reshape(n_layers, 1, -1)
    conv_b_r = conv_b.reshape(n_layers, 1, -1)
    ones_p = jnp.ones((d_pool,), F32)
    ones_vf = jnp.ones((d_attn + d_fnet,), F32)

    cs_chan = jnp.stack(_dft_tables(fnet_w.shape[-1])).astype(BF16)

    cond = jnp.concatenate([c.reshape(1, d), c_ctx.reshape(1, d), jnp.zeros((SUBLANES_F32 - 2, d), F32)], axis=0)
    mod = _ada(cond, w_ada, b_ada)

    tk_o = d
    tk_down = _tile(f, 5632)
    tm_ffn = min(s_len, 1024)
    blk = lambda col: col // HEAD_DIM

    xl = x.reshape(s_len, d)
    xc = ctx.reshape(n_ctx, d)
    for l in range(n_layers):
        last = l == n_layers - 1
        alt = l % 2 == 1
        tn_in = _tile(d_pool, 1024 if alt else 512)
        assert d_pool % tn_in == 0 and d_attn % tn_in == 0
        qk_tiles = (q_lo // tn_in, v_lo // tn_in)
        w_in_l = w_in_b if alt else w_in
        sh1, sc1, g1, sh2, sc2, g2 = [mod[l, 0:1, i * d:(i + 1) * d] for i in range(N_MOD)]
        csh1, csc1, cg1, csh2, csc2, cg2 = [mod[l, 1:2, i * d:(i + 1) * d] for i in range(N_MOD)]
        n1w = norm1_w[l].reshape(1, d)
        n2w = norm2_w[l].reshape(1, d)
        nw_tiles = jnp.concatenate(
            [ones_p, jnp.tile(q_norm_w[l], n_heads), jnp.tile(k_norm_w[l], n_heads), ones_vf]
        ).reshape(d_in // tn_in, 1, tn_in)
        bias = _nat_bias(rpb[l], s_len // GRID_W)
        pool_scale_l = pool_scale[l].reshape(1, -1)

        h = _norm_mod(xl, n1w, sh1, sc1)
        hc = _norm_mod(xc, n1w, csh1, csc1)
        p = _in_proj(h, w_in_l, l, nw_tiles, tn_in, 0, d_in // tn_in, *qk_tiles, not alt)
        if last:
            pc = _in_proj(hc, w_in_l, l, nw_tiles, tn_in, k_lo // tn_in, (f_lo - k_lo) // tn_in, *qk_tiles, not alt)
            kc_blk, vc_blk = 0, blk(d_attn)
        else:
            pc = _in_proj(hc, w_in_l, l, nw_tiles, tn_in, 0, d_in // tn_in, *qk_tiles, not alt)
            kc_blk, vc_blk = blk(k_lo), blk(v_lo)
        y_attn = _nat_attention(p, pc, bias, n_heads, blk(q_lo), blk(k_lo), blk(v_lo), kc_blk, vc_blk,
                                2 * ATTN_HEADS if alt else ATTN_HEADS)
        y_pool = _pool_mix(p, 0, pool_w_b[l], pool_scale_l)
        y_fnet = _fourier_mix(p, f_lo // d_fnet, cs_chan, fnet_w_b[l])
        m = _merge(h, y_pool, y_attn, y_fnet, l, w_gate_b, b_gate_r, w_bp_b, w_ba_b, w_bf_b, alt)
        xl = _resid_proj(m, w_o_b, l, xl, g1, tk_o, 1024 if alt else 512, single_a=alt)

        h2 = _norm_mod(xl, n2w, sh2, sc2)
        halo = _halo_norm(xl, n2w, sh2, sc2, tm_ffn)
        t = _ffn_up(h2, halo, l, w_up_b, conv_w, conv_b_r, not alt)
        if alt:
            xl = _resid_proj(t, w_down_b, l, xl, g2, f, 512, tm_pref=512)
        else:
            xl = _resid_proj(t, w_down_b, l, xl, g2, tk_down, 512)

        if not last:
            yc_attn = _ctx_attention(pc, n_heads, blk(q_lo), blk(k_lo), blk(v_lo))
            yc_pool = _pool_mix(pc, 0, pool_w_b[l], pool_scale_l)
            yc_fnet = _fourier_mix(pc, f_lo // d_fnet, cs_chan, fnet_w_b[l])
            mc = _merge(hc, yc_pool, yc_attn, yc_fnet, l, w_gate_b, b_gate_r, w_bp_b, w_ba_b, w_bf_b, False)
            xc = _resid_proj(mc, w_o_b, l, xc, cg1, tk_o, 512)
            hc2 = _norm_mod(xc, n2w, csh2, csc2)
            halo_c = _halo_norm(xc, n2w, csh2, csc2, n_ctx)
            tc = _ffn_up(hc2, halo_c, l, w_up_b, conv_w, conv_b_r, True)
            xc = _resid_proj(tc, w_down_b, l, xc, cg2, tk_down, 512)
    return xl.reshape(batch, s_len, d)
```

```python
import functools
import math

import numpy as np
import jax
import jax.numpy as jnp
from jax import lax
from jax.experimental import pallas as pl
from jax.experimental.pallas import tpu as pltpu

F32 = jnp.float32
BF16 = jnp.bfloat16

GRID_W = 64
HEAD_DIM = 128
WIN_ROWS = 8
WIN_COLS = 16
POOL_WINDOWS = (2, 4, 8, 16)
N_GROUPS = 4
N_MOD = 6
CONV_W = 3
EPS = 1e-6

LANES = 128
SUBLANES_F32 = 8
SUBLANES_BF16 = 16
VMEM_CAP_BYTES = 56 * 1024 * 1024

ATTN_ROWS = 4
ATTN_KEY_ROWS = ATTN_ROWS + WIN_ROWS
ATTN_HEADS = 4
FFT_STAGE1_COLS = 8
EPILOGUE_ROWS = 256
NEG_BIAS = -1e30
FFT_MIN_LEN = 1024


def _tile(n, pref):
    if n <= pref:
        return n
    t = pref - pref % LANES
    while t >= LANES:
        if n % t == 0:
            return t
        t -= LANES
    return n


def _params(n_axes, vmem_bytes):
    return pltpu.CompilerParams(
        dimension_semantics=("arbitrary",) * n_axes,
        vmem_limit_bytes=int(min(VMEM_CAP_BYTES, max(vmem_bytes, 16 * 1024 * 1024))),
    )


def _dot(a, b):
    return jnp.dot(a, b, preferred_element_type=F32)


def _dot_nt(a, b):
    return lax.dot_general(a, b, (((1,), (1,)), ((), ())), preferred_element_type=F32)


def _rms_mod(x, nw, sh, sc):
    ms = jnp.mean(x * x, axis=-1, keepdims=True)
    return (x * lax.rsqrt(ms + EPS) * nw) * (1.0 + sc) + sh


def _ada_kernel(s_ref, w_ref, b_ref, o_ref):
    s = s_ref[...]
    a = (s * jax.nn.sigmoid(s)).astype(BF16)
    o_ref[0] = _dot(a, w_ref[0].astype(BF16)) + b_ref[0]


def _ada(cond, w_ada, b_ada):
    n_layers, d, n = w_ada.shape
    tn = _tile(n, 1024)
    return pl.pallas_call(
        _ada_kernel,
        grid=(n_layers, n // tn),
        in_specs=[
            pl.BlockSpec((SUBLANES_F32, d), lambda l, j: (0, 0)),
            pl.BlockSpec((1, d, tn), lambda l, j: (l, 0, j)),
            pl.BlockSpec((1, 1, tn), lambda l, j: (l, 0, j)),
        ],
        out_specs=pl.BlockSpec((1, SUBLANES_F32, tn), lambda l, j: (l, 0, j)),
        out_shape=jax.ShapeDtypeStruct((n_layers, SUBLANES_F32, n), F32),
        compiler_params=_params(2, 2 * d * tn * 4 + 8 * 1024 * 1024),
        name="ada",
    )(cond, w_ada, b_ada.reshape(n_layers, 1, n))


def _norm_kernel(x_ref, nw_ref, sh_ref, sc_ref, o_ref):
    o_ref[...] = _rms_mod(x_ref[...], nw_ref[...], sh_ref[...], sc_ref[...]).astype(o_ref.dtype)


def _norm_mod(x, nw, sh, sc):
    m, d = x.shape
    tm = min(m, 512)
    vec = pl.BlockSpec((1, d), lambda i: (0, 0))
    return pl.pallas_call(
        _norm_kernel,
        grid=(m // tm,),
        in_specs=[pl.BlockSpec((tm, d), lambda i: (i, 0)), vec, vec, vec],
        out_specs=pl.BlockSpec((tm, d), lambda i: (i, 0)),
        out_shape=jax.ShapeDtypeStruct((m, d), BF16),
        compiler_params=_params(1, 6 * tm * d * 4),
        name="norm_mod",
    )(x, nw, sh, sc)


def _halo_kernel(xp_ref, xn_ref, nw_ref, sh_ref, sc_ref, o_ref):
    i = pl.program_id(0)
    n = pl.num_programs(0)
    hp = _rms_mod(xp_ref[...], nw_ref[...], sh_ref[...], sc_ref[...])
    hn = _rms_mod(xn_ref[...], nw_ref[...], sh_ref[...], sc_ref[...])
    hp = jnp.where(i > 0, hp, 0.0)
    hn = jnp.where(i < n - 1, hn, 0.0)
    o_ref[0] = jnp.concatenate([hn, hp], axis=0).astype(o_ref.dtype)


def _halo_norm(x, nw, sh, sc, tm):
    m, d = x.shape
    nblk = m // tm
    r = SUBLANES_F32
    per = tm // r
    last = m // r - 1
    vec = pl.BlockSpec((1, d), lambda i: (0, 0))
    return pl.pallas_call(
        _halo_kernel,
        grid=(nblk,),
        in_specs=[
            pl.BlockSpec((r, d), lambda i: (jnp.maximum(i * per - 1, 0), 0)),
            pl.BlockSpec((r, d), lambda i: (jnp.minimum((i + 1) * per, last), 0)),
            vec, vec, vec,
        ],
        out_specs=pl.BlockSpec((1, 2 * r, d), lambda i: (i, 0, 0)),
        out_shape=jax.ShapeDtypeStruct((nblk, 2 * r, d), BF16),
        compiler_params=_params(1, 16 * 1024 * 1024),
        name="halo_norm",
    )(x, x, nw, sh, sc)


def _inproj_kernel(a_ref, b_ref, nw_ref, o_ref, acc0, acc1, *, j_off, qk_lo, qk_hi):
    tm = a_ref.shape[0]
    j = pl.program_id(1)
    nj = pl.num_programs(1) - 1
    tile = j - 1 + j_off
    is_qk = jnp.logical_and(tile >= qk_lo, tile < qk_hi)
    rc = min(tm, EPILOGUE_ROWS)

    def matmul(acc):
        acc[...] = _dot(a_ref[...], b_ref[...].astype(BF16))

    def epilogue(acc):
        for r0 in range(0, tm, rc):
            for h in range(o_ref.shape[1] // HEAD_DIM):
                sl = slice(h * HEAD_DIM, (h + 1) * HEAD_DIM)
                xh = acc[r0:r0 + rc, sl]
                ms = jnp.mean(xh * xh, axis=-1, keepdims=True)
                normed = xh * lax.rsqrt(ms + EPS) * nw_ref[0, :, sl]
                o_ref[r0:r0 + rc, sl] = jnp.where(is_qk, normed, xh).astype(o_ref.dtype)

    odd = lax.rem(j, 2) == 1
    mid = jnp.logical_and(j > 0, j < nj)

    @pl.when(j == 0)
    def _():
        matmul(acc0)

    @pl.when(jnp.logical_and(mid, odd))
    def _():
        epilogue(acc0)
        matmul(acc1)

    @pl.when(jnp.logical_and(mid, jnp.logical_not(odd)))
    def _():
        epilogue(acc1)
        matmul(acc0)

    @pl.when(jnp.logical_and(j == nj, odd))
    def _():
        epilogue(acc0)

    @pl.when(jnp.logical_and(j == nj, jnp.logical_not(odd)))
    def _():
        epilogue(acc1)


def _inproj_plain_kernel(a_ref, b_ref, nw_ref, o_ref, *, j_off, qk_lo, qk_hi):
    acc = _dot(a_ref[...], b_ref[...])
    j = pl.program_id(1) + j_off
    is_qk = jnp.logical_and(j >= qk_lo, j < qk_hi)

    @pl.when(is_qk)
    def _():
        for h in range(acc.shape[1] // HEAD_DIM):
            sl = slice(h * HEAD_DIM, (h + 1) * HEAD_DIM)
            xh = acc[:, sl]
            ms = jnp.mean(xh * xh, axis=-1, keepdims=True)
            o_ref[:, sl] = (xh * lax.rsqrt(ms + EPS) * nw_ref[0, :, sl]).astype(o_ref.dtype)

    @pl.when(jnp.logical_not(is_qk))
    def _():
        o_ref[...] = acc.astype(o_ref.dtype)


def _in_proj(h, w_in, layer, nw_tiles, tn, j_off, n_tiles, qk_lo, qk_hi, lag):
    m, d = h.shape
    tm = min(m, 1024)
    if not lag:
        return pl.pallas_call(
            functools.partial(_inproj_plain_kernel, j_off=j_off, qk_lo=qk_lo, qk_hi=qk_hi),
            grid=(m // tm, n_tiles),
            in_specs=[
                pl.BlockSpec((tm, d), lambda i, j: (i, 0)),
                pl.BlockSpec((None, d, tn), lambda i, j: (layer, 0, j + j_off)),
                pl.BlockSpec((1, 1, tn), lambda i, j: (j + j_off, 0, 0)),
            ],
            out_specs=pl.BlockSpec((tm, tn), lambda i, j: (i, j)),
            out_shape=jax.ShapeDtypeStruct((m, n_tiles * tn), BF16),
            compiler_params=_params(2, 4 * tm * d + 4 * d * tn + 4 * tm * tn + 3 * tm * tn * 4),
            name="in_proj_plain",
        )(h, w_in, nw_tiles)
    acc = pltpu.VMEM((tm, tn), F32)
    return pl.pallas_call(
        functools.partial(_inproj_kernel, j_off=j_off, qk_lo=qk_lo, qk_hi=qk_hi),
        grid=(m // tm, n_tiles + 1),
        in_specs=[
            pl.BlockSpec((tm, d), lambda i, j: (i, 0)),
            pl.BlockSpec((None, d, tn), lambda i, j: (layer, 0, jnp.minimum(j, n_tiles - 1) + j_off)),
            pl.BlockSpec((1, 1, tn), lambda i, j: (jnp.maximum(j - 1, 0) + j_off, 0, 0)),
        ],
        out_specs=pl.BlockSpec((tm, tn), lambda i, j: (i, jnp.maximum(j - 1, 0))),
        out_shape=jax.ShapeDtypeStruct((m, n_tiles * tn), BF16),
        scratch_shapes=[acc, acc],
        compiler_params=_params(2, 4 * tm * d + 10 * d * tn + 4 * tm * tn + 6 * tm * tn * 4),
        name="in_proj",
    )(h, w_in, nw_tiles)


def _nat_kernel(q_ref, k_ref, v_ref, kc_ref, vc_ref, bias_ref, o_ref, *, grid_rows, scale, heads):
    b = pl.program_id(1)
    ks = jnp.clip(b * ATTN_ROWS - WIN_ROWS // 2, 0, grid_rows - ATTN_KEY_ROWS)
    start = pl.multiple_of(ks * GRID_W, GRID_W)
    nkeys = ATTN_KEY_ROWS * GRID_W
    for h in range(heads):
        sl = slice(h * HEAD_DIM, (h + 1) * HEAD_DIM)
        q = q_ref[:, sl]
        kw = k_ref[pl.ds(start, nkeys), sl]
        vw = v_ref[pl.ds(start, nkeys), sl]
        s = _dot_nt(q, kw) * scale + bias_ref[0, h]
        sc = _dot_nt(q, kc_ref[:, sl]) * scale
        m = jnp.maximum(jnp.max(s, axis=-1, keepdims=True), jnp.max(sc, axis=-1, keepdims=True))
        p = jnp.exp(s - m)
        pc = jnp.exp(sc - m)
        l = jnp.sum(p, axis=-1, keepdims=True) + jnp.sum(pc, axis=-1, keepdims=True)
        o = _dot(p.astype(BF16), vw) + _dot(pc.astype(BF16), vc_ref[:, sl])
        o_ref[:, sl] = (o / l).astype(o_ref.dtype)


def _nat_bias(rpb, grid_rows):
    n_heads = rpb.shape[0]
    nb = grid_rows // ATTN_ROWS
    c = np.arange(GRID_W)[:, None]
    kc = np.arange(GRID_W)[None, :]
    c_start = np.clip(c - WIN_COLS // 2, 0, GRID_W - WIN_COLS)
    col_ok = (kc >= c_start) & (kc < c_start + WIN_COLS)
    dc = kc - c + (WIN_COLS - 1)
    n_dc = 2 * WIN_COLS - 1
    expand = ((dc[None] == np.arange(n_dc)[:, None, None]) & col_ok[None]).astype(np.float32)
    blocks = jnp.einsum("hrd,dn->hrn", rpb, expand.reshape(n_dc, GRID_W * GRID_W),
                        precision=lax.Precision.HIGHEST)
    blocks = blocks.reshape(n_heads, 2 * WIN_ROWS - 1, GRID_W, GRID_W)
    blocks = jnp.where(col_ok, blocks, NEG_BIAS)
    masked = jnp.full((n_heads, GRID_W, GRID_W), NEG_BIAS, F32)
    variants = []
    for b in (0, 1, nb - 1):
        ks = int(np.clip(b * ATTN_ROWS - WIN_ROWS // 2, 0, grid_rows - ATTN_KEY_ROWS))
        q_rows = []
        for r_loc in range(ATTN_ROWS):
            r = b * ATTN_ROWS + r_loc
            r_start = int(np.clip(r - WIN_ROWS // 2, 0, grid_rows - WIN_ROWS))
            row = []
            for kr in range(ks, ks + ATTN_KEY_ROWS):
                ok = r_start <= kr < r_start + WIN_ROWS
                row.append(blocks[:, kr - r + (WIN_ROWS - 1)] if ok else masked)
            q_rows.append(jnp.concatenate(row, axis=-1))
        variants.append(jnp.concatenate(q_rows, axis=-2))
    return jnp.stack(variants)


def _nat_attention(p_lat, p_ctx, bias, n_heads, q_blk, k_blk, v_blk, kc_blk, vc_blk, hs):
    s_len = p_lat.shape[0]
    n_ctx = p_ctx.shape[0]
    grid_rows = s_len // GRID_W
    nb = grid_rows // ATTN_ROWS
    nq = ATTN_ROWS * GRID_W
    nk = ATTN_KEY_ROWS * GRID_W
    assert grid_rows % ATTN_ROWS == 0 and nb >= 3 and grid_rows >= ATTN_KEY_ROWS
    while n_heads % hs or any(blk % hs for blk in (q_blk, k_blk, v_blk, kc_blk, vc_blk)):
        hs //= 2
    width = hs * HEAD_DIM

    def variant(b):
        return jnp.where(b == 0, 0, jnp.where(b == nb - 1, 2, 1))

    def cols(rows, blk):
        mode = pl.Buffered(1 if (rows == s_len and hs > 2) else 2)
        return pl.BlockSpec((rows, width), lambda h, b: (0, blk // hs + h), pipeline_mode=mode)

    return pl.pallas_call(
        functools.partial(_nat_kernel, grid_rows=grid_rows, scale=HEAD_DIM ** -0.5, heads=hs),
        grid=(n_heads // hs, nb),
        in_specs=[
            pl.BlockSpec((nq, width), lambda h, b: (b, q_blk // hs + h)),
            cols(s_len, k_blk), cols(s_len, v_blk), cols(n_ctx, kc_blk), cols(n_ctx, vc_blk),
            pl.BlockSpec((1, hs, nq, nk), lambda h, b: (variant(b), h, 0, 0)),
        ],
        out_specs=pl.BlockSpec((nq, width), lambda h, b: (b, h)),
        out_shape=jax.ShapeDtypeStruct((s_len, n_heads * HEAD_DIM), BF16),
        compiler_params=_params(2, 8 * s_len * width + hs * (2 * nq * nk * 4 + 16 * nq * (nk + n_ctx) * 4)),
        name="nat_attention",
    )(p_lat, p_lat, p_lat, p_ctx, p_ctx, bias)


def _ctx_attn_kernel(q_ref, k_ref, v_ref, o_ref, *, scale):
    s = _dot_nt(q_ref[...], k_ref[...]) * scale
    m = jnp.max(s, axis=-1, keepdims=True)
    p = jnp.exp(s - m)
    l = jnp.sum(p, axis=-1, keepdims=True)
    o_ref[...] = (_dot(p.astype(BF16), v_ref[...]) / l).astype(o_ref.dtype)


def _ctx_attention(p_ctx, n_heads, q_blk, k_blk, v_blk):
    n = p_ctx.shape[0]
    return pl.pallas_call(
        functools.partial(_ctx_attn_kernel, scale=HEAD_DIM ** -0.5),
        grid=(n_heads,),
        in_specs=[
            pl.BlockSpec((n, HEAD_DIM), lambda h: (0, q_blk + h)),
            pl.BlockSpec((n, HEAD_DIM), lambda h: (0, k_blk + h)),
            pl.BlockSpec((n, HEAD_DIM), lambda h: (0, v_blk + h)),
        ],
        out_specs=pl.BlockSpec((n, HEAD_DIM), lambda h: (0, h)),
        out_shape=jax.ShapeDtypeStruct((n, n_heads * HEAD_DIM), BF16),
        compiler_params=_params(1, 16 * 1024 * 1024),
        name="ctx_attention",
    )(p_ctx, p_ctx, p_ctx)


def _pool_kernel(up_ref, um_ref, un_ref, pw_ref, ps_ref, o_ref, *, seq_len, gc):
    i = pl.program_id(0)
    n = pl.num_programs(0)
    tp = um_ref.shape[0]
    halo = up_ref.shape[0]
    um = um_ref[...]
    up = jnp.where(i > 0, up_ref[...], jnp.zeros_like(up_ref[...]))
    un = jnp.where(i < n - 1, un_ref[...], jnp.zeros_like(un_ref[...]))
    ue = jnp.concatenate([up, um, un], axis=0)
    trow = lax.broadcasted_iota(jnp.int32, (tp, tp + 2 * halo), 0)
    jcol = lax.broadcasted_iota(jnp.int32, (tp, tp + 2 * halo), 1)
    off = jcol - trow - halo
    t_abs = i * tp + lax.broadcasted_iota(jnp.int32, (tp, gc), 0)
    for g, w in enumerate(POOL_WINDOWS):
        lo_off = -(w // 2)
        hi_off = w - 1 - w // 2
        band = jnp.where(off >= lo_off, jnp.where(off <= hi_off, 1.0, 0.0), 0.0).astype(BF16)
        sl = slice(g * gc, (g + 1) * gc)
        wsum = _dot(band, ue[:, sl])
        lo = jnp.maximum(t_abs + lo_off, 0)
        hi = jnp.minimum(t_abs + hi_off, seq_len - 1)
        cnt = (hi - lo + 1).astype(F32)
        d = (wsum / cnt - um[:, sl].astype(F32)).astype(BF16)
        o_ref[:, sl] = (_dot(d, pw_ref[g]) * ps_ref[:, sl]).astype(o_ref.dtype)


def _pool_mix(p, col_blk, pool_w, pool_scale):
    m = p.shape[0]
    gc = pool_w.shape[-1]
    width = N_GROUPS * gc
    tp = min(m, 512)
    halo = SUBLANES_BF16
    per = tp // halo
    last = m // halo - 1
    return pl.pallas_call(
        functools.partial(_pool_kernel, seq_len=m, gc=gc),
        grid=(m // tp,),
        in_specs=[
            pl.BlockSpec((halo, width), lambda i: (jnp.maximum(i * per - 1, 0), col_blk)),
            pl.BlockSpec((tp, width), lambda i: (i, col_blk)),
            pl.BlockSpec((halo, width), lambda i: (jnp.minimum((i + 1) * per, last), col_blk)),
            pl.BlockSpec((N_GROUPS, gc, gc), lambda i: (0, 0, 0)),
            pl.BlockSpec((1, width), lambda i: (0, 0)),
        ],
        out_specs=pl.BlockSpec((tp, width), lambda i: (i, 0)),
        out_shape=jax.ShapeDtypeStruct((m, width), BF16),
        compiler_params=_params(1, 32 * 1024 * 1024),
        name="pool_mix",
    )(p, p, p, pool_w, pool_scale)


def _cos_sin(num, den):
    ang = (num % den).astype(F32) * (2.0 * math.pi / den)
    return jnp.cos(ang), jnp.sin(ang)


def _dft_tables(n):
    idx = jnp.arange(n, dtype=jnp.int32)
    return _cos_sin(idx[:, None] * idx[None, :], n)


def _fnet_chan_kernel(u_ref, cs_ref, o_ref, *, gc):
    for g in range(N_GROUPS):
        sl = slice(g * gc, (g + 1) * gc)
        o_ref[:, sl] = _dot(u_ref[:, sl], cs_ref[0]).astype(o_ref.dtype)


def _fnet_pos_kernel(t_ref, ab_ref, fw_ref, o_ref, *, scale, gc):
    acc = _dot(t_ref[...], ab_ref[...])
    for g in range(N_GROUPS):
        sl = slice(g * gc, (g + 1) * gc)
        f = (acc[:, sl] * scale).astype(BF16)
        o_ref[:, sl] = _dot(f, fw_ref[g]).astype(o_ref.dtype)


def _fourier_mix_dense(p, col_blk, cs_chan, fnet_w):
    m = p.shape[0]
    gc = fnet_w.shape[-1]
    width = N_GROUPS * gc
    cpos, spos = _dft_tables(m)
    t_pos = jnp.concatenate([cpos, -spos], axis=1).astype(BF16)
    ab = pl.pallas_call(
        functools.partial(_fnet_chan_kernel, gc=gc),
        grid=(2,),
        in_specs=[
            pl.BlockSpec((m, width), lambda c: (0, col_blk)),
            pl.BlockSpec((1, gc, gc), lambda c: (c, 0, 0)),
        ],
        out_specs=pl.BlockSpec((m, width), lambda c: (c, 0)),
        out_shape=jax.ShapeDtypeStruct((2 * m, width), BF16),
        compiler_params=_params(1, 32 * 1024 * 1024),
        name="fnet_chan",
    )(p, cs_chan)
    return pl.pallas_call(
        functools.partial(_fnet_pos_kernel, scale=1.0 / math.sqrt(m * gc), gc=gc),
        grid=(1,),
        in_specs=[
            pl.BlockSpec((m, 2 * m), lambda i: (0, 0)),
            pl.BlockSpec((2 * m, width), lambda i: (0, 0)),
            pl.BlockSpec((N_GROUPS, gc, gc), lambda i: (0, 0, 0)),
        ],
        out_specs=pl.BlockSpec((m, width), lambda i: (0, 0)),
        out_shape=jax.ShapeDtypeStruct((m, width), BF16),
        compiler_params=_params(1, 32 * 1024 * 1024),
        name="fnet_pos",
    )(t_pos, ab, fnet_w)


def _fft_stage1_kernel(f1_ref, u_ref, tc_ref, ts_ref, zr_ref, zi_ref, *, n1):
    width = zr_ref.shape[1]
    reps = width // LANES
    for t in range(u_ref.shape[1] // width):
        y = _dot(f1_ref[...], u_ref[:, t * width:(t + 1) * width])
        yr, yi = y[:n1], y[n1:]
        tc = jnp.tile(tc_ref[:, t * LANES:(t + 1) * LANES], (1, reps))
        ts = jnp.tile(ts_ref[:, t * LANES:(t + 1) * LANES], (1, reps))
        zr_ref[t * n1:(t + 1) * n1, :] = (yr * tc + yi * ts).astype(zr_ref.dtype)
        zi_ref[t * n1:(t + 1) * n1, :] = (yi * tc - yr * ts).astype(zi_ref.dtype)


def _fft_stage2_kernel(fa_ref, fb_ref, zr_ref, zi_ref, cc_ref, sc_ref, fw_ref, o_ref, *, n2, gc, chunks, scale):
    width = N_GROUPS * gc
    p = _dot(fa_ref[...], zr_ref[...]) + _dot(fb_ref[...], zi_ref[...])
    for g in range(N_GROUPS):
        cols = [slice(ch * width + g * gc, ch * width + (g + 1) * gc) for ch in range(chunks)]
        pr = jnp.concatenate([p[:n2, c] for c in cols], axis=0).astype(BF16)
        pi = jnp.concatenate([p[n2:, c] for c in cols], axis=0).astype(BF16)
        f = ((_dot(pr, cc_ref[...]) + _dot(pi, sc_ref[...])) * scale).astype(BF16)
        y = _dot(f, fw_ref[g]).astype(o_ref.dtype)
        for ch, c in enumerate(cols):
            o_ref[:, c] = y[ch * n2:(ch + 1) * n2]


def _fourier_mix_fft(p, col_blk, cs_chan, fnet_w):
    m, d_in = p.shape
    gc = fnet_w.shape[-1]
    width = N_GROUPS * gc
    n2 = LANES
    n1 = m // n2
    i1 = jnp.arange(n1, dtype=jnp.int32)
    i2 = jnp.arange(n2, dtype=jnp.int32)
    c1, s1 = _cos_sin(i1[:, None] * i1[None, :], n1)
    f1 = jnp.concatenate([c1, -s1], axis=0).astype(BF16)
    tc, ts = _cos_sin(i1[:, None] * i2[None, :], m)
    tc = jnp.repeat(tc, LANES, axis=1)
    ts = jnp.repeat(ts, LANES, axis=1)
    c2, s2 = _cos_sin(i2[:, None] * i2[None, :], n2)
    fa = jnp.concatenate([c2, -s2], axis=0).astype(BF16)
    fb = jnp.concatenate([s2, c2], axis=0).astype(BF16)
    u = p[:, col_blk * width:(col_blk + 1) * width].reshape(n1, n2 * width)
    z_shape = jax.ShapeDtypeStruct((n2 * n1, width), BF16)
    sb = FFT_STAGE1_COLS
    zr, zi = pl.pallas_call(
        functools.partial(_fft_stage1_kernel, n1=n1),
        grid=(n2 // sb,),
        in_specs=[
            pl.BlockSpec((2 * n1, n1), lambda j: (0, 0)),
            pl.BlockSpec((n1, sb * width), lambda j: (0, j)),
            pl.BlockSpec((n1, sb * LANES), lambda j: (0, j)),
            pl.BlockSpec((n1, sb * LANES), lambda j: (0, j)),
        ],
        out_specs=[pl.BlockSpec((sb * n1, width), lambda j: (j, 0))] * 2,
        out_shape=[z_shape, z_shape],
        compiler_params=_params(1, 16 * 1024 * 1024),
        name="fft_stage1",
    )(f1, u, tc, ts)
    chunks = min(n1, 8)
    tn = chunks * width
    y = pl.pallas_call(
        functools.partial(_fft_stage2_kernel, n2=n2, gc=gc, chunks=chunks, scale=1.0 / math.sqrt(m * gc)),
        grid=(n1 // chunks,),
        in_specs=[
            pl.BlockSpec((2 * n2, n2), lambda j: (0, 0)),
            pl.BlockSpec((2 * n2, n2), lambda j: (0, 0)),
            pl.BlockSpec((n2, tn), lambda j: (0, j)),
            pl.BlockSpec((n2, tn), lambda j: (0, j)),
            pl.BlockSpec((None, gc, gc), lambda j: (0, 0, 0)),
            pl.BlockSpec((None, gc, gc), lambda j: (1, 0, 0)),
            pl.BlockSpec((N_GROUPS, gc, gc), lambda j: (0, 0, 0)),
        ],
        out_specs=pl.BlockSpec((n2, tn), lambda j: (0, j)),
        out_shape=jax.ShapeDtypeStruct((n2, n1 * width), BF16),
        compiler_params=_params(1, 40 * 1024 * 1024),
        name="fft_stage2",
    )(fa, fb, zr.reshape(n2, n1 * width), zi.reshape(n2, n1 * width), cs_chan, cs_chan, fnet_w)
    return y.reshape(m, width)


def _fourier_mix(p, col_blk, cs_chan, fnet_w):
    if p.shape[0] >= FFT_MIN_LEN and p.shape[0] % (LANES * SUBLANES_F32) == 0:
        return _fourier_mix_fft(p, col_blk, cs_chan, fnet_w)
    return _fourier_mix_dense(p, col_blk, cs_chan, fnet_w)


def _merge_kernel(h_ref, yp_ref, ya_ref, yf_ref, wg0, wg1, wg2, bg0, bg1, bg2, wp, wa, wf, o_ref):
    h = h_ref[...]

    def gate(w_ref, b_ref):
        return jax.nn.sigmoid(_dot(h, w_ref[...]) + b_ref[...])

    m = gate(wg0, bg0) * _dot(yp_ref[...], wp[...])
    m = m + gate(wg1, bg1) * _dot(ya_ref[...], wa[...])
    m = m + gate(wg2, bg2) * _dot(yf_ref[...], wf[...])
    o_ref[...] = m.astype(o_ref.dtype)


def _merge(h, y_pool, y_attn, y_fnet, layer, w_gate, b_gate, w_bp, w_ba, w_bf, big):
    m, d = h.shape
    dp, da, df = y_pool.shape[1], y_attn.shape[1], y_fnet.shape[1]
    tm = min(m, 1024 if big else 512)
    tn = _tile(d, 256)
    nj = d // tn

    def act(width):
        if big and width == d:
            return pl.BlockSpec((tm, width), lambda i, j: (i, 0), pipeline_mode=pl.Buffered(1))
        return pl.BlockSpec((tm, width), lambda i, j: (i, 0))

    def gate_w(b):
        return pl.BlockSpec((None, d, tn), lambda i, j: (layer, 0, b * nj + j))

    def gate_b(b):
        return pl.BlockSpec((None, 1, tn), lambda i, j: (layer, 0, b * nj + j))

    def br_w(k):
        return pl.BlockSpec((None, k, tn), lambda i, j: (layer, 0, j))

    vmem = 4 * tm * (d + dp + da + df) + 4 * tn * (3 * d + dp + da + df) + 12 * tm * tn * 4
    return pl.pallas_call(
        _merge_kernel,
        grid=(m // tm, nj),
        in_specs=[act(d), act(dp), act(da), act(df), gate_w(0), gate_w(1), gate_w(2),
                  gate_b(0), gate_b(1), gate_b(2), br_w(dp), br_w(da), br_w(df)],
        out_specs=pl.BlockSpec((tm, tn), lambda i, j: (i, j)),
        out_shape=jax.ShapeDtypeStruct((m, d), BF16),
        compiler_params=_params(2, vmem),
        name="merge",
    )(h, y_pool, y_attn, y_fnet, w_gate, w_gate, w_gate, b_gate, b_gate, b_gate, w_bp, w_ba, w_bf)


def _resid_kernel(a_ref, w_ref, x_ref, g_ref, o_ref, acc_ref):
    k = pl.program_id(2)
    d = _dot(a_ref[...], w_ref[...])

    @pl.when(k == 0)
    def _():
        acc_ref[...] = d

    @pl.when(k > 0)
    def _():
        acc_ref[...] += d

    @pl.when(k == pl.num_programs(2) - 1)
    def _():
        o_ref[...] = x_ref[...] + g_ref[...] * acc_ref[...]


def _resid_full_kernel(a_ref, w_ref, x_ref, g_ref, o_ref):
    o_ref[...] = x_ref[...] + g_ref[...] * _dot(a_ref[...], w_ref[...])


def _resid_proj(a, w, layer, x, g, tk, tn_pref, single_a=False, tm_pref=1024):
    m, kk = a.shape
    n = w.shape[2]
    tm = min(m, tm_pref)
    tn = _tile(n, tn_pref)
    if tk == kk:
        return pl.pallas_call(
            _resid_full_kernel,
            grid=(m // tm, n // tn),
            in_specs=[
                pl.BlockSpec((tm, kk), lambda i, j: (i, 0), pipeline_mode=pl.Buffered(1 if single_a else 2)),
                pl.BlockSpec((None, kk, tn), lambda i, j: (layer, 0, j)),
                pl.BlockSpec((tm, tn), lambda i, j: (i, j)),
                pl.BlockSpec((1, tn), lambda i, j: (0, j)),
            ],
            out_specs=pl.BlockSpec((tm, tn), lambda i, j: (i, j)),
            out_shape=jax.ShapeDtypeStruct((m, n), F32),
            compiler_params=_params(2, 4 * tm * kk + 4 * kk * tn + 7 * tm * tn * 4),
            name="resid_proj_full",
        )(a, w, x, g)
    return pl.pallas_call(
        _resid_kernel,
        grid=(m // tm, n // tn, kk // tk),
        in_specs=[
            pl.BlockSpec((tm, tk), lambda i, j, k: (i, k)),
            pl.BlockSpec((None, tk, tn), lambda i, j, k: (layer, k, j)),
            pl.BlockSpec((tm, tn), lambda i, j, k: (i, j)),
            pl.BlockSpec((1, tn), lambda i, j, k: (0, j)),
        ],
        out_specs=pl.BlockSpec((tm, tn), lambda i, j, k: (i, j)),
        out_shape=jax.ShapeDtypeStruct((m, n), F32),
        scratch_shapes=[pltpu.VMEM((tm, tn), F32)],
        compiler_params=_params(3, 4 * tm * tk + 4 * tk * tn + 7 * tm * tn * 4),
        name="resid_proj",
    )(a, w, x, g)


def _ffn_up_kernel(a_ref, halo_ref, wg_ref, wv_ref, cwg_ref, cwv_ref, cbg_ref, cbv_ref, o_ref, a_ext):
    tm = a_ref.shape[0]

    @pl.when(pl.program_id(1) == 0)
    def _():
        a_ext[:tm, :] = a_ref[...]
        a_ext[tm:, :] = halo_ref[0]

    a = a_ext[...]
    row = lax.broadcasted_iota(jnp.int32, (tm, wg_ref.shape[1]), 0)

    def conv(w_ref, cw_ref, cb_ref):
        u = _dot(a, w_ref[...].astype(BF16))
        um = u[:tm]
        next_row = u[tm:tm + 1]
        prev_row = u[tm + 2 * SUBLANES_F32 - 1:tm + 2 * SUBLANES_F32]
        u_dn = jnp.where(row == 0, prev_row, pltpu.roll(um, 1, 0))
        u_up = jnp.where(row == tm - 1, next_row, pltpu.roll(um, tm - 1, 0))
        return u_dn * cw_ref[0:1] + um * cw_ref[1:2] + u_up * cw_ref[2:3] + cb_ref[...]

    gate = conv(wg_ref, cwg_ref, cbg_ref)
    val = conv(wv_ref, cwv_ref, cbv_ref)
    o_ref[...] = (gate * jax.nn.sigmoid(gate) * val).astype(o_ref.dtype)


def _ffn_up(h2, halo, layer, w_up, conv_w, conv_b):
    m, d = h2.shape
    f = w_up.shape[2] // 2
    tm = m // halo.shape[0]
    ext = halo.shape[1]
    tn = _tile(f, 256)
    nj = f // tn
    w_bytes = jnp.dtype(w_up.dtype).itemsize
    a_bufs = 1 if w_bytes == 4 else 2
    vmem = (2 * a_bufs * tm * d + 2 * (tm + ext) * d + (4 * w_bytes + 4) * d * tn + 4 * tm * tn
            + 12 * (tm + ext) * tn * 4)

    def cols(rows, half):
        return pl.BlockSpec((None, rows, tn), lambda i, j: (layer, 0, half * nj + j))

    return pl.pallas_call(
        _ffn_up_kernel,
        grid=(m // tm, nj),
        in_specs=[
            pl.BlockSpec((tm, d), lambda i, j: (i, 0), pipeline_mode=pl.Buffered(a_bufs)),
            pl.BlockSpec((1, ext, d), lambda i, j: (i, 0, 0)),
            cols(d, 0), cols(d, 1), cols(CONV_W, 0), cols(CONV_W, 1), cols(1, 0), cols(1, 1),
        ],
        out_specs=pl.BlockSpec((tm, tn), lambda i, j: (i, j)),
        out_shape=jax.ShapeDtypeStruct((m, f), BF16),
        scratch_shapes=[pltpu.VMEM((tm + ext, d), BF16)],
        compiler_params=_params(2, vmem),
        name="ffn_up",
    )(h2, halo, w_up, w_up, conv_w, conv_w, conv_b, conv_b)


def kernel(x, c, ctx, c_ctx, w_ada, b_ada, norm1_w, norm2_w, w_in, pool_w, pool_scale, q_norm_w, k_norm_w, rpb, fnet_w, w_gate, b_gate, w_br_pool, w_br_attn, w_br_fnet, w_o, w_up, conv_w, conv_b, w_down):
    n_layers, d, d_in = w_in.shape
    batch, s_len, _ = x.shape
    n_ctx = ctx.shape[1]
    assert batch == 1
    d_pool = w_br_pool.shape[1]
    d_attn = w_br_attn.shape[1]
    d_fnet = w_br_fnet.shape[1]
    n_heads = d_attn // HEAD_DIM
    f = w_down.shape[1]
    assert d_in == d_pool + 3 * d_attn + d_fnet and s_len % GRID_W == 0

    assert d_fnet == d_pool
    q_lo, k_lo, v_lo, f_lo = d_pool, d_pool + d_attn, d_pool + 2 * d_attn, d_pool + 3 * d_attn

    w_gate_b = w_gate.astype(BF16)
    w_bp_b = w_br_pool.astype(BF16)
    w_ba_b = w_br_attn.astype(BF16)
    w_bf_b = w_br_fnet.astype(BF16)
    w_o_b = w_o.astype(BF16)
    w_up_b0 = w_up[:1].astype(BF16)
    w_down_b = w_down.astype(BF16)
    pool_w_b = pool_w.astype(BF16)
    fnet_w_b = fnet_w.astype(BF16)
    b_gate_r = b_gate.reshape(n_layers, 1, -1)
    conv_b_r = conv_b.reshape(n_layers, 1, -1)
    ones_p = jnp.ones((d_pool,), F32)
    ones_vf = jnp.ones((d_attn + d_fnet,), F32)

    cs_chan = jnp.stack(_dft_tables(fnet_w.shape[-1])).astype(BF16)

    cond = jnp.concatenate([c.reshape(1, d), c_ctx.reshape(1, d), jnp.zeros((SUBLANES_F32 - 2, d), F32)], axis=0)
    mod = _ada(cond, w_ada, b_ada)

    tm_ffn = min(s_len, 1024)
    blk = lambda col: col // HEAD_DIM
    tn_in = _tile(d_pool, 512)
    assert d_pool % tn_in == 0 and d_attn % tn_in == 0
    qk_tiles = (q_lo // tn_in, v_lo // tn_in)

    xl = x.reshape(s_len, d)
    xc = ctx.reshape(n_ctx, d)
    for l in range(n_layers):
        last = l == n_layers - 1
        alt = l % 2 == 1
        sh1, sc1, g1, sh2, sc2, g2 = [mod[l, 0:1, i * d:(i + 1) * d] for i in range(N_MOD)]
        csh1, csc1, cg1, csh2, csc2, cg2 = [mod[l, 1:2, i * d:(i + 1) * d] for i in range(N_MOD)]
        n1w = norm1_w[l].reshape(1, d)
        n2w = norm2_w[l].reshape(1, d)
        nw_tiles = jnp.concatenate(
            [ones_p, jnp.tile(q_norm_w[l], n_heads), jnp.tile(k_norm_w[l], n_heads), ones_vf]
        ).reshape(d_in // tn_in, 1, tn_in)
        bias = _nat_bias(rpb[l], s_len // GRID_W)
        pool_scale_l = pool_scale[l].reshape(1, -1)

        h = _norm_mod(xl, n1w, sh1, sc1)
        hc = _norm_mod(xc, n1w, csh1, csc1)
        p = _in_proj(h, w_in, l, nw_tiles, tn_in, 0, d_in // tn_in, *qk_tiles, True)
        if last:
            pc = _in_proj(hc, w_in, l, nw_tiles, tn_in, k_lo // tn_in, (f_lo - k_lo) // tn_in, *qk_tiles, True)
            kc_blk, vc_blk = 0, blk(d_attn)
        else:
            pc = _in_proj(hc, w_in, l, nw_tiles, tn_in, 0, d_in // tn_in, *qk_tiles, True)
            kc_blk, vc_blk = blk(k_lo), blk(v_lo)
        y_attn = _nat_attention(p, pc, bias, n_heads, blk(q_lo), blk(k_lo), blk(v_lo), kc_blk, vc_blk,
                                2 * ATTN_HEADS if alt else ATTN_HEADS)
        y_pool = _pool_mix(p, 0, pool_w_b[l], pool_scale_l)
        y_fnet = _fourier_mix(p, f_lo // d_fnet, cs_chan, fnet_w_b[l])
        m = _merge(h, y_pool, y_attn, y_fnet, l, w_gate_b, b_gate_r, w_bp_b, w_ba_b, w_bf_b, True)
        xl = _resid_proj(m, w_o_b, l, xl, g1, d, 512)

        h2 = _norm_mod(xl, n2w, sh2, sc2)
        halo = _halo_norm(xl, n2w, sh2, sc2, tm_ffn)
        t = _ffn_up(h2, halo, l, w_up if alt else w_up_b0, conv_w, conv_b_r)
        xl = _resid_proj(t, w_down_b, l, xl, g2, f, 512, tm_pref=512)

        if not last:
            yc_attn = _ctx_attention(pc, n_heads, blk(q_lo), blk(k_lo), blk(v_lo))
            yc_pool = _pool_mix(pc, 0, pool_w_b[l], pool_scale_l)
            yc_fnet = _fourier_mix(pc, f_lo // d_fnet, cs_chan, fnet_w_b[l])
            mc = _merge(hc, yc_pool, yc_attn, yc_fnet, l, w_gate_b, b_gate_r, w_bp_b, w_ba_b, w_bf_b, False)
            xc = _resid_proj(mc, w_o_b, l, xc, cg1, d, 512)
            hc2 = _norm_mod(xc, n2w, csh2, csc2)
            halo_c = _halo_norm(xc, n2w, csh2, csc2, n_ctx)
            tc = _ffn_up(hc2, halo_c, l, w_up if alt else w_up_b0, conv_w, conv_b_r)
            xc = _resid_proj(tc, w_down_b, l, xc, cg2, f, 512, tm_pref=512)
    return xl.reshape(batch, s_len, d)
```

```python
import functools
import math

import numpy as np
import jax
import jax.numpy as jnp
from jax import lax
from jax.experimental import pallas as pl
from jax.experimental.pallas import tpu as pltpu

F32 = jnp.float32
BF16 = jnp.bfloat16

GRID_W = 64
HEAD_DIM = 128
WIN_ROWS = 8
WIN_COLS = 16
POOL_WINDOWS = (2, 4, 8, 16)
N_GROUPS = 4
N_MOD = 6
CONV_W = 3
EPS = 1e-6

LANES = 128
SUBLANES_F32 = 8
SUBLANES_BF16 = 16
VMEM_CAP_BYTES = 56 * 1024 * 1024

ATTN_ROWS = 4
ATTN_KEY_ROWS = ATTN_ROWS + WIN_ROWS
ATTN_HEADS = 8
FFT_STAGE1_COLS = 8
EPILOGUE_ROWS = 256
NEG_BIAS = -1e30
FFT_MIN_LEN = 1024


def _tile(n, pref):
    if n <= pref:
        return n
    t = pref - pref % LANES
    while t >= LANES:
        if n % t == 0:
            return t
        t -= LANES
    return n


def _params(n_axes, vmem_bytes):
    return pltpu.CompilerParams(
        dimension_semantics=("arbitrary",) * n_axes,
        vmem_limit_bytes=int(min(VMEM_CAP_BYTES, max(vmem_bytes, 16 * 1024 * 1024))),
    )


def _dot(a, b):
    return jnp.dot(a, b, preferred_element_type=F32)


def _dot_nt(a, b):
    return lax.dot_general(a, b, (((1,), (1,)), ((), ())), preferred_element_type=F32)


def _rms_mod(x, nw, sh, sc):
    ms = jnp.mean(x * x, axis=-1, keepdims=True)
    return (x * lax.rsqrt(ms + EPS) * nw) * (1.0 + sc) + sh


def _ada_kernel(s_ref, w_ref, b_ref, o_ref):
    s = s_ref[...]
    a = (s * jax.nn.sigmoid(s)).astype(BF16)
    o_ref[0] = _dot(a, w_ref[0].astype(BF16)) + b_ref[0]


def _ada(cond, w_ada, b_ada):
    n_layers, d, n = w_ada.shape
    tn = _tile(n, 1024)
    return pl.pallas_call(
        _ada_kernel,
        grid=(n_layers, n // tn),
        in_specs=[
            pl.BlockSpec((SUBLANES_F32, d), lambda l, j: (0, 0)),
            pl.BlockSpec((1, d, tn), lambda l, j: (l, 0, j)),
            pl.BlockSpec((1, 1, tn), lambda l, j: (l, 0, j)),
        ],
        out_specs=pl.BlockSpec((1, SUBLANES_F32, tn), lambda l, j: (l, 0, j)),
        out_shape=jax.ShapeDtypeStruct((n_layers, SUBLANES_F32, n), F32),
        compiler_params=_params(2, 2 * d * tn * 4 + 8 * 1024 * 1024),
        name="ada",
    )(cond, w_ada, b_ada.reshape(n_layers, 1, n))


def _norm_kernel(x_ref, nw_ref, sh_ref, sc_ref, o_ref):
    o_ref[...] = _rms_mod(x_ref[...], nw_ref[...], sh_ref[...], sc_ref[...]).astype(o_ref.dtype)


def _norm_mod(x, nw, sh, sc):
    m, d = x.shape
    tm = min(m, 512)
    vec = pl.BlockSpec((1, d), lambda i: (0, 0))
    return pl.pallas_call(
        _norm_kernel,
        grid=(m // tm,),
        in_specs=[pl.BlockSpec((tm, d), lambda i: (i, 0)), vec, vec, vec],
        out_specs=pl.BlockSpec((tm, d), lambda i: (i, 0)),
        out_shape=jax.ShapeDtypeStruct((m, d), BF16),
        compiler_params=_params(1, 6 * tm * d * 4),
        name="norm_mod",
    )(x, nw, sh, sc)


def _halo_kernel(xp_ref, xn_ref, nw_ref, sh_ref, sc_ref, o_ref):
    i = pl.program_id(0)
    n = pl.num_programs(0)
    hp = _rms_mod(xp_ref[...], nw_ref[...], sh_ref[...], sc_ref[...])
    hn = _rms_mod(xn_ref[...], nw_ref[...], sh_ref[...], sc_ref[...])
    hp = jnp.where(i > 0, hp, 0.0)
    hn = jnp.where(i < n - 1, hn, 0.0)
    o_ref[0] = jnp.concatenate([hn, hp], axis=0).astype(o_ref.dtype)


def _halo_norm(x, nw, sh, sc, tm):
    m, d = x.shape
    nblk = m // tm
    r = SUBLANES_F32
    per = tm // r
    last = m // r - 1
    vec = pl.BlockSpec((1, d), lambda i: (0, 0))
    return pl.pallas_call(
        _halo_kernel,
        grid=(nblk,),
        in_specs=[
            pl.BlockSpec((r, d), lambda i: (jnp.maximum(i * per - 1, 0), 0)),
            pl.BlockSpec((r, d), lambda i: (jnp.minimum((i + 1) * per, last), 0)),
            vec, vec, vec,
        ],
        out_specs=pl.BlockSpec((1, 2 * r, d), lambda i: (i, 0, 0)),
        out_shape=jax.ShapeDtypeStruct((nblk, 2 * r, d), BF16),
        compiler_params=_params(1, 16 * 1024 * 1024),
        name="halo_norm",
    )(x, x, nw, sh, sc)


def _inproj_kernel(a_ref, b_ref, nw_ref, o_ref, acc0, acc1, *, j_off, qk_lo, qk_hi):
    tm = a_ref.shape[0]
    j = pl.program_id(1)
    nj = pl.num_programs(1) - 1
    tile = j - 1 + j_off
    is_qk = jnp.logical_and(tile >= qk_lo, tile < qk_hi)
    rc = min(tm, EPILOGUE_ROWS)

    def matmul(acc):
        acc[...] = _dot(a_ref[...], b_ref[...].astype(BF16))

    def epilogue(acc):
        for r0 in range(0, tm, rc):
            for h in range(o_ref.shape[1] // HEAD_DIM):
                sl = slice(h * HEAD_DIM, (h + 1) * HEAD_DIM)
                xh = acc[r0:r0 + rc, sl]
                ms = jnp.mean(xh * xh, axis=-1, keepdims=True)
                normed = xh * lax.rsqrt(ms + EPS) * nw_ref[0, :, sl]
                o_ref[r0:r0 + rc, sl] = jnp.where(is_qk, normed, xh).astype(o_ref.dtype)

    odd = lax.rem(j, 2) == 1
    mid = jnp.logical_and(j > 0, j < nj)

    @pl.when(j == 0)
    def _():
        matmul(acc0)

    @pl.when(jnp.logical_and(mid, odd))
    def _():
        epilogue(acc0)
        matmul(acc1)

    @pl.when(jnp.logical_and(mid, jnp.logical_not(odd)))
    def _():
        epilogue(acc1)
        matmul(acc0)

    @pl.when(jnp.logical_and(j == nj, odd))
    def _():
        epilogue(acc0)

    @pl.when(jnp.logical_and(j == nj, jnp.logical_not(odd)))
    def _():
        epilogue(acc1)


def _in_proj(h, w_in, layer, nw_tiles, tn, j_off, n_tiles, qk_lo, qk_hi):
    m, d = h.shape
    tm = min(m, 1024)
    acc = pltpu.VMEM((tm, tn), F32)
    return pl.pallas_call(
        functools.partial(_inproj_kernel, j_off=j_off, qk_lo=qk_lo, qk_hi=qk_hi),
        grid=(m // tm, n_tiles + 1),
        in_specs=[
            pl.BlockSpec((tm, d), lambda i, j: (i, 0)),
            pl.BlockSpec((None, d, tn), lambda i, j: (layer, 0, jnp.minimum(j, n_tiles - 1) + j_off)),
            pl.BlockSpec((1, 1, tn), lambda i, j: (jnp.maximum(j - 1, 0) + j_off, 0, 0)),
        ],
        out_specs=pl.BlockSpec((tm, tn), lambda i, j: (i, jnp.maximum(j - 1, 0))),
        out_shape=jax.ShapeDtypeStruct((m, n_tiles * tn), BF16),
        scratch_shapes=[acc, acc],
        compiler_params=_params(2, 4 * tm * d + 10 * d * tn + 4 * tm * tn + 6 * tm * tn * 4),
        name="in_proj",
    )(h, w_in, nw_tiles)


def _nat_kernel(q_ref, k_ref, v_ref, kc_ref, vc_ref, bias_ref, o_ref, *, grid_rows, scale, heads):
    b = pl.program_id(1)
    ks = jnp.clip(b * ATTN_ROWS - WIN_ROWS // 2, 0, grid_rows - ATTN_KEY_ROWS)
    start = pl.multiple_of(ks * GRID_W, GRID_W)
    nkeys = ATTN_KEY_ROWS * GRID_W
    for h in range(heads):
        sl = slice(h * HEAD_DIM, (h + 1) * HEAD_DIM)
        q = q_ref[:, sl]
        kw = k_ref[pl.ds(start, nkeys), sl]
        vw = v_ref[pl.ds(start, nkeys), sl]
        s = _dot_nt(q, kw) * scale + bias_ref[0, h]
        sc = _dot_nt(q, kc_ref[:, sl]) * scale
        m = jnp.maximum(jnp.max(s, axis=-1, keepdims=True), jnp.max(sc, axis=-1, keepdims=True))
        p = jnp.exp(s - m)
        pc = jnp.exp(sc - m)
        l = jnp.sum(p, axis=-1, keepdims=True) + jnp.sum(pc, axis=-1, keepdims=True)
        o = _dot(p.astype(BF16), vw) + _dot(pc.astype(BF16), vc_ref[:, sl])
        o_ref[:, sl] = (o / l).astype(o_ref.dtype)


def _nat_bias(rpb, grid_rows):
    n_heads = rpb.shape[0]
    nb = grid_rows // ATTN_ROWS
    c = np.arange(GRID_W)[:, None]
    kc = np.arange(GRID_W)[None, :]
    c_start = np.clip(c - WIN_COLS // 2, 0, GRID_W - WIN_COLS)
    col_ok = (kc >= c_start) & (kc < c_start + WIN_COLS)
    dc = kc - c + (WIN_COLS - 1)
    n_dc = 2 * WIN_COLS - 1
    expand = ((dc[None] == np.arange(n_dc)[:, None, None]) & col_ok[None]).astype(np.float32)
    blocks = jnp.einsum("hrd,dn->hrn", rpb, expand.reshape(n_dc, GRID_W * GRID_W),
                        precision=lax.Precision.HIGHEST)
    blocks = blocks.reshape(n_heads, 2 * WIN_ROWS - 1, GRID_W, GRID_W)
    blocks = jnp.where(col_ok, blocks, NEG_BIAS)
    masked = jnp.full((n_heads, GRID_W, GRID_W), NEG_BIAS, F32)
    variants = []
    for b in (0, 1, nb - 1):
        ks = int(np.clip(b * ATTN_ROWS - WIN_ROWS // 2, 0, grid_rows - ATTN_KEY_ROWS))
        q_rows = []
        for r_loc in range(ATTN_ROWS):
            r = b * ATTN_ROWS + r_loc
            r_start = int(np.clip(r - WIN_ROWS // 2, 0, grid_rows - WIN_ROWS))
            row = []
            for kr in range(ks, ks + ATTN_KEY_ROWS):
                ok = r_start <= kr < r_start + WIN_ROWS
                row.append(blocks[:, kr - r + (WIN_ROWS - 1)] if ok else masked)
            q_rows.append(jnp.concatenate(row, axis=-1))
        variants.append(jnp.concatenate(q_rows, axis=-2))
    return jnp.stack(variants)


def _nat_attention(p_lat, p_ctx, bias, n_heads, q_blk, k_blk, v_blk, kc_blk, vc_blk, hs):
    s_len = p_lat.shape[0]
    n_ctx = p_ctx.shape[0]
    grid_rows = s_len // GRID_W
    nb = grid_rows // ATTN_ROWS
    nq = ATTN_ROWS * GRID_W
    nk = ATTN_KEY_ROWS * GRID_W
    assert grid_rows % ATTN_ROWS == 0 and nb >= 3 and grid_rows >= ATTN_KEY_ROWS
    while n_heads % hs or any(blk % hs for blk in (q_blk, k_blk, v_blk, kc_blk, vc_blk)):
        hs //= 2
    width = hs * HEAD_DIM

    def variant(b):
        return jnp.where(b == 0, 0, jnp.where(b == nb - 1, 2, 1))

    def cols(rows, blk):
        mode = pl.Buffered(1 if (rows == s_len and hs > 2) else 2)
        return pl.BlockSpec((rows, width), lambda h, b: (0, blk // hs + h), pipeline_mode=mode)

    return pl.pallas_call(
        functools.partial(_nat_kernel, grid_rows=grid_rows, scale=HEAD_DIM ** -0.5, heads=hs),
        grid=(n_heads // hs, nb),
        in_specs=[
            pl.BlockSpec((nq, width), lambda h, b: (b, q_blk // hs + h)),
            cols(s_len, k_blk), cols(s_len, v_blk), cols(n_ctx, kc_blk), cols(n_ctx, vc_blk),
            pl.BlockSpec((1, hs, nq, nk), lambda h, b: (variant(b), h, 0, 0)),
        ],
        out_specs=pl.BlockSpec((nq, width), lambda h, b: (b, h)),
        out_shape=jax.ShapeDtypeStruct((s_len, n_heads * HEAD_DIM), BF16),
        compiler_params=_params(2, 8 * s_len * width + hs * (2 * nq * nk * 4 + 16 * nq * (nk + n_ctx) * 4)),
        name="nat_attention",
    )(p_lat, p_lat, p_lat, p_ctx, p_ctx, bias)


def _ctx_attn_kernel(q_ref, k_ref, v_ref, o_ref, *, scale):
    s = _dot_nt(q_ref[...], k_ref[...]) * scale
    m = jnp.max(s, axis=-1, keepdims=True)
    p = jnp.exp(s - m)
    l = jnp.sum(p, axis=-1, keepdims=True)
    o_ref[...] = (_dot(p.astype(BF16), v_ref[...]) / l).astype(o_ref.dtype)


def _ctx_attention(p_ctx, n_heads, q_blk, k_blk, v_blk):
    n = p_ctx.shape[0]
    return pl.pallas_call(
        functools.partial(_ctx_attn_kernel, scale=HEAD_DIM ** -0.5),
        grid=(n_heads,),
        in_specs=[
            pl.BlockSpec((n, HEAD_DIM), lambda h: (0, q_blk + h)),
            pl.BlockSpec((n, HEAD_DIM), lambda h: (0, k_blk + h)),
            pl.BlockSpec((n, HEAD_DIM), lambda h: (0, v_blk + h)),
        ],
        out_specs=pl.BlockSpec((n, HEAD_DIM), lambda h: (0, h)),
        out_shape=jax.ShapeDtypeStruct((n, n_heads * HEAD_DIM), BF16),
        compiler_params=_params(1, 16 * 1024 * 1024),
        name="ctx_attention",
    )(p_ctx, p_ctx, p_ctx)


def _pool_kernel(up_ref, um_ref, un_ref, pw_ref, ps_ref, o_ref, *, seq_len, gc):
    i = pl.program_id(0)
    n = pl.num_programs(0)
    tp = um_ref.shape[0]
    halo = up_ref.shape[0]
    um = um_ref[...]
    up = jnp.where(i > 0, up_ref[...], jnp.zeros_like(up_ref[...]))
    un = jnp.where(i < n - 1, un_ref[...], jnp.zeros_like(un_ref[...]))
    ue = jnp.concatenate([up, um, un], axis=0)
    trow = lax.broadcasted_iota(jnp.int32, (tp, tp + 2 * halo), 0)
    jcol = lax.broadcasted_iota(jnp.int32, (tp, tp + 2 * halo), 1)
    off = jcol - trow - halo
    t_abs = i * tp + lax.broadcasted_iota(jnp.int32, (tp, gc), 0)
    for g, w in enumerate(POOL_WINDOWS):
        lo_off = -(w // 2)
        hi_off = w - 1 - w // 2
        band = jnp.where(off >= lo_off, jnp.where(off <= hi_off, 1.0, 0.0), 0.0).astype(BF16)
        sl = slice(g * gc, (g + 1) * gc)
        wsum = _dot(band, ue[:, sl])
        lo = jnp.maximum(t_abs + lo_off, 0)
        hi = jnp.minimum(t_abs + hi_off, seq_len - 1)
        cnt = (hi - lo + 1).astype(F32)
        d = (wsum / cnt - um[:, sl].astype(F32)).astype(BF16)
        o_ref[:, sl] = (_dot(d, pw_ref[g]) * ps_ref[:, sl]).astype(o_ref.dtype)


def _pool_mix(p, col_blk, pool_w, pool_scale):
    m = p.shape[0]
    gc = pool_w.shape[-1]
    width = N_GROUPS * gc
    tp = min(m, 512)
    halo = SUBLANES_BF16
    per = tp // halo
    last = m // halo - 1
    return pl.pallas_call(
        functools.partial(_pool_kernel, seq_len=m, gc=gc),
        grid=(m // tp,),
        in_specs=[
            pl.BlockSpec((halo, width), lambda i: (jnp.maximum(i * per - 1, 0), col_blk)),
            pl.BlockSpec((tp, width), lambda i: (i, col_blk)),
            pl.BlockSpec((halo, width), lambda i: (jnp.minimum((i + 1) * per, last), col_blk)),
            pl.BlockSpec((N_GROUPS, gc, gc), lambda i: (0, 0, 0)),
            pl.BlockSpec((1, width), lambda i: (0, 0)),
        ],
        out_specs=pl.BlockSpec((tp, width), lambda i: (i, 0)),
        out_shape=jax.ShapeDtypeStruct((m, width), BF16),
        compiler_params=_params(1, 32 * 1024 * 1024),
        name="pool_mix",
    )(p, p, p, pool_w, pool_scale)


def _cos_sin(num, den):
    ang = (num % den).astype(F32) * (2.0 * math.pi / den)
    return jnp.cos(ang), jnp.sin(ang)


def _dft_tables(n):
    idx = jnp.arange(n, dtype=jnp.int32)
    return _cos_sin(idx[:, None] * idx[None, :], n)


def _fnet_chan_kernel(u_ref, cs_ref, o_ref, *, gc):
    for g in range(N_GROUPS):
        sl = slice(g * gc, (g + 1) * gc)
        o_ref[:, sl] = _dot(u_ref[:, sl], cs_ref[0]).astype(o_ref.dtype)


def _fnet_pos_kernel(t_ref, ab_ref, fw_ref, o_ref, *, scale, gc):
    acc = _dot(t_ref[...], ab_ref[...])
    for g in range(N_GROUPS):
        sl = slice(g * gc, (g + 1) * gc)
        f = (acc[:, sl] * scale).astype(BF16)
        o_ref[:, sl] = _dot(f, fw_ref[g]).astype(o_ref.dtype)


def _fourier_mix_dense(p, col_blk, cs_chan, fnet_w):
    m = p.shape[0]
    gc = fnet_w.shape[-1]
    width = N_GROUPS * gc
    cpos, spos = _dft_tables(m)
    t_pos = jnp.concatenate([cpos, -spos], axis=1).astype(BF16)
    ab = pl.pallas_call(
        functools.partial(_fnet_chan_kernel, gc=gc),
        grid=(2,),
        in_specs=[
            pl.BlockSpec((m, width), lambda c: (0, col_blk)),
            pl.BlockSpec((1, gc, gc), lambda c: (c, 0, 0)),
        ],
        out_specs=pl.BlockSpec((m, width), lambda c: (c, 0)),
        out_shape=jax.ShapeDtypeStruct((2 * m, width), BF16),
        compiler_params=_params(1, 32 * 1024 * 1024),
        name="fnet_chan",
    )(p, cs_chan)
    return pl.pallas_call(
        functools.partial(_fnet_pos_kernel, scale=1.0 / math.sqrt(m * gc), gc=gc),
        grid=(1,),
        in_specs=[
            pl.BlockSpec((m, 2 * m), lambda i: (0, 0)),
            pl.BlockSpec((2 * m, width), lambda i: (0, 0)),
            pl.BlockSpec((N_GROUPS, gc, gc), lambda i: (0, 0, 0)),
        ],
        out_specs=pl.BlockSpec((m, width), lambda i: (0, 0)),
        out_shape=jax.ShapeDtypeStruct((m, width), BF16),
        compiler_params=_params(1, 32 * 1024 * 1024),
        name="fnet_pos",
    )(t_pos, ab, fnet_w)


def _fft_stage1_kernel(f1_ref, u_ref, tc_ref, ts_ref, zr_ref, zi_ref, *, n1):
    width = zr_ref.shape[1]
    reps = width // LANES
    for t in range(u_ref.shape[1] // width):
        y = _dot(f1_ref[...], u_ref[:, t * width:(t + 1) * width])
        yr, yi = y[:n1], y[n1:]
        tc = jnp.tile(tc_ref[:, t * LANES:(t + 1) * LANES], (1, reps))
        ts = jnp.tile(ts_ref[:, t * LANES:(t + 1) * LANES], (1, reps))
        zr_ref[t * n1:(t + 1) * n1, :] = (yr * tc + yi * ts).astype(zr_ref.dtype)
        zi_ref[t * n1:(t + 1) * n1, :] = (yi * tc - yr * ts).astype(zi_ref.dtype)


def _fft_stage2_kernel(fa_ref, fb_ref, zr_ref, zi_ref, cc_ref, sc_ref, fw_ref, o_ref, *, n2, gc, chunks, scale):
    width = N_GROUPS * gc
    p = _dot(fa_ref[...], zr_ref[...]) + _dot(fb_ref[...], zi_ref[...])
    for g in range(N_GROUPS):
        cols = [slice(ch * width + g * gc, ch * width + (g + 1) * gc) for ch in range(chunks)]
        pr = jnp.concatenate([p[:n2, c] for c in cols], axis=0).astype(BF16)
        pi = jnp.concatenate([p[n2:, c] for c in cols], axis=0).astype(BF16)
        f = ((_dot(pr, cc_ref[...]) + _dot(pi, sc_ref[...])) * scale).astype(BF16)
        y = _dot(f, fw_ref[g]).astype(o_ref.dtype)
        for ch, c in enumerate(cols):
            o_ref[:, c] = y[ch * n2:(ch + 1) * n2]


def _fourier_mix_fft(p, col_blk, cs_chan, fnet_w):
    m, d_in = p.shape
    gc = fnet_w.shape[-1]
    width = N_GROUPS * gc
    n2 = LANES
    n1 = m // n2
    i1 = jnp.arange(n1, dtype=jnp.int32)
    i2 = jnp.arange(n2, dtype=jnp.int32)
    c1, s1 = _cos_sin(i1[:, None] * i1[None, :], n1)
    f1 = jnp.concatenate([c1, -s1], axis=0).astype(BF16)
    tc, ts = _cos_sin(i1[:, None] * i2[None, :], m)
    tc = jnp.repeat(tc, LANES, axis=1)
    ts = jnp.repeat(ts, LANES, axis=1)
    c2, s2 = _cos_sin(i2[:, None] * i2[None, :], n2)
    fa = jnp.concatenate([c2, -s2], axis=0).astype(BF16)
    fb = jnp.concatenate([s2, c2], axis=0).astype(BF16)
    u = p[:, col_blk * width:(col_blk + 1) * width].reshape(n1, n2 * width)
    z_shape = jax.ShapeDtypeStruct((n2 * n1, width), BF16)
    sb = FFT_STAGE1_COLS
    zr, zi = pl.pallas_call(
        functools.partial(_fft_stage1_kernel, n1=n1),
        grid=(n2 // sb,),
        in_specs=[
            pl.BlockSpec((2 * n1, n1), lambda j: (0, 0)),
            pl.BlockSpec((n1, sb * width), lambda j: (0, j)),
            pl.BlockSpec((n1, sb * LANES), lambda j: (0, j)),
            pl.BlockSpec((n1, sb * LANES), lambda j: (0, j)),
        ],
        out_specs=[pl.BlockSpec((sb * n1, width), lambda j: (j, 0))] * 2,
        out_shape=[z_shape, z_shape],
        compiler_params=_params(1, 16 * 1024 * 1024),
        name="fft_stage1",
    )(f1, u, tc, ts)
    chunks = min(n1, 8)
    tn = chunks * width
    y = pl.pallas_call(
        functools.partial(_fft_stage2_kernel, n2=n2, gc=gc, chunks=chunks, scale=1.0 / math.sqrt(m * gc)),
        grid=(n1 // chunks,),
        in_specs=[
            pl.BlockSpec((2 * n2, n2), lambda j: (0, 0)),
            pl.BlockSpec((2 * n2, n2), lambda j: (0, 0)),
            pl.BlockSpec((n2, tn), lambda j: (0, j)),
            pl.BlockSpec((n2, tn), lambda j: (0, j)),
            pl.BlockSpec((None, gc, gc), lambda j: (0, 0, 0)),
            pl.BlockSpec((None, gc, gc), lambda j: (1, 0, 0)),
            pl.BlockSpec((N_GROUPS, gc, gc), lambda j: (0, 0, 0)),
        ],
        out_specs=pl.BlockSpec((n2, tn), lambda j: (0, j)),
        out_shape=jax.ShapeDtypeStruct((n2, n1 * width), BF16),
        compiler_params=_params(1, 40 * 1024 * 1024),
        name="fft_stage2",
    )(fa, fb, zr.reshape(n2, n1 * width), zi.reshape(n2, n1 * width), cs_chan, cs_chan, fnet_w)
    return y.reshape(m, width)


def _fourier_mix(p, col_blk, cs_chan, fnet_w):
    if p.shape[0] >= FFT_MIN_LEN and p.shape[0] % (LANES * SUBLANES_F32) == 0:
        return _fourier_mix_fft(p, col_blk, cs_chan, fnet_w)
    return _fourier_mix_dense(p, col_blk, cs_chan, fnet_w)


def _merge_kernel(h_ref, yp_ref, ya_ref, yf_ref, wg0, wg1, wg2, bg0, bg1, bg2, wp, wa, wf, o_ref):
    h = h_ref[...]

    def gate(w_ref, b_ref):
        return jax.nn.sigmoid(_dot(h, w_ref[...]) + b_ref[...])

    m = gate(wg0, bg0) * _dot(yp_ref[...], wp[...])
    m = m + gate(wg1, bg1) * _dot(ya_ref[...], wa[...])
    m = m + gate(wg2, bg2) * _dot(yf_ref[...], wf[...])
    o_ref[...] = m.astype(o_ref.dtype)


def _merge(h, y_pool, y_attn, y_fnet, layer, w_gate, b_gate, w_bp, w_ba, w_bf):
    m, d = h.shape
    dp, da, df = y_pool.shape[1], y_attn.shape[1], y_fnet.shape[1]
    tm = min(m, 1024)
    tn = _tile(d, 256)
    nj = d // tn

    def act(width):
        bufs = 1 if width == d else 2
        return pl.BlockSpec((tm, width), lambda i, j: (i, 0), pipeline_mode=pl.Buffered(bufs))

    def gate_w(b):
        return pl.BlockSpec((None, d, tn), lambda i, j: (layer, 0, b * nj + j))

    def gate_b(b):
        return pl.BlockSpec((None, 1, tn), lambda i, j: (layer, 0, b * nj + j))

    def br_w(k):
        return pl.BlockSpec((None, k, tn), lambda i, j: (layer, 0, j))

    vmem = 4 * tm * (d + dp + da + df) + 4 * tn * (3 * d + dp + da + df) + 12 * tm * tn * 4
    return pl.pallas_call(
        _merge_kernel,
        grid=(m // tm, nj),
        in_specs=[act(d), act(dp), act(da), act(df), gate_w(0), gate_w(1), gate_w(2),
                  gate_b(0), gate_b(1), gate_b(2), br_w(dp), br_w(da), br_w(df)],
        out_specs=pl.BlockSpec((tm, tn), lambda i, j: (i, j)),
        out_shape=jax.ShapeDtypeStruct((m, d), BF16),
        compiler_params=_params(2, vmem),
        name="merge",
    )(h, y_pool, y_attn, y_fnet, w_gate, w_gate, w_gate, b_gate, b_gate, b_gate, w_bp, w_ba, w_bf)


def _resid_kernel(a_ref, w_ref, x_ref, g_ref, o_ref):
    o_ref[...] = x_ref[...] + g_ref[...] * _dot(a_ref[...], w_ref[...])


def _resid_proj(a, w, layer, x, g, tm_pref):
    m, kk = a.shape
    n = w.shape[2]
    tm = min(m, tm_pref)
    tn = _tile(n, 512)
    return pl.pallas_call(
        _resid_kernel,
        grid=(m // tm, n // tn),
        in_specs=[
            pl.BlockSpec((tm, kk), lambda i, j: (i, 0)),
            pl.BlockSpec((None, kk, tn), lambda i, j: (layer, 0, j)),
            pl.BlockSpec((tm, tn), lambda i, j: (i, j)),
            pl.BlockSpec((1, tn), lambda i, j: (0, j)),
        ],
        out_specs=pl.BlockSpec((tm, tn), lambda i, j: (i, j)),
        out_shape=jax.ShapeDtypeStruct((m, n), F32),
        compiler_params=_params(2, 4 * tm * kk + 4 * kk * tn + 7 * tm * tn * 4),
        name="resid_proj",
    )(a, w, x, g)


def _ffn_up_kernel(a_ref, halo_ref, wg_ref, wv_ref, cwg_ref, cwv_ref, cbg_ref, cbv_ref, o_ref, a_ext):
    tm = a_ref.shape[0]

    @pl.when(pl.program_id(1) == 0)
    def _():
        a_ext[:tm, :] = a_ref[...]
        a_ext[tm:, :] = halo_ref[0]

    a = a_ext[...]
    row = lax.broadcasted_iota(jnp.int32, (tm, wg_ref.shape[1]), 0)

    def conv(w_ref, cw_ref, cb_ref):
        u = _dot(a, w_ref[...].astype(BF16))
        um = u[:tm]
        next_row = u[tm:tm + 1]
        prev_row = u[tm + 2 * SUBLANES_F32 - 1:tm + 2 * SUBLANES_F32]
        u_dn = jnp.where(row == 0, prev_row, pltpu.roll(um, 1, 0))
        u_up = jnp.where(row == tm - 1, next_row, pltpu.roll(um, tm - 1, 0))
        return u_dn * cw_ref[0:1] + um * cw_ref[1:2] + u_up * cw_ref[2:3] + cb_ref[...]

    gate = conv(wg_ref, cwg_ref, cbg_ref)
    val = conv(wv_ref, cwv_ref, cbv_ref)
    o_ref[...] = (gate * jax.nn.sigmoid(gate) * val).astype(o_ref.dtype)


def _ffn_up(h2, halo, layer, w_up, conv_w, conv_b):
    m, d = h2.shape
    f = w_up.shape[2] // 2
    tm = m // halo.shape[0]
    ext = halo.shape[1]
    tn = _tile(f, 256)
    nj = f // tn
    w_bytes = jnp.dtype(w_up.dtype).itemsize
    a_bufs = 1 if w_bytes == 4 else 2
    vmem = (2 * a_bufs * tm * d + 2 * (tm + ext) * d + (4 * w_bytes + 4) * d * tn + 4 * tm * tn
            + 12 * (tm + ext) * tn * 4)

    def cols(rows, half):
        return pl.BlockSpec((None, rows, tn), lambda i, j: (layer, 0, half * nj + j))

    return pl.pallas_call(
        _ffn_up_kernel,
        grid=(m // tm, nj),
        in_specs=[
            pl.BlockSpec((tm, d), lambda i, j: (i, 0), pipeline_mode=pl.Buffered(a_bufs)),
            pl.BlockSpec((1, ext, d), lambda i, j: (i, 0, 0)),
            cols(d, 0), cols(d, 1), cols(CONV_W, 0), cols(CONV_W, 1), cols(1, 0), cols(1, 1),
        ],
        out_specs=pl.BlockSpec((tm, tn), lambda i, j: (i, j)),
        out_shape=jax.ShapeDtypeStruct((m, f), BF16),
        scratch_shapes=[pltpu.VMEM((tm + ext, d), BF16)],
        compiler_params=_params(2, vmem),
        name="ffn_up",
    )(h2, halo, w_up, w_up, conv_w, conv_w, conv_b, conv_b)


def kernel(x, c, ctx, c_ctx, w_ada, b_ada, norm1_w, norm2_w, w_in, pool_w, pool_scale, q_norm_w, k_norm_w, rpb, fnet_w, w_gate, b_gate, w_br_pool, w_br_attn, w_br_fnet, w_o, w_up, conv_w, conv_b, w_down):
    n_layers, d, d_in = w_in.shape
    batch, s_len, _ = x.shape
    n_ctx = ctx.shape[1]
    assert batch == 1
    d_pool = w_br_pool.shape[1]
    d_attn = w_br_attn.shape[1]
    d_fnet = w_br_fnet.shape[1]
    n_heads = d_attn // HEAD_DIM
    assert d_in == d_pool + 3 * d_attn + d_fnet and s_len % GRID_W == 0 and d_fnet == d_pool
    q_lo, k_lo, v_lo, f_lo = d_pool, d_pool + d_attn, d_pool + 2 * d_attn, d_pool + 3 * d_attn

    w_gate_b = w_gate.astype(BF16)
    w_bp_b = w_br_pool.astype(BF16)
    w_ba_b = w_br_attn.astype(BF16)
    w_bf_b = w_br_fnet.astype(BF16)
    w_o_b = w_o.astype(BF16)
    w_down_b = w_down.astype(BF16)
    pool_w_b = pool_w.astype(BF16)
    fnet_w_b = fnet_w.astype(BF16)
    b_gate_r = b_gate.reshape(n_layers, 1, -1)
    conv_b_r = conv_b.reshape(n_layers, 1, -1)
    ones_p = jnp.ones((d_pool,), F32)
    ones_vf = jnp.ones((d_attn + d_fnet,), F32)

    cs_chan = jnp.stack(_dft_tables(fnet_w.shape[-1])).astype(BF16)

    cond = jnp.concatenate([c.reshape(1, d), c_ctx.reshape(1, d), jnp.zeros((SUBLANES_F32 - 2, d), F32)], axis=0)
    mod = _ada(cond, w_ada, b_ada)

    tm_ffn = min(s_len, 1024)
    blk = lambda col: col // HEAD_DIM
    tn_in = _tile(d_pool, 512)
    assert d_pool % tn_in == 0 and d_attn % tn_in == 0
    qk_tiles = (q_lo // tn_in, v_lo // tn_in)

    xl = x.reshape(s_len, d)
    xc = ctx.reshape(n_ctx, d)
    for l in range(n_layers):
        last = l == n_layers - 1
        sh1, sc1, g1, sh2, sc2, g2 = [mod[l, 0:1, i * d:(i + 1) * d] for i in range(N_MOD)]
        csh1, csc1, cg1, csh2, csc2, cg2 = [mod[l, 1:2, i * d:(i + 1) * d] for i in range(N_MOD)]
        n1w = norm1_w[l].reshape(1, d)
        n2w = norm2_w[l].reshape(1, d)
        nw_tiles = jnp.concatenate(
            [ones_p, jnp.tile(q_norm_w[l], n_heads), jnp.tile(k_norm_w[l], n_heads), ones_vf]
        ).reshape(d_in // tn_in, 1, tn_in)
        bias = _nat_bias(rpb[l], s_len // GRID_W)
        pool_scale_l = pool_scale[l].reshape(1, -1)

        h = _norm_mod(xl, n1w, sh1, sc1)
        hc = _norm_mod(xc, n1w, csh1, csc1)
        p = _in_proj(h, w_in, l, nw_tiles, tn_in, 0, d_in // tn_in, *qk_tiles)
        if last:
            pc = _in_proj(hc, w_in, l, nw_tiles, tn_in, k_lo // tn_in, (f_lo - k_lo) // tn_in, *qk_tiles)
            kc_blk, vc_blk = 0, blk(d_attn)
        else:
            pc = _in_proj(hc, w_in, l, nw_tiles, tn_in, 0, d_in // tn_in, *qk_tiles)
            kc_blk, vc_blk = blk(k_lo), blk(v_lo)
        y_attn = _nat_attention(p, pc, bias, n_heads, blk(q_lo), blk(k_lo), blk(v_lo), kc_blk, vc_blk, ATTN_HEADS)
        y_pool = _pool_mix(p, 0, pool_w_b[l], pool_scale_l)
        y_fnet = _fourier_mix(p, f_lo // d_fnet, cs_chan, fnet_w_b[l])
        m = _merge(h, y_pool, y_attn, y_fnet, l, w_gate_b, b_gate_r, w_bp_b, w_ba_b, w_bf_b)
        xl = _resid_proj(m, w_o_b, l, xl, g1, 1024)

        h2 = _norm_mod(xl, n2w, sh2, sc2)
        halo = _halo_norm(xl, n2w, sh2, sc2, tm_ffn)
        t = _ffn_up(h2, halo, l, w_up, conv_w, conv_b_r)
        xl = _resid_proj(t, w_down_b, l, xl, g2, 512)

        if not last:
            yc_attn = _ctx_attention(pc, n_heads, blk(q_lo), blk(k_lo), blk(v_lo))
            yc_pool = _pool_mix(pc, 0, pool_w_b[l], pool_scale_l)
            yc_fnet = _fourier_mix(pc, f_lo // d_fnet, cs_chan, fnet_w_b[l])
            mc = _merge(hc, yc_pool, yc_attn, yc_fnet, l, w_gate_b, b_gate_r, w_bp_b, w_ba_b, w_bf_b)
            xc = _resid_proj(mc, w_o_b, l, xc, cg1, 1024)
            hc2 = _norm_mod(xc, n2w, csh2, csc2)
            halo_c = _halo_norm(xc, n2w, csh2, csc2, n_ctx)
            tc = _ffn_up(hc2, halo_c, l, w_up, conv_w, conv_b_r)
            xc = _resid_proj(tc, w_down_b, l, xc, cg2, 512)
    return xl.reshape(batch, s_len, d)
```

```python
import functools
import math

import numpy as np
import jax
import jax.numpy as jnp
from jax import lax
from jax.experimental import pallas as pl
from jax.experimental.pallas import tpu as pltpu

F32 = jnp.float32
BF16 = jnp.bfloat16

GRID_W = 64
HEAD_DIM = 128
WIN_ROWS = 8
WIN_COLS = 16
POOL_WINDOWS = (2, 4, 8, 16)
N_GROUPS = 4
N_MOD = 6
CONV_W = 3
EPS = 1e-6

LANES = 128
SUBLANES_F32 = 8
SUBLANES_BF16 = 16
VMEM_CAP_BYTES = 56 * 1024 * 1024

ATTN_ROWS = 4
ATTN_KEY_ROWS = ATTN_ROWS + WIN_ROWS
ATTN_HEADS = 8
FFT_STAGE1_COLS = 8
EPILOGUE_ROWS = 256
NEG_BIAS = -1e30
FFT_MIN_LEN = 1024


def _tile(n, pref):
    if n <= pref:
        return n
    t = pref - pref % LANES
    while t >= LANES:
        if n % t == 0:
            return t
        t -= LANES
    return n


def _params(n_axes, vmem_bytes):
    return pltpu.CompilerParams(
        dimension_semantics=("arbitrary",) * n_axes,
        vmem_limit_bytes=int(min(VMEM_CAP_BYTES, max(vmem_bytes, 16 * 1024 * 1024))),
    )


def _dot(a, b):
    return jnp.dot(a, b, preferred_element_type=F32)


def _dot_nt(a, b):
    return lax.dot_general(a, b, (((1,), (1,)), ((), ())), preferred_element_type=F32)


def _rms_mod(x, nw, sh, sc):
    ms = jnp.mean(x * x, axis=-1, keepdims=True)
    return (x * lax.rsqrt(ms + EPS) * nw) * (1.0 + sc) + sh


def _ada_kernel(s_ref, w_ref, b_ref, o_ref):
    s = s_ref[...]
    a = (s * jax.nn.sigmoid(s)).astype(BF16)
    o_ref[0] = _dot(a, w_ref[0].astype(BF16)) + b_ref[0]


def _ada(cond, w_ada, b_ada):
    n_layers, d, n = w_ada.shape
    tn = _tile(n, 1024)
    return pl.pallas_call(
        _ada_kernel,
        grid=(n_layers, n // tn),
        in_specs=[
            pl.BlockSpec((SUBLANES_F32, d), lambda l, j: (0, 0)),
            pl.BlockSpec((1, d, tn), lambda l, j: (l, 0, j)),
            pl.BlockSpec((1, 1, tn), lambda l, j: (l, 0, j)),
        ],
        out_specs=pl.BlockSpec((1, SUBLANES_F32, tn), lambda l, j: (l, 0, j)),
        out_shape=jax.ShapeDtypeStruct((n_layers, SUBLANES_F32, n), F32),
        compiler_params=_params(2, 2 * d * tn * 4 + 8 * 1024 * 1024),
        name="ada",
    )(cond, w_ada, b_ada.reshape(n_layers, 1, n))


def _norm_kernel(x_ref, nw_ref, sh_ref, sc_ref, o_ref):
    o_ref[...] = _rms_mod(x_ref[...], nw_ref[...], sh_ref[...], sc_ref[...]).astype(o_ref.dtype)


def _norm_mod(x, nw, sh, sc):
    m, d = x.shape
    tm = min(m, 512)
    vec = pl.BlockSpec((1, d), lambda i: (0, 0))
    return pl.pallas_call(
        _norm_kernel,
        grid=(m // tm,),
        in_specs=[pl.BlockSpec((tm, d), lambda i: (i, 0)), vec, vec, vec],
        out_specs=pl.BlockSpec((tm, d), lambda i: (i, 0)),
        out_shape=jax.ShapeDtypeStruct((m, d), BF16),
        compiler_params=_params(1, 6 * tm * d * 4),
        name="norm_mod",
    )(x, nw, sh, sc)


def _halo_kernel(xp_ref, xn_ref, nw_ref, sh_ref, sc_ref, o_ref):
    i = pl.program_id(0)
    n = pl.num_programs(0)
    hp = _rms_mod(xp_ref[...], nw_ref[...], sh_ref[...], sc_ref[...])
    hn = _rms_mod(xn_ref[...], nw_ref[...], sh_ref[...], sc_ref[...])
    hp = jnp.where(i > 0, hp, 0.0)
    hn = jnp.where(i < n - 1, hn, 0.0)
    o_ref[0] = jnp.concatenate([hn, hp], axis=0).astype(o_ref.dtype)


def _halo_norm(x, nw, sh, sc, tm):
    m, d = x.shape
    nblk = m // tm
    r = SUBLANES_F32
    per = tm // r
    last = m // r - 1
    vec = pl.BlockSpec((1, d), lambda i: (0, 0))
    return pl.pallas_call(
        _halo_kernel,
        grid=(nblk,),
        in_specs=[
            pl.BlockSpec((r, d), lambda i: (jnp.maximum(i * per - 1, 0), 0)),
            pl.BlockSpec((r, d), lambda i: (jnp.minimum((i + 1) * per, last), 0)),
            vec, vec, vec,
        ],
        out_specs=pl.BlockSpec((1, 2 * r, d), lambda i: (i, 0, 0)),
        out_shape=jax.ShapeDtypeStruct((nblk, 2 * r, d), BF16),
        compiler_params=_params(1, 16 * 1024 * 1024),
        name="halo_norm",
    )(x, x, nw, sh, sc)


def _cast_spec(shape, layer, block_of_step):
    _, rows, cols = shape
    n_steps, step_of = block_of_step
    rb = max(SUBLANES_BF16, rows // n_steps // SUBLANES_BF16 * SUBLANES_BF16)
    while rows % rb:
        rb += SUBLANES_BF16
    n_blk = rows // rb

    def blk(i, j):
        return step_of(i, j) * n_blk // n_steps

    return (pl.BlockSpec((None, rb, cols), lambda i, j: (layer, blk(i, j), 0)),
            pl.BlockSpec((rb, cols), lambda i, j: (blk(i, j), 0)),
            jax.ShapeDtypeStruct((rows, cols), BF16))


def _inproj_kernel(*refs, j_off, qk_lo, qk_hi, n_cast):
    a_ref, b_ref, nw_ref = refs[:3]
    o_ref = refs[3 + n_cast]
    acc0, acc1 = refs[4 + 2 * n_cast:]
    for src, dst in zip(refs[3:3 + n_cast], refs[4 + n_cast:4 + 2 * n_cast]):
        dst[...] = src[...].astype(dst.dtype)
    tm = a_ref.shape[0]
    j = pl.program_id(1)
    nj = pl.num_programs(1) - 1
    tile = j - 1 + j_off
    is_qk = jnp.logical_and(tile >= qk_lo, tile < qk_hi)
    rc = min(tm, EPILOGUE_ROWS)

    def matmul(acc):
        acc[...] = _dot(a_ref[...], b_ref[...].astype(BF16))

    def epilogue(acc):
        for r0 in range(0, tm, rc):
            for h in range(o_ref.shape[1] // HEAD_DIM):
                sl = slice(h * HEAD_DIM, (h + 1) * HEAD_DIM)
                xh = acc[r0:r0 + rc, sl]
                ms = jnp.mean(xh * xh, axis=-1, keepdims=True)
                normed = xh * lax.rsqrt(ms + EPS) * nw_ref[0, :, sl]
                o_ref[r0:r0 + rc, sl] = jnp.where(is_qk, normed, xh).astype(o_ref.dtype)

    odd = lax.rem(j, 2) == 1
    mid = jnp.logical_and(j > 0, j < nj)

    @pl.when(j == 0)
    def _():
        matmul(acc0)

    @pl.when(jnp.logical_and(mid, odd))
    def _():
        epilogue(acc0)
        matmul(acc1)

    @pl.when(jnp.logical_and(mid, jnp.logical_not(odd)))
    def _():
        epilogue(acc1)
        matmul(acc0)

    @pl.when(jnp.logical_and(j == nj, odd))
    def _():
        epilogue(acc0)

    @pl.when(jnp.logical_and(j == nj, jnp.logical_not(odd)))
    def _():
        epilogue(acc1)


def _in_proj(h, w_in, layer, nw_tiles, tn, j_off, n_tiles, qk_lo, qk_hi, casts=()):
    m, d = h.shape
    tm = min(m, 1024)
    acc = pltpu.VMEM((tm, tn), F32)
    steps = (m // tm * n_tiles, lambda i, j: i * n_tiles + jnp.minimum(j, n_tiles - 1))
    cast_specs = [_cast_spec(w.shape, layer, steps) for w in casts]
    cast_bytes = sum(6 * 2 * s[1].block_shape[0] * s[1].block_shape[1] for s in cast_specs)
    out = pl.pallas_call(
        functools.partial(_inproj_kernel, j_off=j_off, qk_lo=qk_lo, qk_hi=qk_hi, n_cast=len(casts)),
        grid=(m // tm, n_tiles + 1),
        in_specs=[
            pl.BlockSpec((tm, d), lambda i, j: (i, 0)),
            pl.BlockSpec((None, d, tn), lambda i, j: (layer, 0, jnp.minimum(j, n_tiles - 1) + j_off)),
            pl.BlockSpec((1, 1, tn), lambda i, j: (jnp.maximum(j - 1, 0) + j_off, 0, 0)),
        ] + [s[0] for s in cast_specs],
        out_specs=[pl.BlockSpec((tm, tn), lambda i, j: (i, jnp.maximum(j - 1, 0)))] + [s[1] for s in cast_specs],
        out_shape=[jax.ShapeDtypeStruct((m, n_tiles * tn), BF16)] + [s[2] for s in cast_specs],
        scratch_shapes=[acc, acc],
        compiler_params=_params(2, 4 * tm * d + 10 * d * tn + 4 * tm * tn + 6 * tm * tn * 4 + cast_bytes),
        name="in_proj",
    )(h, w_in, nw_tiles, *casts)
    return out[0], out[1:]


def _nat_kernel(q_ref, k_ref, v_ref, kc_ref, vc_ref, bias_ref, o_ref, *, grid_rows, scale, heads):
    b = pl.program_id(1)
    ks = jnp.clip(b * ATTN_ROWS - WIN_ROWS // 2, 0, grid_rows - ATTN_KEY_ROWS)
    start = pl.multiple_of(ks * GRID_W, GRID_W)
    nkeys = ATTN_KEY_ROWS * GRID_W
    for h in range(heads):
        sl = slice(h * HEAD_DIM, (h + 1) * HEAD_DIM)
        q = q_ref[:, sl]
        kw = k_ref[pl.ds(start, nkeys), sl]
        vw = v_ref[pl.ds(start, nkeys), sl]
        s = _dot_nt(q, kw) * scale + bias_ref[0, h]
        sc = _dot_nt(q, kc_ref[:, sl]) * scale
        m = jnp.maximum(jnp.max(s, axis=-1, keepdims=True), jnp.max(sc, axis=-1, keepdims=True))
        p = jnp.exp(s - m)
        pc = jnp.exp(sc - m)
        l = jnp.sum(p, axis=-1, keepdims=True) + jnp.sum(pc, axis=-1, keepdims=True)
        o = _dot(p.astype(BF16), vw) + _dot(pc.astype(BF16), vc_ref[:, sl])
        o_ref[:, sl] = (o / l).astype(o_ref.dtype)


def _nat_bias(rpb, grid_rows):
    n_heads = rpb.shape[0]
    nb = grid_rows // ATTN_ROWS
    c = np.arange(GRID_W)[:, None]
    kc = np.arange(GRID_W)[None, :]
    c_start = np.clip(c - WIN_COLS // 2, 0, GRID_W - WIN_COLS)
    col_ok = (kc >= c_start) & (kc < c_start + WIN_COLS)
    dc = kc - c + (WIN_COLS - 1)
    n_dc = 2 * WIN_COLS - 1
    expand = ((dc[None] == np.arange(n_dc)[:, None, None]) & col_ok[None]).astype(np.float32)
    blocks = jnp.einsum("hrd,dn->hrn", rpb, expand.reshape(n_dc, GRID_W * GRID_W),
                        precision=lax.Precision.HIGHEST)
    blocks = blocks.reshape(n_heads, 2 * WIN_ROWS - 1, GRID_W, GRID_W)
    blocks = jnp.where(col_ok, blocks, NEG_BIAS)
    masked = jnp.full((n_heads, GRID_W, GRID_W), NEG_BIAS, F32)
    variants = []
    for b in (0, 1, nb - 1):
        ks = int(np.clip(b * ATTN_ROWS - WIN_ROWS // 2, 0, grid_rows - ATTN_KEY_ROWS))
        q_rows = []
        for r_loc in range(ATTN_ROWS):
            r = b * ATTN_ROWS + r_loc
            r_start = int(np.clip(r - WIN_ROWS // 2, 0, grid_rows - WIN_ROWS))
            row = []
            for kr in range(ks, ks + ATTN_KEY_ROWS):
                ok = r_start <= kr < r_start + WIN_ROWS
                row.append(blocks[:, kr - r + (WIN_ROWS - 1)] if ok else masked)
            q_rows.append(jnp.concatenate(row, axis=-1))
        variants.append(jnp.concatenate(q_rows, axis=-2))
    return jnp.stack(variants)


def _nat_attention(p_lat, p_ctx, bias, n_heads, q_blk, k_blk, v_blk, kc_blk, vc_blk, hs):
    s_len = p_lat.shape[0]
    n_ctx = p_ctx.shape[0]
    grid_rows = s_len // GRID_W
    nb = grid_rows // ATTN_ROWS
    nq = ATTN_ROWS * GRID_W
    nk = ATTN_KEY_ROWS * GRID_W
    assert grid_rows % ATTN_ROWS == 0 and nb >= 3 and grid_rows >= ATTN_KEY_ROWS
    while n_heads % hs or any(blk % hs for blk in (q_blk, k_blk, v_blk, kc_blk, vc_blk)):
        hs //= 2
    width = hs * HEAD_DIM

    def variant(b):
        return jnp.where(b == 0, 0, jnp.where(b == nb - 1, 2, 1))

    def cols(rows, blk):
        mode = pl.Buffered(1 if (rows == s_len and hs > 2) else 2)
        return pl.BlockSpec((rows, width), lambda h, b: (0, blk // hs + h), pipeline_mode=mode)

    return pl.pallas_call(
        functools.partial(_nat_kernel, grid_rows=grid_rows, scale=HEAD_DIM ** -0.5, heads=hs),
        grid=(n_heads // hs, nb),
        in_specs=[
            pl.BlockSpec((nq, width), lambda h, b: (b, q_blk // hs + h)),
            cols(s_len, k_blk), cols(s_len, v_blk), cols(n_ctx, kc_blk), cols(n_ctx, vc_blk),
            pl.BlockSpec((1, hs, nq, nk), lambda h, b: (variant(b), h, 0, 0)),
        ],
        out_specs=pl.BlockSpec((nq, width), lambda h, b: (b, h)),
        out_shape=jax.ShapeDtypeStruct((s_len, n_heads * HEAD_DIM), BF16),
        compiler_params=_params(2, 8 * s_len * width + hs * (2 * nq * nk * 4 + 16 * nq * (nk + n_ctx) * 4)),
        name="nat_attention",
    )(p_lat, p_lat, p_lat, p_ctx, p_ctx, bias)


def _ctx_attn_kernel(q_ref, k_ref, v_ref, o_ref, *, scale):
    s = _dot_nt(q_ref[...], k_ref[...]) * scale
    m = jnp.max(s, axis=-1, keepdims=True)
    p = jnp.exp(s - m)
    l = jnp.sum(p, axis=-1, keepdims=True)
    o_ref[...] = (_dot(p.astype(BF16), v_ref[...]) / l).astype(o_ref.dtype)


def _ctx_attention(p_ctx, n_heads, q_blk, k_blk, v_blk):
    n = p_ctx.shape[0]
    return pl.pallas_call(
        functools.partial(_ctx_attn_kernel, scale=HEAD_DIM ** -0.5),
        grid=(n_heads,),
        in_specs=[
            pl.BlockSpec((n, HEAD_DIM), lambda h: (0, q_blk + h)),
            pl.BlockSpec((n, HEAD_DIM), lambda h: (0, k_blk + h)),
            pl.BlockSpec((n, HEAD_DIM), lambda h: (0, v_blk + h)),
        ],
        out_specs=pl.BlockSpec((n, HEAD_DIM), lambda h: (0, h)),
        out_shape=jax.ShapeDtypeStruct((n, n_heads * HEAD_DIM), BF16),
        compiler_params=_params(1, 16 * 1024 * 1024),
        name="ctx_attention",
    )(p_ctx, p_ctx, p_ctx)


def _pool_kernel(up_ref, um_ref, un_ref, pw_ref, ps_ref, o_ref, *, seq_len, gc):
    i = pl.program_id(0)
    n = pl.num_programs(0)
    tp = um_ref.shape[0]
    halo = up_ref.shape[0]
    um = um_ref[...]
    up = jnp.where(i > 0, up_ref[...], jnp.zeros_like(up_ref[...]))
    un = jnp.where(i < n - 1, un_ref[...], jnp.zeros_like(un_ref[...]))
    ue = jnp.concatenate([up, um, un], axis=0)
    trow = lax.broadcasted_iota(jnp.int32, (tp, tp + 2 * halo), 0)
    jcol = lax.broadcasted_iota(jnp.int32, (tp, tp + 2 * halo), 1)
    off = jcol - trow - halo
    t_abs = i * tp + lax.broadcasted_iota(jnp.int32, (tp, gc), 0)
    for g, w in enumerate(POOL_WINDOWS):
        lo_off = -(w // 2)
        hi_off = w - 1 - w // 2
        band = jnp.where(off >= lo_off, jnp.where(off <= hi_off, 1.0, 0.0), 0.0).astype(BF16)
        sl = slice(g * gc, (g + 1) * gc)
        wsum = _dot(band, ue[:, sl])
        lo = jnp.maximum(t_abs + lo_off, 0)
        hi = jnp.minimum(t_abs + hi_off, seq_len - 1)
        cnt = (hi - lo + 1).astype(F32)
        d = (wsum / cnt - um[:, sl].astype(F32)).astype(BF16)
        o_ref[:, sl] = (_dot(d, pw_ref[g]) * ps_ref[:, sl]).astype(o_ref.dtype)


def _pool_mix(p, col_blk, pool_w, pool_scale):
    m = p.shape[0]
    gc = pool_w.shape[-1]
    width = N_GROUPS * gc
    tp = min(m, 512)
    halo = SUBLANES_BF16
    per = tp // halo
    last = m // halo - 1
    return pl.pallas_call(
        functools.partial(_pool_kernel, seq_len=m, gc=gc),
        grid=(m // tp,),
        in_specs=[
            pl.BlockSpec((halo, width), lambda i: (jnp.maximum(i * per - 1, 0), col_blk)),
            pl.BlockSpec((tp, width), lambda i: (i, col_blk)),
            pl.BlockSpec((halo, width), lambda i: (jnp.minimum((i + 1) * per, last), col_blk)),
            pl.BlockSpec((N_GROUPS, gc, gc), lambda i: (0, 0, 0)),
            pl.BlockSpec((1, width), lambda i: (0, 0)),
        ],
        out_specs=pl.BlockSpec((tp, width), lambda i: (i, 0)),
        out_shape=jax.ShapeDtypeStruct((m, width), BF16),
        compiler_params=_params(1, 32 * 1024 * 1024),
        name="pool_mix",
    )(p, p, p, pool_w, pool_scale)


def _cos_sin(num, den):
    ang = (num % den).astype(F32) * (2.0 * math.pi / den)
    return jnp.cos(ang), jnp.sin(ang)


def _dft_tables(n):
    idx = jnp.arange(n, dtype=jnp.int32)
    return _cos_sin(idx[:, None] * idx[None, :], n)


def _fnet_chan_kernel(u_ref, cs_ref, o_ref, *, gc):
    for g in range(N_GROUPS):
        sl = slice(g * gc, (g + 1) * gc)
        o_ref[:, sl] = _dot(u_ref[:, sl], cs_ref[0]).astype(o_ref.dtype)


def _fnet_pos_kernel(t_ref, ab_ref, fw_ref, o_ref, *, scale, gc):
    acc = _dot(t_ref[...], ab_ref[...])
    for g in range(N_GROUPS):
        sl = slice(g * gc, (g + 1) * gc)
        f = (acc[:, sl] * scale).astype(BF16)
        o_ref[:, sl] = _dot(f, fw_ref[g]).astype(o_ref.dtype)


def _fourier_mix_dense(p, col_blk, cs_chan, fnet_w):
    m = p.shape[0]
    gc = fnet_w.shape[-1]
    width = N_GROUPS * gc
    cpos, spos = _dft_tables(m)
    t_pos = jnp.concatenate([cpos, -spos], axis=1).astype(BF16)
    ab = pl.pallas_call(
        functools.partial(_fnet_chan_kernel, gc=gc),
        grid=(2,),
        in_specs=[
            pl.BlockSpec((m, width), lambda c: (0, col_blk)),
            pl.BlockSpec((1, gc, gc), lambda c: (c, 0, 0)),
        ],
        out_specs=pl.BlockSpec((m, width), lambda c: (c, 0)),
        out_shape=jax.ShapeDtypeStruct((2 * m, width), BF16),
        compiler_params=_params(1, 32 * 1024 * 1024),
        name="fnet_chan",
    )(p, cs_chan)
    return pl.pallas_call(
        functools.partial(_fnet_pos_kernel, scale=1.0 / math.sqrt(m * gc), gc=gc),
        grid=(1,),
        in_specs=[
            pl.BlockSpec((m, 2 * m), lambda i: (0, 0)),
            pl.BlockSpec((2 * m, width), lambda i: (0, 0)),
            pl.BlockSpec((N_GROUPS, gc, gc), lambda i: (0, 0, 0)),
        ],
        out_specs=pl.BlockSpec((m, width), lambda i: (0, 0)),
        out_shape=jax.ShapeDtypeStruct((m, width), BF16),
        compiler_params=_params(1, 32 * 1024 * 1024),
        name="fnet_pos",
    )(t_pos, ab, fnet_w)


def _fft_stage1_kernel(f1_ref, u_ref, tc_ref, ts_ref, zr_ref, zi_ref, *, n1):
    width = zr_ref.shape[1]
    reps = width // LANES
    for t in range(u_ref.shape[1] // width):
        y = _dot(f1_ref[...], u_ref[:, t * width:(t + 1) * width])
        yr, yi = y[:n1], y[n1:]
        tc = jnp.tile(tc_ref[:, t * LANES:(t + 1) * LANES], (1, reps))
        ts = jnp.tile(ts_ref[:, t * LANES:(t + 1) * LANES], (1, reps))
        zr_ref[t * n1:(t + 1) * n1, :] = (yr * tc + yi * ts).astype(zr_ref.dtype)
        zi_ref[t * n1:(t + 1) * n1, :] = (yi * tc - yr * ts).astype(zi_ref.dtype)


def _fft_stage2_kernel(fa_ref, fb_ref, zr_ref, zi_ref, cc_ref, sc_ref, fw_ref, o_ref, *, n2, gc, chunks, scale):
    width = N_GROUPS * gc
    p = _dot(fa_ref[...], zr_ref[...]) + _dot(fb_ref[...], zi_ref[...])
    for g in range(N_GROUPS):
        cols = [slice(ch * width + g * gc, ch * width + (g + 1) * gc) for ch in range(chunks)]
        pr = jnp.concatenate([p[:n2, c] for c in cols], axis=0).astype(BF16)
        pi = jnp.concatenate([p[n2:, c] for c in cols], axis=0).astype(BF16)
        f = ((_dot(pr, cc_ref[...]) + _dot(pi, sc_ref[...])) * scale).astype(BF16)
        y = _dot(f, fw_ref[g]).astype(o_ref.dtype)
        for ch, c in enumerate(cols):
            o_ref[:, c] = y[ch * n2:(ch + 1) * n2]


def _fourier_mix_fft(p, col_blk, cs_chan, fnet_w):
    m, d_in = p.shape
    gc = fnet_w.shape[-1]
    width = N_GROUPS * gc
    n2 = LANES
    n1 = m // n2
    i1 = jnp.arange(n1, dtype=jnp.int32)
    i2 = jnp.arange(n2, dtype=jnp.int32)
    c1, s1 = _cos_sin(i1[:, None] * i1[None, :], n1)
    f1 = jnp.concatenate([c1, -s1], axis=0).astype(BF16)
    tc, ts = _cos_sin(i1[:, None] * i2[None, :], m)
    tc = jnp.repeat(tc, LANES, axis=1)
    ts = jnp.repeat(ts, LANES, axis=1)
    c2, s2 = _cos_sin(i2[:, None] * i2[None, :], n2)
    fa = jnp.concatenate([c2, -s2], axis=0).astype(BF16)
    fb = jnp.concatenate([s2, c2], axis=0).astype(BF16)
    u = p[:, col_blk * width:(col_blk + 1) * width].reshape(n1, n2 * width)
    z_shape = jax.ShapeDtypeStruct((n2 * n1, width), BF16)
    sb = FFT_STAGE1_COLS
    zr, zi = pl.pallas_call(
        functools.partial(_fft_stage1_kernel, n1=n1),
        grid=(n2 // sb,),
        in_specs=[
            pl.BlockSpec((2 * n1, n1), lambda j: (0, 0)),
            pl.BlockSpec((n1, sb * width), lambda j: (0, j)),
            pl.BlockSpec((n1, sb * LANES), lambda j: (0, j)),
            pl.BlockSpec((n1, sb * LANES), lambda j: (0, j)),
        ],
        out_specs=[pl.BlockSpec((sb * n1, width), lambda j: (j, 0))] * 2,
        out_shape=[z_shape, z_shape],
        compiler_params=_params(1, 16 * 1024 * 1024),
        name="fft_stage1",
    )(f1, u, tc, ts)
    chunks = min(n1, 8)
    tn = chunks * width
    y = pl.pallas_call(
        functools.partial(_fft_stage2_kernel, n2=n2, gc=gc, chunks=chunks, scale=1.0 / math.sqrt(m * gc)),
        grid=(n1 // chunks,),
        in_specs=[
            pl.BlockSpec((2 * n2, n2), lambda j: (0, 0)),
            pl.BlockSpec((2 * n2, n2), lambda j: (0, 0)),
            pl.BlockSpec((n2, tn), lambda j: (0, j)),
            pl.BlockSpec((n2, tn), lambda j: (0, j)),
            pl.BlockSpec((None, gc, gc), lambda j: (0, 0, 0)),
            pl.BlockSpec((None, gc, gc), lambda j: (1, 0, 0)),
            pl.BlockSpec((N_GROUPS, gc, gc), lambda j: (0, 0, 0)),
        ],
        out_specs=pl.BlockSpec((n2, tn), lambda j: (0, j)),
        out_shape=jax.ShapeDtypeStruct((n2, n1 * width), BF16),
        compiler_params=_params(1, 40 * 1024 * 1024),
        name="fft_stage2",
    )(fa, fb, zr.reshape(n2, n1 * width), zi.reshape(n2, n1 * width), cs_chan, cs_chan, fnet_w)
    return y.reshape(m, width)


def _fourier_mix(p, col_blk, cs_chan, fnet_w):
    if p.shape[0] >= FFT_MIN_LEN and p.shape[0] % (LANES * SUBLANES_F32) == 0:
        return _fourier_mix_fft(p, col_blk, cs_chan, fnet_w)
    return _fourier_mix_dense(p, col_blk, cs_chan, fnet_w)


def _merge_kernel(h_ref, yp_ref, ya_ref, yf_ref, wg0, wg1, wg2, bg0, bg1, bg2, wp, wa, wf, o_ref):
    h = h_ref[...]

    def gate(w_ref, b_ref):
        return jax.nn.sigmoid(_dot(h, w_ref[...]) + b_ref[...])

    m = gate(wg0, bg0) * _dot(yp_ref[...], wp[...])
    m = m + gate(wg1, bg1) * _dot(ya_ref[...], wa[...])
    m = m + gate(wg2, bg2) * _dot(yf_ref[...], wf[...])
    o_ref[...] = m.astype(o_ref.dtype)


def _merge(h, y_pool, y_attn, y_fnet, layer, w_gate, b_gate, w_bp, w_ba, w_bf):
    m, d = h.shape
    dp, da, df = y_pool.shape[1], y_attn.shape[1], y_fnet.shape[1]
    tm = min(m, 1024)
    tn = _tile(d, 256)
    nj = d // tn

    def act(width):
        bufs = 1 if width == d else 2
        return pl.BlockSpec((tm, width), lambda i, j: (i, 0), pipeline_mode=pl.Buffered(bufs))

    def gate_w(b):
        return pl.BlockSpec((None, d, tn), lambda i, j: (layer, 0, b * nj + j))

    def gate_b(b):
        return pl.BlockSpec((None, 1, tn), lambda i, j: (layer, 0, b * nj + j))

    def br_w(k):
        return pl.BlockSpec((None, k, tn), lambda i, j: (layer, 0, j))

    vmem = 4 * tm * (d + dp + da + df) + 4 * tn * (3 * d + dp + da + df) + 12 * tm * tn * 4
    return pl.pallas_call(
        _merge_kernel,
        grid=(m // tm, nj),
        in_specs=[act(d), act(dp), act(da), act(df), gate_w(0), gate_w(1), gate_w(2),
                  gate_b(0), gate_b(1), gate_b(2), br_w(dp), br_w(da), br_w(df)],
        out_specs=pl.BlockSpec((tm, tn), lambda i, j: (i, j)),
        out_shape=jax.ShapeDtypeStruct((m, d), BF16),
        compiler_params=_params(2, vmem),
        name="merge",
    )(h, y_pool, y_attn, y_fnet, w_gate, w_gate, w_gate, b_gate, b_gate, b_gate, w_bp, w_ba, w_bf)


def _resid_kernel(a_ref, w_ref, x_ref, g_ref, o_ref):
    o_ref[...] = x_ref[...] + g_ref[...] * _dot(a_ref[...], w_ref[...])


def _resid_proj(a, w, layer, x, g, tm_pref):
    m, kk = a.shape
    n = w.shape[2]
    tm = min(m, tm_pref)
    tn = _tile(n, 512)
    return pl.pallas_call(
        _resid_kernel,
        grid=(m // tm, n // tn),
        in_specs=[
            pl.BlockSpec((tm, kk), lambda i, j: (i, 0)),
            pl.BlockSpec((None, kk, tn), lambda i, j: (layer, 0, j)),
            pl.BlockSpec((tm, tn), lambda i, j: (i, j)),
            pl.BlockSpec((1, tn), lambda i, j: (0, j)),
        ],
        out_specs=pl.BlockSpec((tm, tn), lambda i, j: (i, j)),
        out_shape=jax.ShapeDtypeStruct((m, n), F32),
        compiler_params=_params(2, 4 * tm * kk + 4 * kk * tn + 7 * tm * tn * 4),
        name="resid_proj",
    )(a, w, x, g)


def _ffn_up_kernel(*refs, n_cast):
    a_ref, halo_ref, wg_ref, wv_ref, cwg_ref, cwv_ref, cbg_ref, cbv_ref = refs[:8]
    o_ref = refs[8 + n_cast]
    a_ext = refs[-1]
    for src, dst in zip(refs[8:8 + n_cast], refs[9 + n_cast:9 + 2 * n_cast]):
        dst[...] = src[...].astype(dst.dtype)
    tm = a_ref.shape[0]

    @pl.when(pl.program_id(1) == 0)
    def _():
        a_ext[:tm, :] = a_ref[...]
        a_ext[tm:, :] = halo_ref[0]

    a = a_ext[...]
    row = lax.broadcasted_iota(jnp.int32, (tm, wg_ref.shape[1]), 0)

    def conv(w_ref, cw_ref, cb_ref):
        u = _dot(a, w_ref[...].astype(BF16))
        um = u[:tm]
        next_row = u[tm:tm + 1]
        prev_row = u[tm + 2 * SUBLANES_F32 - 1:tm + 2 * SUBLANES_F32]
        u_dn = jnp.where(row == 0, prev_row, pltpu.roll(um, 1, 0))
        u_up = jnp.where(row == tm - 1, next_row, pltpu.roll(um, tm - 1, 0))
        return u_dn * cw_ref[0:1] + um * cw_ref[1:2] + u_up * cw_ref[2:3] + cb_ref[...]

    gate = conv(wg_ref, cwg_ref, cbg_ref)
    val = conv(wv_ref, cwv_ref, cbv_ref)
    o_ref[...] = (gate * jax.nn.sigmoid(gate) * val).astype(o_ref.dtype)


def _ffn_up(h2, halo, layer, w_up, conv_w, conv_b, casts=()):
    m, d = h2.shape
    f = w_up.shape[2] // 2
    tm = m // halo.shape[0]
    ext = halo.shape[1]
    tn = _tile(f, 256)
    nj = f // tn
    w_bytes = jnp.dtype(w_up.dtype).itemsize
    a_bufs = 1 if w_bytes == 4 else 2
    vmem = (2 * a_bufs * tm * d + 2 * (tm + ext) * d + (4 * w_bytes + 4) * d * tn + 4 * tm * tn
            + 12 * (tm + ext) * tn * 4)

    def cols(rows, half):
        return pl.BlockSpec((None, rows, tn), lambda i, j: (layer, 0, half * nj + j))

    steps = (m // tm * nj, lambda i, j: i * nj + j)
    cast_specs = [_cast_spec(w.shape, layer, steps) for w in casts]
    vmem += sum(6 * 2 * s[1].block_shape[0] * s[1].block_shape[1] for s in cast_specs)
    out = pl.pallas_call(
        functools.partial(_ffn_up_kernel, n_cast=len(casts)),
        grid=(m // tm, nj),
        in_specs=[
            pl.BlockSpec((tm, d), lambda i, j: (i, 0), pipeline_mode=pl.Buffered(a_bufs)),
            pl.BlockSpec((1, ext, d), lambda i, j: (i, 0, 0)),
            cols(d, 0), cols(d, 1), cols(CONV_W, 0), cols(CONV_W, 1), cols(1, 0), cols(1, 1),
        ] + [s[0] for s in cast_specs],
        out_specs=[pl.BlockSpec((tm, tn), lambda i, j: (i, j))] + [s[1] for s in cast_specs],
        out_shape=[jax.ShapeDtypeStruct((m, f), BF16)] + [s[2] for s in cast_specs],
        scratch_shapes=[pltpu.VMEM((tm + ext, d), BF16)],
        compiler_params=_params(2, vmem),
        name="ffn_up",
    )(h2, halo, w_up, w_up, conv_w, conv_w, conv_b, conv_b, *casts)
    return out[0], out[1:]


def kernel(x, c, ctx, c_ctx, w_ada, b_ada, norm1_w, norm2_w, w_in, pool_w, pool_scale, q_norm_w, k_norm_w, rpb, fnet_w, w_gate, b_gate, w_br_pool, w_br_attn, w_br_fnet, w_o, w_up, conv_w, conv_b, w_down):
    n_layers, d, d_in = w_in.shape
    batch, s_len, _ = x.shape
    n_ctx = ctx.shape[1]
    assert batch == 1
    d_pool = w_br_pool.shape[1]
    d_attn = w_br_attn.shape[1]
    d_fnet = w_br_fnet.shape[1]
    n_heads = d_attn // HEAD_DIM
    assert d_in == d_pool + 3 * d_attn + d_fnet and s_len % GRID_W == 0 and d_fnet == d_pool
    q_lo, k_lo, v_lo, f_lo = d_pool, d_pool + d_attn, d_pool + 2 * d_attn, d_pool + 3 * d_attn

    pool_w_b = pool_w.astype(BF16)
    fnet_w_b = fnet_w.astype(BF16)
    b_gate_r = b_gate.reshape(n_layers, 1, -1)
    conv_b_r = conv_b.reshape(n_layers, 1, -1)
    ones_p = jnp.ones((d_pool,), F32)
    ones_vf = jnp.ones((d_attn + d_fnet,), F32)

    cs_chan = jnp.stack(_dft_tables(fnet_w.shape[-1])).astype(BF16)

    cond = jnp.concatenate([c.reshape(1, d), c_ctx.reshape(1, d), jnp.zeros((SUBLANES_F32 - 2, d), F32)], axis=0)
    mod = _ada(cond, w_ada, b_ada)

    tm_ffn = min(s_len, 1024)
    blk = lambda col: col // HEAD_DIM
    tn_in = _tile(d_pool, 512)
    assert d_pool % tn_in == 0 and d_attn % tn_in == 0
    qk_tiles = (q_lo // tn_in, v_lo // tn_in)

    xl = x.reshape(s_len, d)
    xc = ctx.reshape(n_ctx, d)
    for l in range(n_layers):
        last = l == n_layers - 1
        sh1, sc1, g1, sh2, sc2, g2 = [mod[l, 0:1, i * d:(i + 1) * d] for i in range(N_MOD)]
        csh1, csc1, cg1, csh2, csc2, cg2 = [mod[l, 1:2, i * d:(i + 1) * d] for i in range(N_MOD)]
        n1w = norm1_w[l].reshape(1, d)
        n2w = norm2_w[l].reshape(1, d)
        nw_tiles = jnp.concatenate(
            [ones_p, jnp.tile(q_norm_w[l], n_heads), jnp.tile(k_norm_w[l], n_heads), ones_vf]
        ).reshape(d_in // tn_in, 1, tn_in)
        bias = _nat_bias(rpb[l], s_len // GRID_W)
        pool_scale_l = pool_scale[l].reshape(1, -1)

        h = _norm_mod(xl, n1w, sh1, sc1)
        hc = _norm_mod(xc, n1w, csh1, csc1)
        p, w_bf16 = _in_proj(h, w_in, l, nw_tiles, tn_in, 0, d_in // tn_in, *qk_tiles,
                             casts=(w_gate, w_br_pool, w_br_attn, w_br_fnet, w_o))
        w_gate_b, w_bp_b, w_ba_b, w_bf_b, w_o_b = [w[None] for w in w_bf16]
        b_gate_l = b_gate_r[l:l + 1]
        if last:
            pc, _ = _in_proj(hc, w_in, l, nw_tiles, tn_in, k_lo // tn_in, (f_lo - k_lo) // tn_in, *qk_tiles)
            kc_blk, vc_blk = 0, blk(d_attn)
        else:
            pc, _ = _in_proj(hc, w_in, l, nw_tiles, tn_in, 0, d_in // tn_in, *qk_tiles)
            kc_blk, vc_blk = blk(k_lo), blk(v_lo)
        y_attn = _nat_attention(p, pc, bias, n_heads, blk(q_lo), blk(k_lo), blk(v_lo), kc_blk, vc_blk, ATTN_HEADS)
        y_pool = _pool_mix(p, 0, pool_w_b[l], pool_scale_l)
        y_fnet = _fourier_mix(p, f_lo // d_fnet, cs_chan, fnet_w_b[l])
        m = _merge(h, y_pool, y_attn, y_fnet, 0, w_gate_b, b_gate_l, w_bp_b, w_ba_b, w_bf_b)
        xl = _resid_proj(m, w_o_b, 0, xl, g1, 1024)

        h2 = _norm_mod(xl, n2w, sh2, sc2)
        halo = _halo_norm(xl, n2w, sh2, sc2, tm_ffn)
        t, (w_down_l,) = _ffn_up(h2, halo, l, w_up, conv_w, conv_b_r, casts=(w_down,))
        w_down_b = w_down_l[None]
        xl = _resid_proj(t, w_down_b, 0, xl, g2, 512)

        if not last:
            yc_attn = _ctx_attention(pc, n_heads, blk(q_lo), blk(k_lo), blk(v_lo))
            yc_pool = _pool_mix(pc, 0, pool_w_b[l], pool_scale_l)
            yc_fnet = _fourier_mix(pc, f_lo // d_fnet, cs_chan, fnet_w_b[l])
            mc = _merge(hc, yc_pool, yc_attn, yc_fnet, 0, w_gate_b, b_gate_l, w_bp_b, w_ba_b, w_bf_b)
            xc = _resid_proj(mc, w_o_b, 0, xc, cg1, 1024)
            hc2 = _norm_mod(xc, n2w, csh2, csc2)
            halo_c = _halo_norm(xc, n2w, csh2, csc2, n_ctx)
            tc, _ = _ffn_up(hc2, halo_c, l, w_up, conv_w, conv_b_r)
            xc = _resid_proj(tc, w_down_b, 0, xc, cg2, 512)
    return xl.reshape(batch, s_len, d)
```

```python
import functools
import math

import numpy as np
import jax
import jax.numpy as jnp
from jax import lax
from jax.experimental import pallas as pl
from jax.experimental.pallas import tpu as pltpu

F32 = jnp.float32
BF16 = jnp.bfloat16

GRID_W = 64
HEAD_DIM = 128
WIN_ROWS = 8
WIN_COLS = 16
POOL_WINDOWS = (2, 4, 8, 16)
N_GROUPS = 4
N_MOD = 6
CONV_W = 3
EPS = 1e-6

LANES = 128
SUBLANES_F32 = 8
SUBLANES_BF16 = 16
VMEM_CAP_BYTES = 56 * 1024 * 1024

ATTN_ROWS = 4
ATTN_KEY_ROWS = ATTN_ROWS + WIN_ROWS
ATTN_HEADS = 8
FFT_STAGE1_COLS = 16
EPILOGUE_ROWS = 256
NEG_BIAS = -1e30
FFT_MIN_LEN = 1024


def _tile(n, pref):
    if n <= pref:
        return n
    t = pref - pref % LANES
    while t >= LANES:
        if n % t == 0:
            return t
        t -= LANES
    return n


def _params(n_axes, vmem_bytes):
    return pltpu.CompilerParams(
        dimension_semantics=("arbitrary",) * n_axes,
        vmem_limit_bytes=int(min(VMEM_CAP_BYTES, max(vmem_bytes, 16 * 1024 * 1024))),
    )


def _dot(a, b):
    return jnp.dot(a, b, preferred_element_type=F32)


def _dot_nt(a, b):
    return lax.dot_general(a, b, (((1,), (1,)), ((), ())), preferred_element_type=F32)


def _rms_mod(x, nw, sh, sc):
    ms = jnp.mean(x * x, axis=-1, keepdims=True)
    return (x * lax.rsqrt(ms + EPS) * nw) * (1.0 + sc) + sh


def _ada_kernel(s_ref, w_ref, b_ref, o_ref):
    s = s_ref[...]
    a = (s * jax.nn.sigmoid(s)).astype(BF16)
    o_ref[0] = _dot(a, w_ref[0].astype(BF16)) + b_ref[0]


def _ada(cond, w_ada, b_ada):
    n_layers, d, n = w_ada.shape
    tn = _tile(n, 1024)
    return pl.pallas_call(
        _ada_kernel,
        grid=(n_layers, n // tn),
        in_specs=[
            pl.BlockSpec((SUBLANES_F32, d), lambda l, j: (0, 0)),
            pl.BlockSpec((1, d, tn), lambda l, j: (l, 0, j)),
            pl.BlockSpec((1, 1, tn), lambda l, j: (l, 0, j)),
        ],
        out_specs=pl.BlockSpec((1, SUBLANES_F32, tn), lambda l, j: (l, 0, j)),
        out_shape=jax.ShapeDtypeStruct((n_layers, SUBLANES_F32, n), F32),
        compiler_params=_params(2, 2 * d * tn * 4 + 8 * 1024 * 1024),
        name="ada",
    )(cond, w_ada, b_ada.reshape(n_layers, 1, n))


def _norm_kernel(x_ref, nw_ref, sh_ref, sc_ref, o_ref):
    o_ref[...] = _rms_mod(x_ref[...], nw_ref[...], sh_ref[...], sc_ref[...]).astype(o_ref.dtype)


def _norm_mod(x, nw, sh, sc):
    m, d = x.shape
    tm = min(m, 512)
    vec = pl.BlockSpec((1, d), lambda i: (0, 0))
    return pl.pallas_call(
        _norm_kernel,
        grid=(m // tm,),
        in_specs=[pl.BlockSpec((tm, d), lambda i: (i, 0)), vec, vec, vec],
        out_specs=pl.BlockSpec((tm, d), lambda i: (i, 0)),
        out_shape=jax.ShapeDtypeStruct((m, d), BF16),
        compiler_params=_params(1, 6 * tm * d * 4),
        name="norm_mod",
    )(x, nw, sh, sc)


def _halo_kernel(xp_ref, xn_ref, nw_ref, sh_ref, sc_ref, o_ref):
    i = pl.program_id(0)
    n = pl.num_programs(0)
    hp = _rms_mod(xp_ref[...], nw_ref[...], sh_ref[...], sc_ref[...])
    hn = _rms_mod(xn_ref[...], nw_ref[...], sh_ref[...], sc_ref[...])
    hp = jnp.where(i > 0, hp, 0.0)
    hn = jnp.where(i < n - 1, hn, 0.0)
    o_ref[0] = jnp.concatenate([hn, hp], axis=0).astype(o_ref.dtype)


def _halo_norm(x, nw, sh, sc, tm):
    m, d = x.shape
    nblk = m // tm
    r = SUBLANES_F32
    per = tm // r
    last = m // r - 1
    vec = pl.BlockSpec((1, d), lambda i: (0, 0))
    return pl.pallas_call(
        _halo_kernel,
        grid=(nblk,),
        in_specs=[
            pl.BlockSpec((r, d), lambda i: (jnp.maximum(i * per - 1, 0), 0)),
            pl.BlockSpec((r, d), lambda i: (jnp.minimum((i + 1) * per, last), 0)),
            vec, vec, vec,
        ],
        out_specs=pl.BlockSpec((1, 2 * r, d), lambda i: (i, 0, 0)),
        out_shape=jax.ShapeDtypeStruct((nblk, 2 * r, d), BF16),
        compiler_params=_params(1, 16 * 1024 * 1024),
        name="halo_norm",
    )(x, x, nw, sh, sc)


def _cast_spec(shape, layer, block_of_step):
    _, rows, cols = shape
    n_steps, step_of = block_of_step
    rb = max(SUBLANES_BF16, rows // n_steps // SUBLANES_BF16 * SUBLANES_BF16)
    while rows % rb:
        rb += SUBLANES_BF16
    n_blk = rows // rb

    def blk(i, j):
        return step_of(i, j) * n_blk // n_steps

    return (pl.BlockSpec((None, rb, cols), lambda i, j: (layer, blk(i, j), 0)),
            pl.BlockSpec((rb, cols), lambda i, j: (blk(i, j), 0)),
            jax.ShapeDtypeStruct((rows, cols), BF16))


def _inproj_kernel(*refs, j_off, qk_lo, qk_hi, n_cast):
    a_ref, b_ref, nw_ref = refs[:3]
    o_ref = refs[3 + n_cast]
    acc0, acc1 = refs[4 + 2 * n_cast:]
    for src, dst in zip(refs[3:3 + n_cast], refs[4 + n_cast:4 + 2 * n_cast]):
        dst[...] = src[...].astype(dst.dtype)
    tm = a_ref.shape[0]
    j = pl.program_id(1)
    nj = pl.num_programs(1) - 1
    tile = j - 1 + j_off
    is_qk = jnp.logical_and(tile >= qk_lo, tile < qk_hi)
    rc = min(tm, EPILOGUE_ROWS)

    def matmul(acc):
        acc[...] = _dot(a_ref[...], b_ref[...].astype(BF16))

    def epilogue(acc):
        for r0 in range(0, tm, rc):
            for h in range(o_ref.shape[1] // HEAD_DIM):
                sl = slice(h * HEAD_DIM, (h + 1) * HEAD_DIM)
                xh = acc[r0:r0 + rc, sl]
                ms = jnp.mean(xh * xh, axis=-1, keepdims=True)
                normed = xh * lax.rsqrt(ms + EPS) * nw_ref[0, :, sl]
                o_ref[r0:r0 + rc, sl] = jnp.where(is_qk, normed, xh).astype(o_ref.dtype)

    odd = lax.rem(j, 2) == 1
    mid = jnp.logical_and(j > 0, j < nj)

    @pl.when(j == 0)
    def _():
        matmul(acc0)

    @pl.when(jnp.logical_and(mid, odd))
    def _():
        epilogue(acc0)
        matmul(acc1)

    @pl.when(jnp.logical_and(mid, jnp.logical_not(odd)))
    def _():
        epilogue(acc1)
        matmul(acc0)

    @pl.when(jnp.logical_and(j == nj, odd))
    def _():
        epilogue(acc0)

    @pl.when(jnp.logical_and(j == nj, jnp.logical_not(odd)))
    def _():
        epilogue(acc1)


def _in_proj(h, w_in, layer, nw_tiles, tn, j_off, n_tiles, qk_lo, qk_hi, casts=()):
    m, d = h.shape
    tm = min(m, 1024)
    acc = pltpu.VMEM((tm, tn), F32)
    steps = (m // tm * n_tiles, lambda i, j: i * n_tiles + jnp.minimum(j, n_tiles - 1))
    cast_specs = [_cast_spec(w.shape, layer, steps) for w in casts]
    cast_bytes = sum(6 * 2 * s[1].block_shape[0] * s[1].block_shape[1] for s in cast_specs)
    out = pl.pallas_call(
        functools.partial(_inproj_kernel, j_off=j_off, qk_lo=qk_lo, qk_hi=qk_hi, n_cast=len(casts)),
        grid=(m // tm, n_tiles + 1),
        in_specs=[
            pl.BlockSpec((tm, d), lambda i, j: (i, 0)),
            pl.BlockSpec((None, d, tn), lambda i, j: (layer, 0, jnp.minimum(j, n_tiles - 1) + j_off)),
            pl.BlockSpec((1, 1, tn), lambda i, j: (jnp.maximum(j - 1, 0) + j_off, 0, 0)),
        ] + [s[0] for s in cast_specs],
        out_specs=[pl.BlockSpec((tm, tn), lambda i, j: (i, jnp.maximum(j - 1, 0)))] + [s[1] for s in cast_specs],
        out_shape=[jax.ShapeDtypeStruct((m, n_tiles * tn), BF16)] + [s[2] for s in cast_specs],
        scratch_shapes=[acc, acc],
        compiler_params=_params(2, 4 * tm * d + 10 * d * tn + 4 * tm * tn + 6 * tm * tn * 4 + cast_bytes),
        name="in_proj",
    )(h, w_in, nw_tiles, *casts)
    return out[0], out[1:]


def _nat_kernel(q_ref, k_ref, v_ref, kc_ref, vc_ref, bias_ref, o_ref, *, grid_rows, scale, heads):
    b = pl.program_id(1)
    ks = jnp.clip(b * ATTN_ROWS - WIN_ROWS // 2, 0, grid_rows - ATTN_KEY_ROWS)
    start = pl.multiple_of(ks * GRID_W, GRID_W)
    nkeys = ATTN_KEY_ROWS * GRID_W
    for h in range(heads):
        sl = slice(h * HEAD_DIM, (h + 1) * HEAD_DIM)
        q = q_ref[:, sl]
        kw = k_ref[pl.ds(start, nkeys), sl]
        vw = v_ref[pl.ds(start, nkeys), sl]
        s = _dot_nt(q, kw) * scale + bias_ref[0, h]
        sc = _dot_nt(q, kc_ref[:, sl]) * scale
        m = jnp.maximum(jnp.max(s, axis=-1, keepdims=True), jnp.max(sc, axis=-1, keepdims=True))
        p = jnp.exp(s - m)
        pc = jnp.exp(sc - m)
        l = jnp.sum(p, axis=-1, keepdims=True) + jnp.sum(pc, axis=-1, keepdims=True)
        o = _dot(p.astype(BF16), vw) + _dot(pc.astype(BF16), vc_ref[:, sl])
        o_ref[:, sl] = (o / l).astype(o_ref.dtype)


def _nat_bias(rpb, grid_rows):
    n_heads = rpb.shape[0]
    nb = grid_rows // ATTN_ROWS
    c = np.arange(GRID_W)[:, None]
    kc = np.arange(GRID_W)[None, :]
    c_start = np.clip(c - WIN_COLS // 2, 0, GRID_W - WIN_COLS)
    col_ok = (kc >= c_start) & (kc < c_start + WIN_COLS)
    dc = kc - c + (WIN_COLS - 1)
    n_dc = 2 * WIN_COLS - 1
    expand = ((dc[None] == np.arange(n_dc)[:, None, None]) & col_ok[None]).astype(np.float32)
    n_dr = 2 * WIN_ROWS - 1
    blocks = jnp.einsum("hrd,dn->hrn", rpb, expand.reshape(n_dc, GRID_W * GRID_W),
                        precision=lax.Precision.HIGHEST)
    blocks = jnp.where(col_ok.reshape(-1), blocks, NEG_BIAS)
    pick = np.zeros((3, ATTN_ROWS, ATTN_KEY_ROWS, n_dr), np.float32)
    row_mask = np.full((3, ATTN_ROWS, ATTN_KEY_ROWS), NEG_BIAS, np.float32)
    for v, b in enumerate((0, 1, nb - 1)):
        ks = int(np.clip(b * ATTN_ROWS - WIN_ROWS // 2, 0, grid_rows - ATTN_KEY_ROWS))
        for r_loc in range(ATTN_ROWS):
            r = b * ATTN_ROWS + r_loc
            r_start = int(np.clip(r - WIN_ROWS // 2, 0, grid_rows - WIN_ROWS))
            for k_loc in range(ATTN_KEY_ROWS):
                kr = ks + k_loc
                if r_start <= kr < r_start + WIN_ROWS:
                    pick[v, r_loc, k_loc, kr - r + (WIN_ROWS - 1)] = 1.0
                    row_mask[v, r_loc, k_loc] = 0.0
    tiles = jnp.einsum("vrkd,hdn->vhrkn", pick, blocks, precision=lax.Precision.HIGHEST)
    tiles = tiles + row_mask[:, None, :, :, None]
    tiles = tiles.reshape(3, n_heads, ATTN_ROWS, ATTN_KEY_ROWS, GRID_W, GRID_W)
    tiles = jnp.transpose(tiles, (0, 1, 2, 4, 3, 5))
    return tiles.reshape(3, n_heads, ATTN_ROWS * GRID_W, ATTN_KEY_ROWS * GRID_W)


def _nat_attention(p_lat, p_ctx, bias, n_heads, q_blk, k_blk, v_blk, kc_blk, vc_blk, hs):
    s_len = p_lat.shape[0]
    n_ctx = p_ctx.shape[0]
    grid_rows = s_len // GRID_W
    nb = grid_rows // ATTN_ROWS
    nq = ATTN_ROWS * GRID_W
    nk = ATTN_KEY_ROWS * GRID_W
    assert grid_rows % ATTN_ROWS == 0 and nb >= 3 and grid_rows >= ATTN_KEY_ROWS
    while n_heads % hs or any(blk % hs for blk in (q_blk, k_blk, v_blk, kc_blk, vc_blk)):
        hs //= 2
    width = hs * HEAD_DIM

    def variant(b):
        return jnp.where(b == 0, 0, jnp.where(b == nb - 1, 2, 1))

    def cols(rows, blk):
        mode = pl.Buffered(1 if (rows == s_len and hs > 2) else 2)
        return pl.BlockSpec((rows, width), lambda h, b: (0, blk // hs + h), pipeline_mode=mode)

    return pl.pallas_call(
        functools.partial(_nat_kernel, grid_rows=grid_rows, scale=HEAD_DIM ** -0.5, heads=hs),
        grid=(n_heads // hs, nb),
        in_specs=[
            pl.BlockSpec((nq, width), lambda h, b: (b, q_blk // hs + h)),
            cols(s_len, k_blk), cols(s_len, v_blk), cols(n_ctx, kc_blk), cols(n_ctx, vc_blk),
            pl.BlockSpec((1, hs, nq, nk), lambda h, b: (variant(b), h, 0, 0)),
        ],
        out_specs=pl.BlockSpec((nq, width), lambda h, b: (b, h)),
        out_shape=jax.ShapeDtypeStruct((s_len, n_heads * HEAD_DIM), BF16),
        compiler_params=_params(2, 8 * s_len * width + hs * (2 * nq * nk * 4 + 16 * nq * (nk + n_ctx) * 4)),
        name="nat_attention",
    )(p_lat, p_lat, p_lat, p_ctx, p_ctx, bias)


def _ctx_attn_kernel(q_ref, k_ref, v_ref, o_ref, *, scale):
    s = _dot_nt(q_ref[...], k_ref[...]) * scale
    m = jnp.max(s, axis=-1, keepdims=True)
    p = jnp.exp(s - m)
    l = jnp.sum(p, axis=-1, keepdims=True)
    o_ref[...] = (_dot(p.astype(BF16), v_ref[...]) / l).astype(o_ref.dtype)


def _ctx_attention(p_ctx, n_heads, q_blk, k_blk, v_blk):
    n = p_ctx.shape[0]
    return pl.pallas_call(
        functools.partial(_ctx_attn_kernel, scale=HEAD_DIM ** -0.5),
        grid=(n_heads,),
        in_specs=[
            pl.BlockSpec((n, HEAD_DIM), lambda h: (0, q_blk + h)),
            pl.BlockSpec((n, HEAD_DIM), lambda h: (0, k_blk + h)),
            pl.BlockSpec((n, HEAD_DIM), lambda h: (0, v_blk + h)),
        ],
        out_specs=pl.BlockSpec((n, HEAD_DIM), lambda h: (0, h)),
        out_shape=jax.ShapeDtypeStruct((n, n_heads * HEAD_DIM), BF16),
        compiler_params=_params(1, 16 * 1024 * 1024),
        name="ctx_attention",
    )(p_ctx, p_ctx, p_ctx)


def _pool_kernel(up_ref, um_ref, un_ref, pw_ref, ps_ref, o_ref, *, seq_len, gc):
    i = pl.program_id(0)
    n = pl.num_programs(0)
    tp = um_ref.shape[0]
    halo = up_ref.shape[0]
    um = um_ref[...]
    up = jnp.where(i > 0, up_ref[...], jnp.zeros_like(up_ref[...]))
    un = jnp.where(i < n - 1, un_ref[...], jnp.zeros_like(un_ref[...]))
    ue = jnp.concatenate([up, um, un], axis=0)
    trow = lax.broadcasted_iota(jnp.int32, (tp, tp + 2 * halo), 0)
    jcol = lax.broadcasted_iota(jnp.int32, (tp, tp + 2 * halo), 1)
    off = jcol - trow - halo
    t_abs = i * tp + lax.broadcasted_iota(jnp.int32, (tp, gc), 0)
    for g, w in enumerate(POOL_WINDOWS):
        lo_off = -(w // 2)
        hi_off = w - 1 - w // 2
        band = jnp.where(off >= lo_off, jnp.where(off <= hi_off, 1.0, 0.0), 0.0).astype(BF16)
        sl = slice(g * gc, (g + 1) * gc)
        wsum = _dot(band, ue[:, sl])
        lo = jnp.maximum(t_abs + lo_off, 0)
        hi = jnp.minimum(t_abs + hi_off, seq_len - 1)
        cnt = (hi - lo + 1).astype(F32)
        d = (wsum / cnt - um[:, sl].astype(F32)).astype(BF16)
        o_ref[:, sl] = (_dot(d, pw_ref[g]) * ps_ref[:, sl]).astype(o_ref.dtype)


def _pool_mix(p, col_blk, pool_w, pool_scale):
    m = p.shape[0]
    gc = pool_w.shape[-1]
    width = N_GROUPS * gc
    tp = min(m, 512)
    halo = SUBLANES_BF16
    per = tp // halo
    last = m // halo - 1
    return pl.pallas_call(
        functools.partial(_pool_kernel, seq_len=m, gc=gc),
        grid=(m // tp,),
        in_specs=[
            pl.BlockSpec((halo, width), lambda i: (jnp.maximum(i * per - 1, 0), col_blk)),
            pl.BlockSpec((tp, width), lambda i: (i, col_blk)),
            pl.BlockSpec((halo, width), lambda i: (jnp.minimum((i + 1) * per, last), col_blk)),
            pl.BlockSpec((N_GROUPS, gc, gc), lambda i: (0, 0, 0)),
            pl.BlockSpec((1, width), lambda i: (0, 0)),
        ],
        out_specs=pl.BlockSpec((tp, width), lambda i: (i, 0)),
        out_shape=jax.ShapeDtypeStruct((m, width), BF16),
        compiler_params=_params(1, 32 * 1024 * 1024),
        name="pool_mix",
    )(p, p, p, pool_w, pool_scale)


def _cos_sin(num, den):
    ang = (num % den).astype(F32) * (2.0 * math.pi / den)
    return jnp.cos(ang), jnp.sin(ang)


def _dft_tables(n):
    idx = jnp.arange(n, dtype=jnp.int32)
    return _cos_sin(idx[:, None] * idx[None, :], n)


def _fnet_chan_kernel(u_ref, cs_ref, o_ref, *, gc):
    for g in range(N_GROUPS):
        sl = slice(g * gc, (g + 1) * gc)
        o_ref[:, sl] = _dot(u_ref[:, sl], cs_ref[0]).astype(o_ref.dtype)


def _fnet_pos_kernel(t_ref, ab_ref, fw_ref, o_ref, *, scale, gc):
    acc = _dot(t_ref[...], ab_ref[...])
    for g in range(N_GROUPS):
        sl = slice(g * gc, (g + 1) * gc)
        f = (acc[:, sl] * scale).astype(BF16)
        o_ref[:, sl] = _dot(f, fw_ref[g]).astype(o_ref.dtype)


def _fourier_mix_dense(p, col_blk, cs_chan, fnet_w):
    m = p.shape[0]
    gc = fnet_w.shape[-1]
    width = N_GROUPS * gc
    cpos, spos = _dft_tables(m)
    t_pos = jnp.concatenate([cpos, -spos], axis=1).astype(BF16)
    ab = pl.pallas_call(
        functools.partial(_fnet_chan_kernel, gc=gc),
        grid=(2,),
        in_specs=[
            pl.BlockSpec((m, width), lambda c: (0, col_blk)),
            pl.BlockSpec((1, gc, gc), lambda c: (c, 0, 0)),
        ],
        out_specs=pl.BlockSpec((m, width), lambda c: (c, 0)),
        out_shape=jax.ShapeDtypeStruct((2 * m, width), BF16),
        compiler_params=_params(1, 32 * 1024 * 1024),
        name="fnet_chan",
    )(p, cs_chan)
    return pl.pallas_call(
        functools.partial(_fnet_pos_kernel, scale=1.0 / math.sqrt(m * gc), gc=gc),
        grid=(1,),
        in_specs=[
            pl.BlockSpec((m, 2 * m), lambda i: (0, 0)),
            pl.BlockSpec((2 * m, width), lambda i: (0, 0)),
            pl.BlockSpec((N_GROUPS, gc, gc), lambda i: (0, 0, 0)),
        ],
        out_specs=pl.BlockSpec((m, width), lambda i: (0, 0)),
        out_shape=jax.ShapeDtypeStruct((m, width), BF16),
        compiler_params=_params(1, 32 * 1024 * 1024),
        name="fnet_pos",
    )(t_pos, ab, fnet_w)


def _fft_stage1_kernel(f1_ref, u_ref, tc_ref, ts_ref, zr_ref, zi_ref, *, n1):
    width = zr_ref.shape[1]
    reps = width // LANES
    for t in range(u_ref.shape[1] // width):
        y = _dot(f1_ref[...], u_ref[:, t * width:(t + 1) * width])
        yr, yi = y[:n1], y[n1:]
        tc = jnp.tile(tc_ref[:, t * LANES:(t + 1) * LANES], (1, reps))
        ts = jnp.tile(ts_ref[:, t * LANES:(t + 1) * LANES], (1, reps))
        zr_ref[t * n1:(t + 1) * n1, :] = (yr * tc + yi * ts).astype(zr_ref.dtype)
        zi_ref[t * n1:(t + 1) * n1, :] = (yi * tc - yr * ts).astype(zi_ref.dtype)


def _fft_stage2_kernel(fa_ref, fb_ref, zr_ref, zi_ref, cc_ref, sc_ref, fw_ref, o_ref, *, n2, gc, chunks, scale):
    width = N_GROUPS * gc
    p = _dot(fa_ref[...], zr_ref[...]) + _dot(fb_ref[...], zi_ref[...])
    for g in range(N_GROUPS):
        cols = [slice(ch * width + g * gc, ch * width + (g + 1) * gc) for ch in range(chunks)]
        pr = jnp.concatenate([p[:n2, c] for c in cols], axis=0).astype(BF16)
        pi = jnp.concatenate([p[n2:, c] for c in cols], axis=0).astype(BF16)
        f = ((_dot(pr, cc_ref[...]) + _dot(pi, sc_ref[...])) * scale).astype(BF16)
        y = _dot(f, fw_ref[g]).astype(o_ref.dtype)
        for ch, c in enumerate(cols):
            o_ref[:, c] = y[ch * n2:(ch + 1) * n2]


def _fourier_mix_fft(p, col_blk, cs_chan, fnet_w):
    m, d_in = p.shape
    gc = fnet_w.shape[-1]
    width = N_GROUPS * gc
    n2 = LANES
    n1 = m // n2
    i1 = jnp.arange(n1, dtype=jnp.int32)
    i2 = jnp.arange(n2, dtype=jnp.int32)
    c1, s1 = _cos_sin(i1[:, None] * i1[None, :], n1)
    f1 = jnp.concatenate([c1, -s1], axis=0).astype(BF16)
    tc, ts = _cos_sin(i1[:, None] * i2[None, :], m)
    tc = jnp.repeat(tc, LANES, axis=1)
    ts = jnp.repeat(ts, LANES, axis=1)
    c2, s2 = _cos_sin(i2[:, None] * i2[None, :], n2)
    fa = jnp.concatenate([c2, -s2], axis=0).astype(BF16)
    fb = jnp.concatenate([s2, c2], axis=0).astype(BF16)
    u = p[:, col_blk * width:(col_blk + 1) * width].reshape(n1, n2 * width)
    z_shape = jax.ShapeDtypeStruct((n2 * n1, width), BF16)
    sb = FFT_STAGE1_COLS
    zr, zi = pl.pallas_call(
        functools.partial(_fft_stage1_kernel, n1=n1),
        grid=(n2 // sb,),
        in_specs=[
            pl.BlockSpec((2 * n1, n1), lambda j: (0, 0)),
            pl.BlockSpec((n1, sb * width), lambda j: (0, j)),
            pl.BlockSpec((n1, sb * LANES), lambda j: (0, j)),
            pl.BlockSpec((n1, sb * LANES), lambda j: (0, j)),
        ],
        out_specs=[pl.BlockSpec((sb * n1, width), lambda j: (j, 0))] * 2,
        out_shape=[z_shape, z_shape],
        compiler_params=_params(1, 16 * 1024 * 1024),
        name="fft_stage1",
    )(f1, u, tc, ts)
    chunks = min(n1, 8)
    tn = chunks * width
    y = pl.pallas_call(
        functools.partial(_fft_stage2_kernel, n2=n2, gc=gc, chunks=chunks, scale=1.0 / math.sqrt(m * gc)),
        grid=(n1 // chunks,),
        in_specs=[
            pl.BlockSpec((2 * n2, n2), lambda j: (0, 0)),
            pl.BlockSpec((2 * n2, n2), lambda j: (0, 0)),
            pl.BlockSpec((n2, tn), lambda j: (0, j)),
            pl.BlockSpec((n2, tn), lambda j: (0, j)),
            pl.BlockSpec((None, gc, gc), lambda j: (0, 0, 0)),
            pl.BlockSpec((None, gc, gc), lambda j: (1, 0, 0)),
            pl.BlockSpec((N_GROUPS, gc, gc), lambda j: (0, 0, 0)),
        ],
        out_specs=pl.BlockSpec((n2, tn), lambda j: (0, j)),
        out_shape=jax.ShapeDtypeStruct((n2, n1 * width), BF16),
        compiler_params=_params(1, 40 * 1024 * 1024),
        name="fft_stage2",
    )(fa, fb, zr.reshape(n2, n1 * width), zi.reshape(n2, n1 * width), cs_chan, cs_chan, fnet_w)
    return y.reshape(m, width)


def _fourier_mix(p, col_blk, cs_chan, fnet_w):
    if p.shape[0] >= FFT_MIN_LEN and p.shape[0] % (LANES * SUBLANES_F32) == 0:
        return _fourier_mix_fft(p, col_blk, cs_chan, fnet_w)
    return _fourier_mix_dense(p, col_blk, cs_chan, fnet_w)


def _merge_kernel(h_ref, yp_ref, ya_ref, yf_ref, wg0, wg1, wg2, bg0, bg1, bg2, wp, wa, wf, o_ref):
    h = h_ref[...]

    def gate(w_ref, b_ref):
        return jax.nn.sigmoid(_dot(h, w_ref[...]) + b_ref[...])

    m = gate(wg0, bg0) * _dot(yp_ref[...], wp[...])
    m = m + gate(wg1, bg1) * _dot(ya_ref[...], wa[...])
    m = m + gate(wg2, bg2) * _dot(yf_ref[...], wf[...])
    o_ref[...] = m.astype(o_ref.dtype)


def _merge(h, y_pool, y_attn, y_fnet, layer, w_gate, b_gate, w_bp, w_ba, w_bf):
    m, d = h.shape
    dp, da, df = y_pool.shape[1], y_attn.shape[1], y_fnet.shape[1]
    tm = min(m, 1024)
    tn = _tile(d, 256)
    nj = d // tn

    def act(width):
        bufs = 1 if width == d else 2
        return pl.BlockSpec((tm, width), lambda i, j: (i, 0), pipeline_mode=pl.Buffered(bufs))

    def gate_w(b):
        return pl.BlockSpec((None, d, tn), lambda i, j: (layer, 0, b * nj + j))

    def gate_b(b):
        return pl.BlockSpec((None, 1, tn), lambda i, j: (layer, 0, b * nj + j))

    def br_w(k):
        return pl.BlockSpec((None, k, tn), lambda i, j: (layer, 0, j))

    vmem = 4 * tm * (d + dp + da + df) + 4 * tn * (3 * d + dp + da + df) + 12 * tm * tn * 4
    return pl.pallas_call(
        _merge_kernel,
        grid=(m // tm, nj),
        in_specs=[act(d), act(dp), act(da), act(df), gate_w(0), gate_w(1), gate_w(2),
                  gate_b(0), gate_b(1), gate_b(2), br_w(dp), br_w(da), br_w(df)],
        out_specs=pl.BlockSpec((tm, tn), lambda i, j: (i, j)),
        out_shape=jax.ShapeDtypeStruct((m, d), BF16),
        compiler_params=_params(2, vmem),
        name="merge",
    )(h, y_pool, y_attn, y_fnet, w_gate, w_gate, w_gate, b_gate, b_gate, b_gate, w_bp, w_ba, w_bf)


def _resid_kernel(a_ref, w_ref, x_ref, g_ref, o_ref):
    o_ref[...] = x_ref[...] + g_ref[...] * _dot(a_ref[...], w_ref[...])


def _resid_proj(a, w, layer, x, g, tm_pref):
    m, kk = a.shape
    n = w.shape[2]
    tm = min(m, tm_pref)
    tn = _tile(n, 512)
    return pl.pallas_call(
        _resid_kernel,
        grid=(m // tm, n // tn),
        in_specs=[
            pl.BlockSpec((tm, kk), lambda i, j: (i, 0)),
            pl.BlockSpec((None, kk, tn), lambda i, j: (layer, 0, j)),
            pl.BlockSpec((tm, tn), lambda i, j: (i, j)),
            pl.BlockSpec((1, tn), lambda i, j: (0, j)),
        ],
        out_specs=pl.BlockSpec((tm, tn), lambda i, j: (i, j)),
        out_shape=jax.ShapeDtypeStruct((m, n), F32),
        compiler_params=_params(2, 4 * tm * kk + 4 * kk * tn + 7 * tm * tn * 4),
        name="resid_proj",
    )(a, w, x, g)


def _ffn_up_kernel(*refs, n_cast):
    a_ref, halo_ref, wg_ref, wv_ref, cwg_ref, cwv_ref, cbg_ref, cbv_ref = refs[:8]
    o_ref = refs[8 + n_cast]
    a_ext = refs[-1]
    for src, dst in zip(refs[8:8 + n_cast], refs[9 + n_cast:9 + 2 * n_cast]):
        dst[...] = src[...].astype(dst.dtype)
    tm = a_ref.shape[0]

    @pl.when(pl.program_id(1) == 0)
    def _():
        a_ext[:tm, :] = a_ref[...]
        a_ext[tm:, :] = halo_ref[0]

    a = a_ext[...]
    row = lax.broadcasted_iota(jnp.int32, (tm, wg_ref.shape[1]), 0)

    def conv(w_ref, cw_ref, cb_ref):
        u = _dot(a, w_ref[...].astype(BF16))
        um = u[:tm]
        next_row = u[tm:tm + 1]
        prev_row = u[tm + 2 * SUBLANES_F32 - 1:tm + 2 * SUBLANES_F32]
        u_dn = jnp.where(row == 0, prev_row, pltpu.roll(um, 1, 0))
        u_up = jnp.where(row == tm - 1, next_row, pltpu.roll(um, tm - 1, 0))
        return u_dn * cw_ref[0:1] + um * cw_ref[1:2] + u_up * cw_ref[2:3] + cb_ref[...]

    gate = conv(wg_ref, cwg_ref, cbg_ref)
    val = conv(wv_ref, cwv_ref, cbv_ref)
    o_ref[...] = (gate * jax.nn.sigmoid(gate) * val).astype(o_ref.dtype)


def _ffn_up(h2, halo, layer, w_up, conv_w, conv_b, casts=()):
    m, d = h2.shape
    f = w_up.shape[2] // 2
    tm = m // halo.shape[0]
    ext = halo.shape[1]
    tn = _tile(f, 256)
    nj = f // tn
    w_bytes = jnp.dtype(w_up.dtype).itemsize
    a_bufs = 1 if w_bytes == 4 else 2
    vmem = (2 * a_bufs * tm * d + 2 * (tm + ext) * d + (4 * w_bytes + 4) * d * tn + 4 * tm * tn
            + 12 * (tm + ext) * tn * 4)

    def cols(rows, half):
        return pl.BlockSpec((None, rows, tn), lambda i, j: (layer, 0, half * nj + j))

    steps = (m // tm * nj, lambda i, j: i * nj + j)
    cast_specs = [_cast_spec(w.shape, layer, steps) for w in casts]
    vmem += sum(6 * 2 * s[1].block_shape[0] * s[1].block_shape[1] for s in cast_specs)
    out = pl.pallas_call(
        functools.partial(_ffn_up_kernel, n_cast=len(casts)),
        grid=(m // tm, nj),
        in_specs=[
            pl.BlockSpec((tm, d), lambda i, j: (i, 0), pipeline_mode=pl.Buffered(a_bufs)),
            pl.BlockSpec((1, ext, d), lambda i, j: (i, 0, 0)),
            cols(d, 0), cols(d, 1), cols(CONV_W, 0), cols(CONV_W, 1), cols(1, 0), cols(1, 1),
        ] + [s[0] for s in cast_specs],
        out_specs=[pl.BlockSpec((tm, tn), lambda i, j: (i, j))] + [s[1] for s in cast_specs],
        out_shape=[jax.ShapeDtypeStruct((m, f), BF16)] + [s[2] for s in cast_specs],
        scratch_shapes=[pltpu.VMEM((tm + ext, d), BF16)],
        compiler_params=_params(2, vmem),
        name="ffn_up",
    )(h2, halo, w_up, w_up, conv_w, conv_w, conv_b, conv_b, *casts)
    return out[0], out[1:]


def kernel(x, c, ctx, c_ctx, w_ada, b_ada, norm1_w, norm2_w, w_in, pool_w, pool_scale, q_norm_w, k_norm_w, rpb, fnet_w, w_gate, b_gate, w_br_pool, w_br_attn, w_br_fnet, w_o, w_up, conv_w, conv_b, w_down):
    n_layers, d, d_in = w_in.shape
    batch, s_len, _ = x.shape
    n_ctx = ctx.shape[1]
    assert batch == 1
    d_pool = w_br_pool.shape[1]
    d_attn = w_br_attn.shape[1]
    d_fnet = w_br_fnet.shape[1]
    n_heads = d_attn // HEAD_DIM
    assert d_in == d_pool + 3 * d_attn + d_fnet and s_len % GRID_W == 0 and d_fnet == d_pool
    q_lo, k_lo, v_lo, f_lo = d_pool, d_pool + d_attn, d_pool + 2 * d_attn, d_pool + 3 * d_attn

    pool_w_b = pool_w.astype(BF16)
    fnet_w_b = fnet_w.astype(BF16)
    b_gate_r = b_gate.reshape(n_layers, 1, -1)
    conv_b_r = conv_b.reshape(n_layers, 1, -1)
    ones_p = jnp.ones((d_pool,), F32)
    ones_vf = jnp.ones((d_attn + d_fnet,), F32)

    cs_chan = jnp.stack(_dft_tables(fnet_w.shape[-1])).astype(BF16)

    cond = jnp.concatenate([c.reshape(1, d), c_ctx.reshape(1, d), jnp.zeros((SUBLANES_F32 - 2, d), F32)], axis=0)
    mod = _ada(cond, w_ada, b_ada)

    tm_ffn = min(s_len, 1024)
    blk = lambda col: col // HEAD_DIM
    tn_in = _tile(d_pool, 512)
    assert d_pool % tn_in == 0 and d_attn % tn_in == 0
    qk_tiles = (q_lo // tn_in, v_lo // tn_in)

    xl = x.reshape(s_len, d)
    xc = ctx.reshape(n_ctx, d)
    for l in range(n_layers):
        last = l == n_layers - 1
        sh1, sc1, g1, sh2, sc2, g2 = [mod[l, 0:1, i * d:(i + 1) * d] for i in range(N_MOD)]
        csh1, csc1, cg1, csh2, csc2, cg2 = [mod[l, 1:2, i * d:(i + 1) * d] for i in range(N_MOD)]
        n1w = norm1_w[l].reshape(1, d)
        n2w = norm2_w[l].reshape(1, d)
        nw_tiles = jnp.concatenate(
            [ones_p, jnp.tile(q_norm_w[l], n_heads), jnp.tile(k_norm_w[l], n_heads), ones_vf]
        ).reshape(d_in // tn_in, 1, tn_in)
        bias = _nat_bias(rpb[l], s_len // GRID_W)
        pool_scale_l = pool_scale[l].reshape(1, -1)

        h = _norm_mod(xl, n1w, sh1, sc1)
        hc = _norm_mod(xc, n1w, csh1, csc1)
        p, w_bf16 = _in_proj(h, w_in, l, nw_tiles, tn_in, 0, d_in // tn_in, *qk_tiles,
                             casts=(w_gate, w_br_pool, w_br_attn, w_br_fnet, w_o))
        w_gate_b, w_bp_b, w_ba_b, w_bf_b, w_o_b = [w[None] for w in w_bf16]
        b_gate_l = b_gate_r[l:l + 1]
        if last:
            pc, _ = _in_proj(hc, w_in, l, nw_tiles, tn_in, k_lo // tn_in, (f_lo - k_lo) // tn_in, *qk_tiles)
            kc_blk, vc_blk = 0, blk(d_attn)
        else:
            pc, _ = _in_proj(hc, w_in, l, nw_tiles, tn_in, 0, d_in // tn_in, *qk_tiles)
            kc_blk, vc_blk = blk(k_lo), blk(v_lo)
        y_attn = _nat_attention(p, pc, bias, n_heads, blk(q_lo), blk(k_lo), blk(v_lo), kc_blk, vc_blk, ATTN_HEADS)
        y_pool = _pool_mix(p, 0, pool_w_b[l], pool_scale_l)
        y_fnet = _fourier_mix(p, f_lo // d_fnet, cs_chan, fnet_w_b[l])
        m = _merge(h, y_pool, y_attn, y_fnet, 0, w_gate_b, b_gate_l, w_bp_b, w_ba_b, w_bf_b)
        xl = _resid_proj(m, w_o_b, 0, xl, g1, 1024)

        h2 = _norm_mod(xl, n2w, sh2, sc2)
        halo = _halo_norm(xl, n2w, sh2, sc2, tm_ffn)
        t, (w_down_l,) = _ffn_up(h2, halo, l, w_up, conv_w, conv_b_r, casts=(w_down,))
        w_down_b = w_down_l[None]
        xl = _resid_proj(t, w_down_b, 0, xl, g2, 512)

        if not last:
            yc_attn = _ctx_attention(pc, n_heads, blk(q_lo), blk(k_lo), blk(v_lo))
            yc_pool = _pool_mix(pc, 0, pool_w_b[l], pool_scale_l)
            yc_fnet = _fourier_mix(pc, f_lo // d_fnet, cs_chan, fnet_w_b[l])
            mc = _merge(hc, yc_pool, yc_attn, yc_fnet, 0, w_gate_b, b_gate_l, w_bp_b, w_ba_b, w_bf_b)
            xc = _resid_proj(mc, w_o_b, 0, xc, cg1, 1024)
            hc2 = _norm_mod(xc, n2w, csh2, csc2)
            halo_c = _halo_norm(xc, n2w, csh2, csc2, n_ctx)
            tc, _ = _ffn_up(hc2, halo_c, l, w_up, conv_w, conv_b_r)
            xc = _resid_proj(tc, w_down_b, 0, xc, cg2, 512)
    return xl.reshape(batch, s_len, d)
```

```python
import functools
import math

import numpy as np
import jax
import jax.numpy as jnp
from jax import lax
from jax.experimental import pallas as pl
from jax.experimental.pallas import tpu as pltpu

F32 = jnp.float32
BF16 = jnp.bfloat16

GRID_W = 64
HEAD_DIM = 128
WIN_ROWS = 8
WIN_COLS = 16
POOL_WINDOWS = (2, 4, 8, 16)
N_GROUPS = 4
N_MOD = 6
CONV_W = 3
EPS = 1e-6

LANES = 128
SUBLANES_F32 = 8
SUBLANES_BF16 = 16
VMEM_CAP_BYTES = 56 * 1024 * 1024

ATTN_ROWS = 4
ATTN_KEY_ROWS = ATTN_ROWS + WIN_ROWS
ATTN_HEADS = 8
FFT_STAGE1_COLS = 16
EPILOGUE_ROWS = 256
NEG_BIAS = -1e30
FFT_MIN_LEN = 1024


def _tile(n, pref):
    if n <= pref:
        return n
    t = pref - pref % LANES
    while t >= LANES:
        if n % t == 0:
            return t
        t -= LANES
    return n


def _params(n_axes, vmem_bytes):
    return pltpu.CompilerParams(
        dimension_semantics=("arbitrary",) * n_axes,
        vmem_limit_bytes=int(min(VMEM_CAP_BYTES, max(vmem_bytes, 16 * 1024 * 1024))),
    )


def _dot(a, b):
    return jnp.dot(a, b, preferred_element_type=F32)


def _dot_nt(a, b):
    return lax.dot_general(a, b, (((1,), (1,)), ((), ())), preferred_element_type=F32)


def _rms_mod(x, nw, sh, sc):
    ms = jnp.mean(x * x, axis=-1, keepdims=True)
    return (x * lax.rsqrt(ms + EPS) * nw) * (1.0 + sc) + sh


def _ada_kernel(s_ref, w_ref, b_ref, o_ref):
    s = s_ref[...]
    a = (s * jax.nn.sigmoid(s)).astype(BF16)
    o_ref[0] = _dot(a, w_ref[0].astype(BF16)) + b_ref[0]


def _ada(cond, w_ada, b_ada):
    n_layers, d, n = w_ada.shape
    tn = _tile(n, 1024)
    return pl.pallas_call(
        _ada_kernel,
        grid=(n_layers, n // tn),
        in_specs=[
            pl.BlockSpec((SUBLANES_F32, d), lambda l, j: (0, 0)),
            pl.BlockSpec((1, d, tn), lambda l, j: (l, 0, j)),
            pl.BlockSpec((1, 1, tn), lambda l, j: (l, 0, j)),
        ],
        out_specs=pl.BlockSpec((1, SUBLANES_F32, tn), lambda l, j: (l, 0, j)),
        out_shape=jax.ShapeDtypeStruct((n_layers, SUBLANES_F32, n), F32),
        compiler_params=_params(2, 2 * d * tn * 4 + 8 * 1024 * 1024),
        name="ada",
    )(cond, w_ada, b_ada.reshape(n_layers, 1, n))


def _norm_kernel(x_ref, nw_ref, sh_ref, sc_ref, o_ref):
    o_ref[...] = _rms_mod(x_ref[...], nw_ref[...], sh_ref[...], sc_ref[...]).astype(o_ref.dtype)


def _norm_mod(x, nw, sh, sc):
    m, d = x.shape
    tm = min(m, 512)
    vec = pl.BlockSpec((1, d), lambda i: (0, 0))
    return pl.pallas_call(
        _norm_kernel,
        grid=(m // tm,),
        in_specs=[pl.BlockSpec((tm, d), lambda i: (i, 0)), vec, vec, vec],
        out_specs=pl.BlockSpec((tm, d), lambda i: (i, 0)),
        out_shape=jax.ShapeDtypeStruct((m, d), BF16),
        compiler_params=_params(1, 6 * tm * d * 4),
        name="norm_mod",
    )(x, nw, sh, sc)


def _halo_kernel(xp_ref, xn_ref, nw_ref, sh_ref, sc_ref, o_ref):
    i = pl.program_id(0)
    n = pl.num_programs(0)
    hp = _rms_mod(xp_ref[...], nw_ref[...], sh_ref[...], sc_ref[...])
    hn = _rms_mod(xn_ref[...], nw_ref[...], sh_ref[...], sc_ref[...])
    hp = jnp.where(i > 0, hp, 0.0)
    hn = jnp.where(i < n - 1, hn, 0.0)
    o_ref[0] = jnp.concatenate([hn, hp], axis=0).astype(o_ref.dtype)


def _halo_norm(x, nw, sh, sc, tm):
    m, d = x.shape
    nblk = m // tm
    r = SUBLANES_F32
    per = tm // r
    last = m // r - 1
    vec = pl.BlockSpec((1, d), lambda i: (0, 0))
    return pl.pallas_call(
        _halo_kernel,
        grid=(nblk,),
        in_specs=[
            pl.BlockSpec((r, d), lambda i: (jnp.maximum(i * per - 1, 0), 0)),
            pl.BlockSpec((r, d), lambda i: (jnp.minimum((i + 1) * per, last), 0)),
            vec, vec, vec,
        ],
        out_specs=pl.BlockSpec((1, 2 * r, d), lambda i: (i, 0, 0)),
        out_shape=jax.ShapeDtypeStruct((nblk, 2 * r, d), BF16),
        compiler_params=_params(1, 16 * 1024 * 1024),
        name="halo_norm",
    )(x, x, nw, sh, sc)


def _cast_spec(shape, layer, block_of_step):
    _, rows, cols = shape
    n_steps, step_of = block_of_step
    rb = max(SUBLANES_BF16, rows // n_steps // SUBLANES_BF16 * SUBLANES_BF16)
    while rows % rb:
        rb += SUBLANES_BF16
    n_blk = rows // rb

    def blk(i, j):
        return step_of(i, j) * n_blk // n_steps

    return (pl.BlockSpec((None, rb, cols), lambda i, j: (layer, blk(i, j), 0)),
            pl.BlockSpec((rb, cols), lambda i, j: (blk(i, j), 0)),
            jax.ShapeDtypeStruct((rows, cols), BF16))


def _inproj_kernel(*refs, j_off, qk_lo, qk_hi, n_cast):
    a_ref, b_ref, nw_ref = refs[:3]
    o_ref = refs[3 + n_cast]
    acc0, acc1 = refs[4 + 2 * n_cast:]
    for src, dst in zip(refs[3:3 + n_cast], refs[4 + n_cast:4 + 2 * n_cast]):
        dst[...] = src[...].astype(dst.dtype)
    tm = a_ref.shape[0]
    j = pl.program_id(1)
    nj = pl.num_programs(1) - 1
    tile = j - 1 + j_off
    is_qk = jnp.logical_and(tile >= qk_lo, tile < qk_hi)
    rc = min(tm, EPILOGUE_ROWS)

    def matmul(acc):
        acc[...] = _dot(a_ref[...], b_ref[...].astype(BF16))

    def epilogue(acc):
        for r0 in range(0, tm, rc):
            for h in range(o_ref.shape[1] // HEAD_DIM):
                sl = slice(h * HEAD_DIM, (h + 1) * HEAD_DIM)
                xh = acc[r0:r0 + rc, sl]
                ms = jnp.mean(xh * xh, axis=-1, keepdims=True)
                normed = xh * lax.rsqrt(ms + EPS) * nw_ref[0, :, sl]
                o_ref[r0:r0 + rc, sl] = jnp.where(is_qk, normed, xh).astype(o_ref.dtype)

    odd = lax.rem(j, 2) == 1
    mid = jnp.logical_and(j > 0, j < nj)

    @pl.when(j == 0)
    def _():
        matmul(acc0)

    @pl.when(jnp.logical_and(mid, odd))
    def _():
        epilogue(acc0)
        matmul(acc1)

    @pl.when(jnp.logical_and(mid, jnp.logical_not(odd)))
    def _():
        epilogue(acc1)
        matmul(acc0)

    @pl.when(jnp.logical_and(j == nj, odd))
    def _():
        epilogue(acc0)

    @pl.when(jnp.logical_and(j == nj, jnp.logical_not(odd)))
    def _():
        epilogue(acc1)


def _in_proj(h, w_in, layer, nw_tiles, tn, j_off, n_tiles, qk_lo, qk_hi, casts=()):
    m, d = h.shape
    tm = min(m, 1024)
    acc = pltpu.VMEM((tm, tn), F32)
    steps = (m // tm * n_tiles, lambda i, j: i * n_tiles + jnp.minimum(j, n_tiles - 1))
    cast_specs = [_cast_spec(w.shape, layer, steps) for w in casts]
    cast_bytes = sum(6 * 2 * s[1].block_shape[0] * s[1].block_shape[1] for s in cast_specs)
    out = pl.pallas_call(
        functools.partial(_inproj_kernel, j_off=j_off, qk_lo=qk_lo, qk_hi=qk_hi, n_cast=len(casts)),
        grid=(m // tm, n_tiles + 1),
        in_specs=[
            pl.BlockSpec((tm, d), lambda i, j: (i, 0)),
            pl.BlockSpec((None, d, tn), lambda i, j: (layer, 0, jnp.minimum(j, n_tiles - 1) + j_off)),
            pl.BlockSpec((1, 1, tn), lambda i, j: (jnp.maximum(j - 1, 0) + j_off, 0, 0)),
        ] + [s[0] for s in cast_specs],
        out_specs=[pl.BlockSpec((tm, tn), lambda i, j: (i, jnp.maximum(j - 1, 0)))] + [s[1] for s in cast_specs],
        out_shape=[jax.ShapeDtypeStruct((m, n_tiles * tn), BF16)] + [s[2] for s in cast_specs],
        scratch_shapes=[acc, acc],
        compiler_params=_params(2, 4 * tm * d + 10 * d * tn + 4 * tm * tn + 6 * tm * tn * 4 + cast_bytes),
        name="in_proj",
    )(h, w_in, nw_tiles, *casts)
    return out[0], out[1:]


def _nat_kernel(q_ref, k_ref, v_ref, kc_ref, vc_ref, bias_ref, o_ref, *, grid_rows, scale, heads):
    b = pl.program_id(1)
    ks = jnp.clip(b * ATTN_ROWS - WIN_ROWS // 2, 0, grid_rows - ATTN_KEY_ROWS)
    start = pl.multiple_of(ks * GRID_W, GRID_W)
    nkeys = ATTN_KEY_ROWS * GRID_W
    for h in range(heads):
        sl = slice(h * HEAD_DIM, (h + 1) * HEAD_DIM)
        q = q_ref[:, sl]
        kw = k_ref[pl.ds(start, nkeys), sl]
        vw = v_ref[pl.ds(start, nkeys), sl]
        s = _dot_nt(q, kw) * scale + bias_ref[0, h]
        sc = _dot_nt(q, kc_ref[:, sl]) * scale
        m = jnp.maximum(jnp.max(s, axis=-1, keepdims=True), jnp.max(sc, axis=-1, keepdims=True))
        p = jnp.exp(s - m)
        pc = jnp.exp(sc - m)
        l = jnp.sum(p, axis=-1, keepdims=True) + jnp.sum(pc, axis=-1, keepdims=True)
        o = _dot(p.astype(BF16), vw) + _dot(pc.astype(BF16), vc_ref[:, sl])
        o_ref[:, sl] = (o / l).astype(o_ref.dtype)


def _nat_bias(rpb, grid_rows):
    n_heads = rpb.shape[0]
    nb = grid_rows // ATTN_ROWS
    c = np.arange(GRID_W)[:, None]
    kc = np.arange(GRID_W)[None, :]
    c_start = np.clip(c - WIN_COLS // 2, 0, GRID_W - WIN_COLS)
    col_ok = (kc >= c_start) & (kc < c_start + WIN_COLS)
    dc = kc - c + (WIN_COLS - 1)
    n_dc = 2 * WIN_COLS - 1
    expand = ((dc[None] == np.arange(n_dc)[:, None, None]) & col_ok[None]).astype(np.float32)
    blocks = jnp.einsum("hrd,dn->hrn", rpb, expand.reshape(n_dc, GRID_W * GRID_W),
                        precision=lax.Precision.HIGHEST)
    blocks = blocks.reshape(n_heads, 2 * WIN_ROWS - 1, GRID_W, GRID_W)
    blocks = jnp.where(col_ok, blocks, NEG_BIAS)
    masked = jnp.full((n_heads, GRID_W, GRID_W), NEG_BIAS, F32)
    variants = []
    for b in (0, 1, nb - 1):
        ks = int(np.clip(b * ATTN_ROWS - WIN_ROWS // 2, 0, grid_rows - ATTN_KEY_ROWS))
        q_rows = []
        for r_loc in range(ATTN_ROWS):
            r = b * ATTN_ROWS + r_loc
            r_start = int(np.clip(r - WIN_ROWS // 2, 0, grid_rows - WIN_ROWS))
            row = []
            for kr in range(ks, ks + ATTN_KEY_ROWS):
                ok = r_start <= kr < r_start + WIN_ROWS
                row.append(blocks[:, kr - r + (WIN_ROWS - 1)] if ok else masked)
            q_rows.append(jnp.concatenate(row, axis=-1))
        variants.append(jnp.concatenate(q_rows, axis=-2))
    return jnp.stack(variants)


def _nat_attention(p_lat, p_ctx, bias, layer, n_heads, q_blk, k_blk, v_blk, kc_blk, vc_blk, hs):
    s_len = p_lat.shape[0]
    n_ctx = p_ctx.shape[0]
    grid_rows = s_len // GRID_W
    nb = grid_rows // ATTN_ROWS
    nq = ATTN_ROWS * GRID_W
    nk = ATTN_KEY_ROWS * GRID_W
    assert grid_rows % ATTN_ROWS == 0 and nb >= 3 and grid_rows >= ATTN_KEY_ROWS
    while n_heads % hs or any(blk % hs for blk in (q_blk, k_blk, v_blk, kc_blk, vc_blk)):
        hs //= 2
    width = hs * HEAD_DIM

    def variant(b):
        return jnp.where(b == 0, 0, jnp.where(b == nb - 1, 2, 1))

    def cols(rows, blk):
        mode = pl.Buffered(1 if (rows == s_len and hs > 2) else 2)
        return pl.BlockSpec((rows, width), lambda h, b: (0, blk // hs + h), pipeline_mode=mode)

    return pl.pallas_call(
        functools.partial(_nat_kernel, grid_rows=grid_rows, scale=HEAD_DIM ** -0.5, heads=hs),
        grid=(n_heads // hs, nb),
        in_specs=[
            pl.BlockSpec((nq, width), lambda h, b: (b, q_blk // hs + h)),
            cols(s_len, k_blk), cols(s_len, v_blk), cols(n_ctx, kc_blk), cols(n_ctx, vc_blk),
            pl.BlockSpec((1, hs, nq, nk), lambda h, b: (variant(b), layer * (n_heads // hs) + h, 0, 0)),
        ],
        out_specs=pl.BlockSpec((nq, width), lambda h, b: (b, h)),
        out_shape=jax.ShapeDtypeStruct((s_len, n_heads * HEAD_DIM), BF16),
        compiler_params=_params(2, 8 * s_len * width + hs * (2 * nq * nk * 4 + 16 * nq * (nk + n_ctx) * 4)),
        name="nat_attention",
    )(p_lat, p_lat, p_lat, p_ctx, p_ctx, bias)


def _ctx_attn_kernel(q_ref, k_ref, v_ref, o_ref, *, scale):
    s = _dot_nt(q_ref[...], k_ref[...]) * scale
    m = jnp.max(s, axis=-1, keepdims=True)
    p = jnp.exp(s - m)
    l = jnp.sum(p, axis=-1, keepdims=True)
    o_ref[...] = (_dot(p.astype(BF16), v_ref[...]) / l).astype(o_ref.dtype)


def _ctx_attention(p_ctx, n_heads, q_blk, k_blk, v_blk):
    n = p_ctx.shape[0]
    return pl.pallas_call(
        functools.partial(_ctx_attn_kernel, scale=HEAD_DIM ** -0.5),
        grid=(n_heads,),
        in_specs=[
            pl.BlockSpec((n, HEAD_DIM), lambda h: (0, q_blk + h)),
            pl.BlockSpec((n, HEAD_DIM), lambda h: (0, k_blk + h)),
            pl.BlockSpec((n, HEAD_DIM), lambda h: (0, v_blk + h)),
        ],
        out_specs=pl.BlockSpec((n, HEAD_DIM), lambda h: (0, h)),
        out_shape=jax.ShapeDtypeStruct((n, n_heads * HEAD_DIM), BF16),
        compiler_params=_params(1, 16 * 1024 * 1024),
        name="ctx_attention",
    )(p_ctx, p_ctx, p_ctx)


def _pool_kernel(up_ref, um_ref, un_ref, pw_ref, ps_ref, o_ref, *, seq_len, gc):
    i = pl.program_id(0)
    n = pl.num_programs(0)
    tp = um_ref.shape[0]
    halo = up_ref.shape[0]
    um = um_ref[...]
    up = jnp.where(i > 0, up_ref[...], jnp.zeros_like(up_ref[...]))
    un = jnp.where(i < n - 1, un_ref[...], jnp.zeros_like(un_ref[...]))
    ue = jnp.concatenate([up, um, un], axis=0)
    trow = lax.broadcasted_iota(jnp.int32, (tp, tp + 2 * halo), 0)
    jcol = lax.broadcasted_iota(jnp.int32, (tp, tp + 2 * halo), 1)
    off = jcol - trow - halo
    t_abs = i * tp + lax.broadcasted_iota(jnp.int32, (tp, gc), 0)
    for g, w in enumerate(POOL_WINDOWS):
        lo_off = -(w // 2)
        hi_off = w - 1 - w // 2
        band = jnp.where(off >= lo_off, jnp.where(off <= hi_off, 1.0, 0.0), 0.0).astype(BF16)
        sl = slice(g * gc, (g + 1) * gc)
        wsum = _dot(band, ue[:, sl])
        lo = jnp.maximum(t_abs + lo_off, 0)
        hi = jnp.minimum(t_abs + hi_off, seq_len - 1)
        cnt = (hi - lo + 1).astype(F32)
        d = (wsum / cnt - um[:, sl].astype(F32)).astype(BF16)
        o_ref[:, sl] = (_dot(d, pw_ref[g]) * ps_ref[:, sl]).astype(o_ref.dtype)


def _pool_mix(p, col_blk, pool_w, pool_scale):
    m = p.shape[0]
    gc = pool_w.shape[-1]
    width = N_GROUPS * gc
    tp = min(m, 512)
    halo = SUBLANES_BF16
    per = tp // halo
    last = m // halo - 1
    return pl.pallas_call(
        functools.partial(_pool_kernel, seq_len=m, gc=gc),
        grid=(m // tp,),
        in_specs=[
            pl.BlockSpec((halo, width), lambda i: (jnp.maximum(i * per - 1, 0), col_blk)),
            pl.BlockSpec((tp, width), lambda i: (i, col_blk)),
            pl.BlockSpec((halo, width), lambda i: (jnp.minimum((i + 1) * per, last), col_blk)),
            pl.BlockSpec((N_GROUPS, gc, gc), lambda i: (0, 0, 0)),
            pl.BlockSpec((1, width), lambda i: (0, 0)),
        ],
        out_specs=pl.BlockSpec((tp, width), lambda i: (i, 0)),
        out_shape=jax.ShapeDtypeStruct((m, width), BF16),
        compiler_params=_params(1, 32 * 1024 * 1024),
        name="pool_mix",
    )(p, p, p, pool_w, pool_scale)


def _cos_sin(num, den):
    ang = (num % den).astype(F32) * (2.0 * math.pi / den)
    return jnp.cos(ang), jnp.sin(ang)


def _dft_tables(n):
    idx = jnp.arange(n, dtype=jnp.int32)
    return _cos_sin(idx[:, None] * idx[None, :], n)


def _fnet_chan_kernel(u_ref, cs_ref, o_ref, *, gc):
    for g in range(N_GROUPS):
        sl = slice(g * gc, (g + 1) * gc)
        o_ref[:, sl] = _dot(u_ref[:, sl], cs_ref[0]).astype(o_ref.dtype)


def _fnet_pos_kernel(t_ref, ab_ref, fw_ref, o_ref, *, scale, gc):
    acc = _dot(t_ref[...], ab_ref[...])
    for g in range(N_GROUPS):
        sl = slice(g * gc, (g + 1) * gc)
        f = (acc[:, sl] * scale).astype(BF16)
        o_ref[:, sl] = _dot(f, fw_ref[g]).astype(o_ref.dtype)


def _fourier_mix_dense(p, col_blk, cs_chan, fnet_w):
    m = p.shape[0]
    gc = fnet_w.shape[-1]
    width = N_GROUPS * gc
    cpos, spos = _dft_tables(m)
    t_pos = jnp.concatenate([cpos, -spos], axis=1).astype(BF16)
    ab = pl.pallas_call(
        functools.partial(_fnet_chan_kernel, gc=gc),
        grid=(2,),
        in_specs=[
            pl.BlockSpec((m, width), lambda c: (0, col_blk)),
            pl.BlockSpec((1, gc, gc), lambda c: (c, 0, 0)),
        ],
        out_specs=pl.BlockSpec((m, width), lambda c: (c, 0)),
        out_shape=jax.ShapeDtypeStruct((2 * m, width), BF16),
        compiler_params=_params(1, 32 * 1024 * 1024),
        name="fnet_chan",
    )(p, cs_chan)
    return pl.pallas_call(
        functools.partial(_fnet_pos_kernel, scale=1.0 / math.sqrt(m * gc), gc=gc),
        grid=(1,),
        in_specs=[
            pl.BlockSpec((m, 2 * m), lambda i: (0, 0)),
            pl.BlockSpec((2 * m, width), lambda i: (0, 0)),
            pl.BlockSpec((N_GROUPS, gc, gc), lambda i: (0, 0, 0)),
        ],
        out_specs=pl.BlockSpec((m, width), lambda i: (0, 0)),
        out_shape=jax.ShapeDtypeStruct((m, width), BF16),
        compiler_params=_params(1, 32 * 1024 * 1024),
        name="fnet_pos",
    )(t_pos, ab, fnet_w)


def _fft_stage1_kernel(f1_ref, u_ref, tc_ref, ts_ref, zr_ref, zi_ref, *, n1):
    width = zr_ref.shape[1]
    reps = width // LANES
    for t in range(u_ref.shape[1] // width):
        y = _dot(f1_ref[...], u_ref[:, t * width:(t + 1) * width])
        yr, yi = y[:n1], y[n1:]
        tc = jnp.tile(tc_ref[:, t * LANES:(t + 1) * LANES], (1, reps))
        ts = jnp.tile(ts_ref[:, t * LANES:(t + 1) * LANES], (1, reps))
        zr_ref[t * n1:(t + 1) * n1, :] = (yr * tc + yi * ts).astype(zr_ref.dtype)
        zi_ref[t * n1:(t + 1) * n1, :] = (yi * tc - yr * ts).astype(zi_ref.dtype)


def _fft_stage2_kernel(fa_ref, fb_ref, zr_ref, zi_ref, cc_ref, sc_ref, fw_ref, o_ref, *, n2, gc, chunks, scale):
    width = N_GROUPS * gc
    p = _dot(fa_ref[...], zr_ref[...]) + _dot(fb_ref[...], zi_ref[...])
    for g in range(N_GROUPS):
        cols = [slice(ch * width + g * gc, ch * width + (g + 1) * gc) for ch in range(chunks)]
        pr = jnp.concatenate([p[:n2, c] for c in cols], axis=0).astype(BF16)
        pi = jnp.concatenate([p[n2:, c] for c in cols], axis=0).astype(BF16)
        f = ((_dot(pr, cc_ref[...]) + _dot(pi, sc_ref[...])) * scale).astype(BF16)
        y = _dot(f, fw_ref[g]).astype(o_ref.dtype)
        for ch, c in enumerate(cols):
            o_ref[:, c] = y[ch * n2:(ch + 1) * n2]


def _fourier_mix_fft(p, col_blk, cs_chan, fnet_w):
    m, d_in = p.shape
    gc = fnet_w.shape[-1]
    width = N_GROUPS * gc
    n2 = LANES
    n1 = m // n2
    i1 = jnp.arange(n1, dtype=jnp.int32)
    i2 = jnp.arange(n2, dtype=jnp.int32)
    c1, s1 = _cos_sin(i1[:, None] * i1[None, :], n1)
    f1 = jnp.concatenate([c1, -s1], axis=0).astype(BF16)
    tc, ts = _cos_sin(i1[:, None] * i2[None, :], m)
    tc = jnp.repeat(tc, LANES, axis=1)
    ts = jnp.repeat(ts, LANES, axis=1)
    c2, s2 = _cos_sin(i2[:, None] * i2[None, :], n2)
    fa = jnp.concatenate([c2, -s2], axis=0).astype(BF16)
    fb = jnp.concatenate([s2, c2], axis=0).astype(BF16)
    u = p[:, col_blk * width:(col_blk + 1) * width].reshape(n1, n2 * width)
    z_shape = jax.ShapeDtypeStruct((n2 * n1, width), BF16)
    sb = FFT_STAGE1_COLS
    zr, zi = pl.pallas_call(
        functools.partial(_fft_stage1_kernel, n1=n1),
        grid=(n2 // sb,),
        in_specs=[
            pl.BlockSpec((2 * n1, n1), lambda j: (0, 0)),
            pl.BlockSpec((n1, sb * width), lambda j: (0, j)),
            pl.BlockSpec((n1, sb * LANES), lambda j: (0, j)),
            pl.BlockSpec((n1, sb * LANES), lambda j: (0, j)),
        ],
        out_specs=[pl.BlockSpec((sb * n1, width), lambda j: (j, 0))] * 2,
        out_shape=[z_shape, z_shape],
        compiler_params=_params(1, 16 * 1024 * 1024),
        name="fft_stage1",
    )(f1, u, tc, ts)
    chunks = min(n1, 8)
    tn = chunks * width
    y = pl.pallas_call(
        functools.partial(_fft_stage2_kernel, n2=n2, gc=gc, chunks=chunks, scale=1.0 / math.sqrt(m * gc)),
        grid=(n1 // chunks,),
        in_specs=[
            pl.BlockSpec((2 * n2, n2), lambda j: (0, 0)),
            pl.BlockSpec((2 * n2, n2), lambda j: (0, 0)),
            pl.BlockSpec((n2, tn), lambda j: (0, j)),
            pl.BlockSpec((n2, tn), lambda j: (0, j)),
            pl.BlockSpec((None, gc, gc), lambda j: (0, 0, 0)),
            pl.BlockSpec((None, gc, gc), lambda j: (1, 0, 0)),
            pl.BlockSpec((N_GROUPS, gc, gc), lambda j: (0, 0, 0)),
        ],
        out_specs=pl.BlockSpec((n2, tn), lambda j: (0, j)),
        out_shape=jax.ShapeDtypeStruct((n2, n1 * width), BF16),
        compiler_params=_params(1, 40 * 1024 * 1024),
        name="fft_stage2",
    )(fa, fb, zr.reshape(n2, n1 * width), zi.reshape(n2, n1 * width), cs_chan, cs_chan, fnet_w)
    return y.reshape(m, width)


def _fourier_mix(p, col_blk, cs_chan, fnet_w):
    if p.shape[0] >= FFT_MIN_LEN and p.shape[0] % (LANES * SUBLANES_F32) == 0:
        return _fourier_mix_fft(p, col_blk, cs_chan, fnet_w)
    return _fourier_mix_dense(p, col_blk, cs_chan, fnet_w)


def _merge_kernel(h_ref, yp_ref, ya_ref, yf_ref, wg0, wg1, wg2, bg0, bg1, bg2, wp, wa, wf, o_ref):
    h = h_ref[...]

    def gate(w_ref, b_ref):
        return jax.nn.sigmoid(_dot(h, w_ref[...]) + b_ref[...])

    m = gate(wg0, bg0) * _dot(yp_ref[...], wp[...])
    m = m + gate(wg1, bg1) * _dot(ya_ref[...], wa[...])
    m = m + gate(wg2, bg2) * _dot(yf_ref[...], wf[...])
    o_ref[...] = m.astype(o_ref.dtype)


def _merge(h, y_pool, y_attn, y_fnet, layer, w_gate, b_gate, w_bp, w_ba, w_bf):
    m, d = h.shape
    dp, da, df = y_pool.shape[1], y_attn.shape[1], y_fnet.shape[1]
    tm = min(m, 1024)
    tn = _tile(d, 256)
    nj = d // tn

    def act(width):
        bufs = 1 if width == d else 2
        return pl.BlockSpec((tm, width), lambda i, j: (i, 0), pipeline_mode=pl.Buffered(bufs))

    def gate_w(b):
        return pl.BlockSpec((None, d, tn), lambda i, j: (layer, 0, b * nj + j))

    def gate_b(b):
        return pl.BlockSpec((None, 1, tn), lambda i, j: (layer, 0, b * nj + j))

    def br_w(k):
        return pl.BlockSpec((None, k, tn), lambda i, j: (layer, 0, j))

    vmem = 4 * tm * (d + dp + da + df) + 4 * tn * (3 * d + dp + da + df) + 12 * tm * tn * 4
    return pl.pallas_call(
        _merge_kernel,
        grid=(m // tm, nj),
        in_specs=[act(d), act(dp), act(da), act(df), gate_w(0), gate_w(1), gate_w(2),
                  gate_b(0), gate_b(1), gate_b(2), br_w(dp), br_w(da), br_w(df)],
        out_specs=pl.BlockSpec((tm, tn), lambda i, j: (i, j)),
        out_shape=jax.ShapeDtypeStruct((m, d), BF16),
        compiler_params=_params(2, vmem),
        name="merge",
    )(h, y_pool, y_attn, y_fnet, w_gate, w_gate, w_gate, b_gate, b_gate, b_gate, w_bp, w_ba, w_bf)


def _resid_kernel(a_ref, w_ref, x_ref, g_ref, o_ref):
    o_ref[...] = x_ref[...] + g_ref[...] * _dot(a_ref[...], w_ref[...])


def _resid_proj(a, w, layer, x, g, tm_pref, tn_pref):
    m, kk = a.shape
    n = w.shape[2]
    tm = min(m, tm_pref)
    tn = _tile(n, tn_pref)
    return pl.pallas_call(
        _resid_kernel,
        grid=(m // tm, n // tn),
        in_specs=[
            pl.BlockSpec((tm, kk), lambda i, j: (i, 0)),
            pl.BlockSpec((None, kk, tn), lambda i, j: (layer, 0, j)),
            pl.BlockSpec((tm, tn), lambda i, j: (i, j)),
            pl.BlockSpec((1, tn), lambda i, j: (0, j)),
        ],
        out_specs=pl.BlockSpec((tm, tn), lambda i, j: (i, j)),
        out_shape=jax.ShapeDtypeStruct((m, n), F32),
        compiler_params=_params(2, 4 * tm * kk + 4 * kk * tn + 7 * tm * tn * 4),
        name="resid_proj",
    )(a, w, x, g)


def _ffn_up_kernel(*refs, n_cast):
    a_ref, halo_ref, wg_ref, wv_ref, cwg_ref, cwv_ref, cbg_ref, cbv_ref = refs[:8]
    o_ref = refs[8 + n_cast]
    a_ext = refs[-1]
    for src, dst in zip(refs[8:8 + n_cast], refs[9 + n_cast:9 + 2 * n_cast]):
        dst[...] = src[...].astype(dst.dtype)
    tm = a_ref.shape[0]

    @pl.when(pl.program_id(1) == 0)
    def _():
        a_ext[:tm, :] = a_ref[...]
        a_ext[tm:, :] = halo_ref[0]

    a = a_ext[...]
    row = lax.broadcasted_iota(jnp.int32, (tm, wg_ref.shape[1]), 0)

    def conv(w_ref, cw_ref, cb_ref):
        u = _dot(a, w_ref[...].astype(BF16))
        um = u[:tm]
        next_row = u[tm:tm + 1]
        prev_row = u[tm + 2 * SUBLANES_F32 - 1:tm + 2 * SUBLANES_F32]
        u_dn = jnp.where(row == 0, prev_row, pltpu.roll(um, 1, 0))
        u_up = jnp.where(row == tm - 1, next_row, pltpu.roll(um, tm - 1, 0))
        return u_dn * cw_ref[0:1] + um * cw_ref[1:2] + u_up * cw_ref[2:3] + cb_ref[...]

    gate = conv(wg_ref, cwg_ref, cbg_ref)
    val = conv(wv_ref, cwv_ref, cbv_ref)
    o_ref[...] = (gate * jax.nn.sigmoid(gate) * val).astype(o_ref.dtype)


def _ffn_up(h2, halo, layer, w_up, conv_w, conv_b, casts=()):
    m, d = h2.shape
    f = w_up.shape[2] // 2
    tm = m // halo.shape[0]
    ext = halo.shape[1]
    tn = _tile(f, 256)
    nj = f // tn
    w_bytes = jnp.dtype(w_up.dtype).itemsize
    a_bufs = 1 if w_bytes == 4 else 2
    vmem = (2 * a_bufs * tm * d + 2 * (tm + ext) * d + (4 * w_bytes + 4) * d * tn + 4 * tm * tn
            + 12 * (tm + ext) * tn * 4)

    def cols(rows, half):
        return pl.BlockSpec((None, rows, tn), lambda i, j: (layer, 0, half * nj + j))

    steps = (m // tm * nj, lambda i, j: i * nj + j)
    cast_specs = [_cast_spec(w.shape, layer, steps) for w in casts]
    vmem += sum(6 * 2 * s[1].block_shape[0] * s[1].block_shape[1] for s in cast_specs)
    out = pl.pallas_call(
        functools.partial(_ffn_up_kernel, n_cast=len(casts)),
        grid=(m // tm, nj),
        in_specs=[
            pl.BlockSpec((tm, d), lambda i, j: (i, 0), pipeline_mode=pl.Buffered(a_bufs)),
            pl.BlockSpec((1, ext, d), lambda i, j: (i, 0, 0)),
            cols(d, 0), cols(d, 1), cols(CONV_W, 0), cols(CONV_W, 1), cols(1, 0), cols(1, 1),
        ] + [s[0] for s in cast_specs],
        out_specs=[pl.BlockSpec((tm, tn), lambda i, j: (i, j))] + [s[1] for s in cast_specs],
        out_shape=[jax.ShapeDtypeStruct((m, f), BF16)] + [s[2] for s in cast_specs],
        scratch_shapes=[pltpu.VMEM((tm + ext, d), BF16)],
        compiler_params=_params(2, vmem),
        name="ffn_up",
    )(h2, halo, w_up, w_up, conv_w, conv_w, conv_b, conv_b, *casts)
    return out[0], out[1:]


def kernel(x, c, ctx, c_ctx, w_ada, b_ada, norm1_w, norm2_w, w_in, pool_w, pool_scale, q_norm_w, k_norm_w, rpb, fnet_w, w_gate, b_gate, w_br_pool, w_br_attn, w_br_fnet, w_o, w_up, conv_w, conv_b, w_down):
    n_layers, d, d_in = w_in.shape
    batch, s_len, _ = x.shape
    n_ctx = ctx.shape[1]
    assert batch == 1
    d_pool = w_br_pool.shape[1]
    d_attn = w_br_attn.shape[1]
    d_fnet = w_br_fnet.shape[1]
    n_heads = d_attn // HEAD_DIM
    assert d_in == d_pool + 3 * d_attn + d_fnet and s_len % GRID_W == 0 and d_fnet == d_pool
    q_lo, k_lo, v_lo, f_lo = d_pool, d_pool + d_attn, d_pool + 2 * d_attn, d_pool + 3 * d_attn

    pool_w_b = pool_w.astype(BF16)
    fnet_w_b = fnet_w.astype(BF16)
    b_gate_r = b_gate.reshape(n_layers, 1, -1)
    conv_b_r = conv_b.reshape(n_layers, 1, -1)
    ones_p = jnp.ones((d_pool,), F32)
    ones_vf = jnp.ones((d_attn + d_fnet,), F32)

    cs_chan = jnp.stack(_dft_tables(fnet_w.shape[-1])).astype(BF16)

    cond = jnp.concatenate([c.reshape(1, d), c_ctx.reshape(1, d), jnp.zeros((SUBLANES_F32 - 2, d), F32)], axis=0)
    mod = _ada(cond, w_ada, b_ada)

    bias = _nat_bias(rpb.reshape((n_layers * n_heads,) + rpb.shape[2:]), s_len // GRID_W)

    tm_ffn = min(s_len, 1024)
    blk = lambda col: col // HEAD_DIM
    tn_in = _tile(d_pool, 512)
    assert d_pool % tn_in == 0 and d_attn % tn_in == 0
    qk_tiles = (q_lo // tn_in, v_lo // tn_in)

    xl = x.reshape(s_len, d)
    xc = ctx.reshape(n_ctx, d)
    for l in range(n_layers):
        last = l == n_layers - 1
        sh1, sc1, g1, sh2, sc2, g2 = [mod[l, 0:1, i * d:(i + 1) * d] for i in range(N_MOD)]
        csh1, csc1, cg1, csh2, csc2, cg2 = [mod[l, 1:2, i * d:(i + 1) * d] for i in range(N_MOD)]
        n1w = norm1_w[l].reshape(1, d)
        n2w = norm2_w[l].reshape(1, d)
        nw_tiles = jnp.concatenate(
            [ones_p, jnp.tile(q_norm_w[l], n_heads), jnp.tile(k_norm_w[l], n_heads), ones_vf]
        ).reshape(d_in // tn_in, 1, tn_in)
        pool_scale_l = pool_scale[l].reshape(1, -1)

        h = _norm_mod(xl, n1w, sh1, sc1)
        hc = _norm_mod(xc, n1w, csh1, csc1)
        p, w_bf16 = _in_proj(h, w_in, l, nw_tiles, tn_in, 0, d_in // tn_in, *qk_tiles,
                             casts=(w_gate, w_br_pool, w_br_attn, w_br_fnet, w_o))
        w_gate_b, w_bp_b, w_ba_b, w_bf_b, w_o_b = [w[None] for w in w_bf16]
        b_gate_l = b_gate_r[l:l + 1]
        if last:
            pc, _ = _in_proj(hc, w_in, l, nw_tiles, tn_in, k_lo // tn_in, (f_lo - k_lo) // tn_in, *qk_tiles)
            kc_blk, vc_blk = 0, blk(d_attn)
        else:
            pc, _ = _in_proj(hc, w_in, l, nw_tiles, tn_in, 0, d_in // tn_in, *qk_tiles)
            kc_blk, vc_blk = blk(k_lo), blk(v_lo)
        y_attn = _nat_attention(p, pc, bias, l, n_heads, blk(q_lo), blk(k_lo), blk(v_lo), kc_blk, vc_blk, ATTN_HEADS)
        y_pool = _pool_mix(p, 0, pool_w_b[l], pool_scale_l)
        y_fnet = _fourier_mix(p, f_lo // d_fnet, cs_chan, fnet_w_b[l])
        m = _merge(h, y_pool, y_attn, y_fnet, 0, w_gate_b, b_gate_l, w_bp_b, w_ba_b, w_bf_b)
        xl = _resid_proj(m, w_o_b, 0, xl, g1, 1024, 1024)

        h2 = _norm_mod(xl, n2w, sh2, sc2)
        halo = _halo_norm(xl, n2w, sh2, sc2, tm_ffn)
        t, (w_down_l,) = _ffn_up(h2, halo, l, w_up, conv_w, conv_b_r, casts=(w_down,))
        w_down_b = w_down_l[None]
        xl = _resid_proj(t, w_down_b, 0, xl, g2, 512, 512)

        if not last:
            yc_attn = _ctx_attention(pc, n_heads, blk(q_lo), blk(k_lo), blk(v_lo))
            yc_pool = _pool_mix(pc, 0, pool_w_b[l], pool_scale_l)
            yc_fnet = _fourier_mix(pc, f_lo // d_fnet, cs_chan, fnet_w_b[l])
            mc = _merge(hc, yc_pool, yc_attn, yc_fnet, 0, w_gate_b, b_gate_l, w_bp_b, w_ba_b, w_bf_b)
            xc = _resid_proj(mc, w_o_b, 0, xc, cg1, 1024, 1024)
            hc2 = _norm_mod(xc, n2w, csh2, csc2)
            halo_c = _halo_norm(xc, n2w, csh2, csc2, n_ctx)
            tc, _ = _ffn_up(hc2, halo_c, l, w_up, conv_w, conv_b_r)
            xc = _resid_proj(tc, w_down_b, 0, xc, cg2, 512, 512)
    return xl.reshape(batch, s_len, d)
```

```python
import functools
import math

import numpy as np
import jax
import jax.numpy as jnp
from jax import lax
from jax.experimental import pallas as pl
from jax.experimental.pallas import tpu as pltpu

F32 = jnp.float32
BF16 = jnp.bfloat16

GRID_W = 64
HEAD_DIM = 128
WIN_ROWS = 8
WIN_COLS = 16
POOL_WINDOWS = (2, 4, 8, 16)
N_GROUPS = 4
N_MOD = 6
CONV_W = 3
EPS = 1e-6

LANES = 128
SUBLANES_F32 = 8
SUBLANES_BF16 = 16
VMEM_CAP_BYTES = 56 * 1024 * 1024

ATTN_ROWS = 4
ATTN_KEY_ROWS = ATTN_ROWS + WIN_ROWS
ATTN_HEADS = 8
FFT_STAGE1_COLS = 16
NEG_BIAS = -1e30
FFT_MIN_LEN = 1024


def _tile(n, pref):
    if n <= pref:
        return n
    t = pref - pref % LANES
    while t >= LANES:
        if n % t == 0:
            return t
        t -= LANES
    return n


def _params(n_axes, vmem_bytes):
    return pltpu.CompilerParams(
        dimension_semantics=("arbitrary",) * n_axes,
        vmem_limit_bytes=int(min(VMEM_CAP_BYTES, max(vmem_bytes, 16 * 1024 * 1024))),
    )


def _dot(a, b):
    return jnp.dot(a, b, preferred_element_type=F32)


def _dot_nt(a, b):
    return lax.dot_general(a, b, (((1,), (1,)), ((), ())), preferred_element_type=F32)


def _rms_mod(x, nw, sh, sc):
    ms = jnp.mean(x * x, axis=-1, keepdims=True)
    return (x * lax.rsqrt(ms + EPS) * nw) * (1.0 + sc) + sh


def _ada_kernel(s_ref, w_ref, b_ref, o_ref):
    s = s_ref[...]
    a = (s * jax.nn.sigmoid(s)).astype(BF16)
    o_ref[0] = _dot(a, w_ref[0].astype(BF16)) + b_ref[0]


def _ada(cond, w_ada, b_ada):
    n_layers, d, n = w_ada.shape
    tn = _tile(n, 1024)
    return pl.pallas_call(
        _ada_kernel,
        grid=(n_layers, n // tn),
        in_specs=[
            pl.BlockSpec((SUBLANES_F32, d), lambda l, j: (0, 0)),
            pl.BlockSpec((1, d, tn), lambda l, j: (l, 0, j)),
            pl.BlockSpec((1, 1, tn), lambda l, j: (l, 0, j)),
        ],
        out_specs=pl.BlockSpec((1, SUBLANES_F32, tn), lambda l, j: (l, 0, j)),
        out_shape=jax.ShapeDtypeStruct((n_layers, SUBLANES_F32, n), F32),
        compiler_params=_params(2, 2 * d * tn * 4 + 8 * 1024 * 1024),
        name="ada",
    )(cond, w_ada, b_ada.reshape(n_layers, 1, n))


def _norm_kernel(x_ref, nw_ref, sh_ref, sc_ref, o_ref):
    o_ref[...] = _rms_mod(x_ref[...], nw_ref[...], sh_ref[...], sc_ref[...]).astype(o_ref.dtype)


def _norm_mod(x, nw, sh, sc):
    m, d = x.shape
    tm = min(m, 512)
    vec = pl.BlockSpec((1, d), lambda i: (0, 0))
    return pl.pallas_call(
        _norm_kernel,
        grid=(m // tm,),
        in_specs=[pl.BlockSpec((tm, d), lambda i: (i, 0)), vec, vec, vec],
        out_specs=pl.BlockSpec((tm, d), lambda i: (i, 0)),
        out_shape=jax.ShapeDtypeStruct((m, d), BF16),
        compiler_params=_params(1, 6 * tm * d * 4),
        name="norm_mod",
    )(x, nw, sh, sc)


def _halo_kernel(xp_ref, xn_ref, nw_ref, sh_ref, sc_ref, o_ref):
    i = pl.program_id(0)
    n = pl.num_programs(0)
    hp = _rms_mod(xp_ref[...], nw_ref[...], sh_ref[...], sc_ref[...])
    hn = _rms_mod(xn_ref[...], nw_ref[...], sh_ref[...], sc_ref[...])
    hp = jnp.where(i > 0, hp, 0.0)
    hn = jnp.where(i < n - 1, hn, 0.0)
    o_ref[0] = jnp.concatenate([hn, hp], axis=0).astype(o_ref.dtype)


def _halo_norm(x, nw, sh, sc, tm):
    m, d = x.shape
    nblk = m // tm
    r = SUBLANES_F32
    per = tm // r
    last = m // r - 1
    vec = pl.BlockSpec((1, d), lambda i: (0, 0))
    return pl.pallas_call(
        _halo_kernel,
        grid=(nblk,),
        in_specs=[
            pl.BlockSpec((r, d), lambda i: (jnp.maximum(i * per - 1, 0), 0)),
            pl.BlockSpec((r, d), lambda i: (jnp.minimum((i + 1) * per, last), 0)),
            vec, vec, vec,
        ],
        out_specs=pl.BlockSpec((1, 2 * r, d), lambda i: (i, 0, 0)),
        out_shape=jax.ShapeDtypeStruct((nblk, 2 * r, d), BF16),
        compiler_params=_params(1, 16 * 1024 * 1024),
        name="halo_norm",
    )(x, x, nw, sh, sc)


def _cast_spec(shape, layer, block_of_step):
    _, rows, cols = shape
    n_steps, step_of = block_of_step
    rb = max(SUBLANES_BF16, rows // n_steps // SUBLANES_BF16 * SUBLANES_BF16)
    while rows % rb:
        rb += SUBLANES_BF16
    n_blk = rows // rb

    def blk(i, j):
        return step_of(i, j) * n_blk // n_steps

    return (pl.BlockSpec((None, rb, cols), lambda i, j: (layer, blk(i, j), 0)),
            pl.BlockSpec((rb, cols), lambda i, j: (blk(i, j), 0)),
            jax.ShapeDtypeStruct((rows, cols), BF16))


def _inproj_kernel(*refs, j_off, qk_lo, qk_hi, n_cast):
    a_ref, b_ref, nw_ref = refs[:3]
    o_ref = refs[3 + n_cast]
    for src, dst in zip(refs[3:3 + n_cast], refs[4 + n_cast:4 + 2 * n_cast]):
        dst[...] = src[...].astype(dst.dtype)
    acc = _dot(a_ref[...], b_ref[...])
    j = pl.program_id(1) + j_off
    is_qk = jnp.logical_and(j >= qk_lo, j < qk_hi)

    @pl.when(is_qk)
    def _():
        for h in range(acc.shape[1] // HEAD_DIM):
            sl = slice(h * HEAD_DIM, (h + 1) * HEAD_DIM)
            xh = acc[:, sl]
            ms = jnp.mean(xh * xh, axis=-1, keepdims=True)
            o_ref[:, sl] = (xh * lax.rsqrt(ms + EPS) * nw_ref[0, :, sl]).astype(o_ref.dtype)

    @pl.when(jnp.logical_not(is_qk))
    def _():
        o_ref[...] = acc.astype(o_ref.dtype)


def _in_proj(h, w_in, layer, nw_tiles, tn, j_off, n_tiles, qk_lo, qk_hi, casts=()):
    m, d = h.shape
    tm = min(m, 1024)
    steps = (m // tm * n_tiles, lambda i, j: i * n_tiles + j)
    cast_specs = [_cast_spec(w.shape, lyr, steps) for w, lyr in casts]
    cast_bytes = sum(6 * 2 * s[1].block_shape[0] * s[1].block_shape[1] for s in cast_specs)
    out = pl.pallas_call(
        functools.partial(_inproj_kernel, j_off=j_off, qk_lo=qk_lo, qk_hi=qk_hi, n_cast=len(casts)),
        grid=(m // tm, n_tiles),
        in_specs=[
            pl.BlockSpec((tm, d), lambda i, j: (i, 0), pipeline_mode=pl.Buffered(1 if casts else 2)),
            pl.BlockSpec((None, d, tn), lambda i, j: (layer, 0, j + j_off)),
            pl.BlockSpec((1, 1, tn), lambda i, j: (j + j_off, 0, 0)),
        ] + [s[0] for s in cast_specs],
        out_specs=[pl.BlockSpec((tm, tn), lambda i, j: (i, j))] + [s[1] for s in cast_specs],
        out_shape=[jax.ShapeDtypeStruct((m, n_tiles * tn), BF16)] + [s[2] for s in cast_specs],
        compiler_params=_params(2, 4 * tm * d + 4 * d * tn + 4 * tm * tn + 3 * tm * tn * 4 + cast_bytes),
        name="in_proj",
    )(h, w_in, nw_tiles, *[w for w, _ in casts])
    return out[0], out[1:]


def _nat_kernel(q_ref, k_ref, v_ref, kc_ref, vc_ref, bias_ref, o_ref, *, grid_rows, scale, heads):
    b = pl.program_id(1)
    ks = jnp.clip(b * ATTN_ROWS - WIN_ROWS // 2, 0, grid_rows - ATTN_KEY_ROWS)
    start = pl.multiple_of(ks * GRID_W, GRID_W)
    nkeys = ATTN_KEY_ROWS * GRID_W
    for h in range(heads):
        sl = slice(h * HEAD_DIM, (h + 1) * HEAD_DIM)
        q = q_ref[:, sl]
        kw = k_ref[pl.ds(start, nkeys), sl]
        vw = v_ref[pl.ds(start, nkeys), sl]
        s = _dot_nt(q, kw) * scale + bias_ref[0, h]
        sc = _dot_nt(q, kc_ref[:, sl]) * scale
        m = jnp.maximum(jnp.max(s, axis=-1, keepdims=True), jnp.max(sc, axis=-1, keepdims=True))
        p = jnp.exp(s - m)
        pc = jnp.exp(sc - m)
        l = jnp.sum(p, axis=-1, keepdims=True) + jnp.sum(pc, axis=-1, keepdims=True)
        o = _dot(p.astype(BF16), vw) + _dot(pc.astype(BF16), vc_ref[:, sl])
        o_ref[:, sl] = (o / l).astype(o_ref.dtype)


def _nat_bias(rpb, grid_rows):
    n_heads = rpb.shape[0]
    nb = grid_rows // ATTN_ROWS
    c = np.arange(GRID_W)[:, None]
    kc = np.arange(GRID_W)[None, :]
    c_start = np.clip(c - WIN_COLS // 2, 0, GRID_W - WIN_COLS)
    col_ok = (kc >= c_start) & (kc < c_start + WIN_COLS)
    dc = kc - c + (WIN_COLS - 1)
    n_dc = 2 * WIN_COLS - 1
    expand = ((dc[None] == np.arange(n_dc)[:, None, None]) & col_ok[None]).astype(np.float32)
    blocks = jnp.einsum("hrd,dn->hrn", rpb, expand.reshape(n_dc, GRID_W * GRID_W),
                        precision=lax.Precision.HIGHEST)
    blocks = blocks.reshape(n_heads, 2 * WIN_ROWS - 1, GRID_W, GRID_W)
    blocks = jnp.where(col_ok, blocks, NEG_BIAS)
    masked = jnp.full((n_heads, GRID_W, GRID_W), NEG_BIAS, F32)
    variants = []
    for b in (0, 1, nb - 1):
        ks = int(np.clip(b * ATTN_ROWS - WIN_ROWS // 2, 0, grid_rows - ATTN_KEY_ROWS))
        q_rows = []
        for r_loc in range(ATTN_ROWS):
            r = b * ATTN_ROWS + r_loc
            r_start = int(np.clip(r - WIN_ROWS // 2, 0, grid_rows - WIN_ROWS))
            row = []
            for kr in range(ks, ks + ATTN_KEY_ROWS):
                ok = r_start <= kr < r_start + WIN_ROWS
                row.append(blocks[:, kr - r + (WIN_ROWS - 1)] if ok else masked)
            q_rows.append(jnp.concatenate(row, axis=-1))
        variants.append(jnp.concatenate(q_rows, axis=-2))
    return jnp.stack(variants)


def _nat_attention(p_lat, p_ctx, bias, layer, n_heads, q_blk, k_blk, v_blk, kc_blk, vc_blk, hs):
    s_len = p_lat.shape[0]
    n_ctx = p_ctx.shape[0]
    grid_rows = s_len // GRID_W
    nb = grid_rows // ATTN_ROWS
    nq = ATTN_ROWS * GRID_W
    nk = ATTN_KEY_ROWS * GRID_W
    assert grid_rows % ATTN_ROWS == 0 and nb >= 3 and grid_rows >= ATTN_KEY_ROWS
    while n_heads % hs or any(blk % hs for blk in (q_blk, k_blk, v_blk, kc_blk, vc_blk)):
        hs //= 2
    width = hs * HEAD_DIM

    def variant(b):
        return jnp.where(b == 0, 0, jnp.where(b == nb - 1, 2, 1))

    def cols(rows, blk):
        mode = pl.Buffered(1 if (rows == s_len and hs > 2) else 2)
        return pl.BlockSpec((rows, width), lambda h, b: (0, blk // hs + h), pipeline_mode=mode)

    return pl.pallas_call(
        functools.partial(_nat_kernel, grid_rows=grid_rows, scale=HEAD_DIM ** -0.5, heads=hs),
        grid=(n_heads // hs, nb),
        in_specs=[
            pl.BlockSpec((nq, width), lambda h, b: (b, q_blk // hs + h)),
            cols(s_len, k_blk), cols(s_len, v_blk), cols(n_ctx, kc_blk), cols(n_ctx, vc_blk),
            pl.BlockSpec((1, hs, nq, nk), lambda h, b: (variant(b), layer * (n_heads // hs) + h, 0, 0)),
        ],
        out_specs=pl.BlockSpec((nq, width), lambda h, b: (b, h)),
        out_shape=jax.ShapeDtypeStruct((s_len, n_heads * HEAD_DIM), BF16),
        compiler_params=_params(2, 8 * s_len * width + hs * (2 * nq * nk * 4 + 16 * nq * (nk + n_ctx) * 4)),
        name="nat_attention",
    )(p_lat, p_lat, p_lat, p_ctx, p_ctx, bias)


def _ctx_attn_kernel(q_ref, k_ref, v_ref, o_ref, *, scale):
    s = _dot_nt(q_ref[...], k_ref[...]) * scale
    m = jnp.max(s, axis=-1, keepdims=True)
    p = jnp.exp(s - m)
    l = jnp.sum(p, axis=-1, keepdims=True)
    o_ref[...] = (_dot(p.astype(BF16), v_ref[...]) / l).astype(o_ref.dtype)


def _ctx_attention(p_ctx, n_heads, q_blk, k_blk, v_blk):
    n = p_ctx.shape[0]
    return pl.pallas_call(
        functools.partial(_ctx_attn_kernel, scale=HEAD_DIM ** -0.5),
        grid=(n_heads,),
        in_specs=[
            pl.BlockSpec((n, HEAD_DIM), lambda h: (0, q_blk + h)),
            pl.BlockSpec((n, HEAD_DIM), lambda h: (0, k_blk + h)),
            pl.BlockSpec((n, HEAD_DIM), lambda h: (0, v_blk + h)),
        ],
        out_specs=pl.BlockSpec((n, HEAD_DIM), lambda h: (0, h)),
        out_shape=jax.ShapeDtypeStruct((n, n_heads * HEAD_DIM), BF16),
        compiler_params=_params(1, 16 * 1024 * 1024),
        name="ctx_attention",
    )(p_ctx, p_ctx, p_ctx)


def _pool_kernel(up_ref, um_ref, un_ref, pw_ref, ps_ref, o_ref, *, seq_len, gc):
    i = pl.program_id(0)
    n = pl.num_programs(0)
    tp = um_ref.shape[0]
    halo = up_ref.shape[0]
    um = um_ref[...]
    up = jnp.where(i > 0, up_ref[...], jnp.zeros_like(up_ref[...]))
    un = jnp.where(i < n - 1, un_ref[...], jnp.zeros_like(un_ref[...]))
    ue = jnp.concatenate([up, um, un], axis=0)
    trow = lax.broadcasted_iota(jnp.int32, (tp, tp + 2 * halo), 0)
    jcol = lax.broadcasted_iota(jnp.int32, (tp, tp + 2 * halo), 1)
    off = jcol - trow - halo
    t_abs = i * tp + lax.broadcasted_iota(jnp.int32, (tp, gc), 0)
    for g, w in enumerate(POOL_WINDOWS):
        lo_off = -(w // 2)
        hi_off = w - 1 - w // 2
        band = jnp.where(off >= lo_off, jnp.where(off <= hi_off, 1.0, 0.0), 0.0).astype(BF16)
        sl = slice(g * gc, (g + 1) * gc)
        wsum = _dot(band, ue[:, sl])
        lo = jnp.maximum(t_abs + lo_off, 0)
        hi = jnp.minimum(t_abs + hi_off, seq_len - 1)
        cnt = (hi - lo + 1).astype(F32)
        d = (wsum / cnt - um[:, sl].astype(F32)).astype(BF16)
        o_ref[:, sl] = (_dot(d, pw_ref[g]) * ps_ref[:, sl]).astype(o_ref.dtype)


def _pool_mix(p, col_blk, pool_w, pool_scale):
    m = p.shape[0]
    gc = pool_w.shape[-1]
    width = N_GROUPS * gc
    tp = min(m, 512)
    halo = SUBLANES_BF16
    per = tp // halo
    last = m // halo - 1
    return pl.pallas_call(
        functools.partial(_pool_kernel, seq_len=m, gc=gc),
        grid=(m // tp,),
        in_specs=[
            pl.BlockSpec((halo, width), lambda i: (jnp.maximum(i * per - 1, 0), col_blk)),
            pl.BlockSpec((tp, width), lambda i: (i, col_blk)),
            pl.BlockSpec((halo, width), lambda i: (jnp.minimum((i + 1) * per, last), col_blk)),
            pl.BlockSpec((N_GROUPS, gc, gc), lambda i: (0, 0, 0)),
            pl.BlockSpec((1, width), lambda i: (0, 0)),
        ],
        out_specs=pl.BlockSpec((tp, width), lambda i: (i, 0)),
        out_shape=jax.ShapeDtypeStruct((m, width), BF16),
        compiler_params=_params(1, 32 * 1024 * 1024),
        name="pool_mix",
    )(p, p, p, pool_w, pool_scale)


def _cos_sin(num, den):
    ang = (num % den).astype(F32) * (2.0 * math.pi / den)
    return jnp.cos(ang), jnp.sin(ang)


def _dft_tables(n):
    idx = jnp.arange(n, dtype=jnp.int32)
    return _cos_sin(idx[:, None] * idx[None, :], n)


def _fnet_chan_kernel(u_ref, cs_ref, o_ref, *, gc):
    for g in range(N_GROUPS):
        sl = slice(g * gc, (g + 1) * gc)
        o_ref[:, sl] = _dot(u_ref[:, sl], cs_ref[0]).astype(o_ref.dtype)


def _fnet_pos_kernel(t_ref, ab_ref, fw_ref, o_ref, *, scale, gc):
    acc = _dot(t_ref[...], ab_ref[...])
    for g in range(N_GROUPS):
        sl = slice(g * gc, (g + 1) * gc)
        f = (acc[:, sl] * scale).astype(BF16)
        o_ref[:, sl] = _dot(f, fw_ref[g]).astype(o_ref.dtype)


def _fourier_mix_dense(p, col_blk, cs_chan, fnet_w):
    m = p.shape[0]
    gc = fnet_w.shape[-1]
    width = N_GROUPS * gc
    cpos, spos = _dft_tables(m)
    t_pos = jnp.concatenate([cpos, -spos], axis=1).astype(BF16)
    ab = pl.pallas_call(
        functools.partial(_fnet_chan_kernel, gc=gc),
        grid=(2,),
        in_specs=[
            pl.BlockSpec((m, width), lambda c: (0, col_blk)),
            pl.BlockSpec((1, gc, gc), lambda c: (c, 0, 0)),
        ],
        out_specs=pl.BlockSpec((m, width), lambda c: (c, 0)),
        out_shape=jax.ShapeDtypeStruct((2 * m, width), BF16),
        compiler_params=_params(1, 32 * 1024 * 1024),
        name="fnet_chan",
    )(p, cs_chan)
    return pl.pallas_call(
        functools.partial(_fnet_pos_kernel, scale=1.0 / math.sqrt(m * gc), gc=gc),
        grid=(1,),
        in_specs=[
            pl.BlockSpec((m, 2 * m), lambda i: (0, 0)),
            pl.BlockSpec((2 * m, width), lambda i: (0, 0)),
            pl.BlockSpec((N_GROUPS, gc, gc), lambda i: (0, 0, 0)),
        ],
        out_specs=pl.BlockSpec((m, width), lambda i: (0, 0)),
        out_shape=jax.ShapeDtypeStruct((m, width), BF16),
        compiler_params=_params(1, 32 * 1024 * 1024),
        name="fnet_pos",
    )(t_pos, ab, fnet_w)


def _fft_stage1_kernel(f1_ref, u_ref, tc_ref, ts_ref, zr_ref, zi_ref, *, n1):
    width = zr_ref.shape[1]
    reps = width // LANES
    for t in range(u_ref.shape[1] // width):
        y = _dot(f1_ref[...], u_ref[:, t * width:(t + 1) * width])
        yr, yi = y[:n1], y[n1:]
        tc = jnp.tile(tc_ref[:, t * LANES:(t + 1) * LANES], (1, reps))
        ts = jnp.tile(ts_ref[:, t * LANES:(t + 1) * LANES], (1, reps))
        zr_ref[t * n1:(t + 1) * n1, :] = (yr * tc + yi * ts).astype(zr_ref.dtype)
        zi_ref[t * n1:(t + 1) * n1, :] = (yi * tc - yr * ts).astype(zi_ref.dtype)


def _fft_stage2_kernel(fa_ref, fb_ref, zr_ref, zi_ref, cc_ref, sc_ref, fw_ref, o_ref, *, n2, gc, chunks, scale):
    width = N_GROUPS * gc
    p = _dot(fa_ref[...], zr_ref[...]) + _dot(fb_ref[...], zi_ref[...])
    for g in range(N_GROUPS):
        cols = [slice(ch * width + g * gc, ch * width + (g + 1) * gc) for ch in range(chunks)]
        pr = jnp.concatenate([p[:n2, c] for c in cols], axis=0).astype(BF16)
        pi = jnp.concatenate([p[n2:, c] for c in cols], axis=0).astype(BF16)
        f = ((_dot(pr, cc_ref[...]) + _dot(pi, sc_ref[...])) * scale).astype(BF16)
        y = _dot(f, fw_ref[g]).astype(o_ref.dtype)
        for ch, c in enumerate(cols):
            o_ref[:, c] = y[ch * n2:(ch + 1) * n2]


def _fourier_mix_fft(p, col_blk, cs_chan, fnet_w):
    m, d_in = p.shape
    gc = fnet_w.shape[-1]
    width = N_GROUPS * gc
    n2 = LANES
    n1 = m // n2
    i1 = jnp.arange(n1, dtype=jnp.int32)
    i2 = jnp.arange(n2, dtype=jnp.int32)
    c1, s1 = _cos_sin(i1[:, None] * i1[None, :], n1)
    f1 = jnp.concatenate([c1, -s1], axis=0).astype(BF16)
    tc, ts = _cos_sin(i1[:, None] * i2[None, :], m)
    tc = jnp.repeat(tc, LANES, axis=1)
    ts = jnp.repeat(ts, LANES, axis=1)
    c2, s2 = _cos_sin(i2[:, None] * i2[None, :], n2)
    fa = jnp.concatenate([c2, -s2], axis=0).astype(BF16)
    fb = jnp.concatenate([s2, c2], axis=0).astype(BF16)
    u = p[:, col_blk * width:(col_blk + 1) * width].reshape(n1, n2 * width)
    z_shape = jax.ShapeDtypeStruct((n2 * n1, width), BF16)
    sb = FFT_STAGE1_COLS
    zr, zi = pl.pallas_call(
        functools.partial(_fft_stage1_kernel, n1=n1),
        grid=(n2 // sb,),
        in_specs=[
            pl.BlockSpec((2 * n1, n1), lambda j: (0, 0)),
            pl.BlockSpec((n1, sb * width), lambda j: (0, j)),
            pl.BlockSpec((n1, sb * LANES), lambda j: (0, j)),
            pl.BlockSpec((n1, sb * LANES), lambda j: (0, j)),
        ],
        out_specs=[pl.BlockSpec((sb * n1, width), lambda j: (j, 0))] * 2,
        out_shape=[z_shape, z_shape],
        compiler_params=_params(1, 16 * 1024 * 1024),
        name="fft_stage1",
    )(f1, u, tc, ts)
    chunks = min(n1, 8)
    tn = chunks * width
    y = pl.pallas_call(
        functools.partial(_fft_stage2_kernel, n2=n2, gc=gc, chunks=chunks, scale=1.0 / math.sqrt(m * gc)),
        grid=(n1 // chunks,),
        in_specs=[
            pl.BlockSpec((2 * n2, n2), lambda j: (0, 0)),
            pl.BlockSpec((2 * n2, n2), lambda j: (0, 0)),
            pl.BlockSpec((n2, tn), lambda j: (0, j)),
            pl.BlockSpec((n2, tn), lambda j: (0, j)),
            pl.BlockSpec((None, gc, gc), lambda j: (0, 0, 0)),
            pl.BlockSpec((None, gc, gc), lambda j: (1, 0, 0)),
            pl.BlockSpec((N_GROUPS, gc, gc), lambda j: (0, 0, 0)),
        ],
        out_specs=pl.BlockSpec((n2, tn), lambda j: (0, j)),
        out_shape=jax.ShapeDtypeStruct((n2, n1 * width), BF16),
        compiler_params=_params(1, 40 * 1024 * 1024),
        name="fft_stage2",
    )(fa, fb, zr.reshape(n2, n1 * width), zi.reshape(n2, n1 * width), cs_chan, cs_chan, fnet_w)
    return y.reshape(m, width)


def _fourier_mix(p, col_blk, cs_chan, fnet_w):
    if p.shape[0] >= FFT_MIN_LEN and p.shape[0] % (LANES * SUBLANES_F32) == 0:
        return _fourier_mix_fft(p, col_blk, cs_chan, fnet_w)
    return _fourier_mix_dense(p, col_blk, cs_chan, fnet_w)


def _merge_kernel(h_ref, yp_ref, ya_ref, yf_ref, wg0, wg1, wg2, bg0, bg1, bg2, wp, wa, wf, o_ref):
    h = h_ref[...]

    def gate(w_ref, b_ref):
        return jax.nn.sigmoid(_dot(h, w_ref[...]) + b_ref[...])

    m = gate(wg0, bg0) * _dot(yp_ref[...], wp[...])
    m = m + gate(wg1, bg1) * _dot(ya_ref[...], wa[...])
    m = m + gate(wg2, bg2) * _dot(yf_ref[...], wf[...])
    o_ref[...] = m.astype(o_ref.dtype)


def _merge(h, y_pool, y_attn, y_fnet, layer, w_gate, b_gate, w_bp, w_ba, w_bf):
    m, d = h.shape
    dp, da, df = y_pool.shape[1], y_attn.shape[1], y_fnet.shape[1]
    tm = min(m, 1024)
    tn = _tile(d, 256)
    nj = d // tn

    def act(width):
        bufs = 1 if width == d else 2
        return pl.BlockSpec((tm, width), lambda i, j: (i, 0), pipeline_mode=pl.Buffered(bufs))

    def gate_w(b):
        return pl.BlockSpec((None, d, tn), lambda i, j: (layer, 0, b * nj + j))

    def gate_b(b):
        return pl.BlockSpec((None, 1, tn), lambda i, j: (layer, 0, b * nj + j))

    def br_w(k):
        return pl.BlockSpec((None, k, tn), lambda i, j: (layer, 0, j))

    vmem = 4 * tm * (d + dp + da + df) + 4 * tn * (3 * d + dp + da + df) + 12 * tm * tn * 4
    return pl.pallas_call(
        _merge_kernel,
        grid=(m // tm, nj),
        in_specs=[act(d), act(dp), act(da), act(df), gate_w(0), gate_w(1), gate_w(2),
                  gate_b(0), gate_b(1), gate_b(2), br_w(dp), br_w(da), br_w(df)],
        out_specs=pl.BlockSpec((tm, tn), lambda i, j: (i, j)),
        out_shape=jax.ShapeDtypeStruct((m, d), BF16),
        compiler_params=_params(2, vmem),
        name="merge",
    )(h, y_pool, y_attn, y_fnet, w_gate, w_gate, w_gate, b_gate, b_gate, b_gate, w_bp, w_ba, w_bf)


def _resid_kernel(a_ref, w_ref, x_ref, g_ref, o_ref):
    o_ref[...] = x_ref[...] + g_ref[...] * _dot(a_ref[...], w_ref[...])


def _resid_proj(a, w, layer, x, g, tm_pref, tn_pref):
    m, kk = a.shape
    n = w.shape[2]
    tm = min(m, tm_pref)
    tn = _tile(n, tn_pref)
    return pl.pallas_call(
        _resid_kernel,
        grid=(m // tm, n // tn),
        in_specs=[
            pl.BlockSpec((tm, kk), lambda i, j: (i, 0)),
            pl.BlockSpec((None, kk, tn), lambda i, j: (layer, 0, j)),
            pl.BlockSpec((tm, tn), lambda i, j: (i, j)),
            pl.BlockSpec((1, tn), lambda i, j: (0, j)),
        ],
        out_specs=pl.BlockSpec((tm, tn), lambda i, j: (i, j)),
        out_shape=jax.ShapeDtypeStruct((m, n), F32),
        compiler_params=_params(2, 4 * tm * kk + 4 * kk * tn + 7 * tm * tn * 4),
        name="resid_proj",
    )(a, w, x, g)


def _ffn_up_kernel(*refs, n_cast):
    a_ref, halo_ref, wg_ref, wv_ref, cwg_ref, cwv_ref, cbg_ref, cbv_ref = refs[:8]
    o_ref = refs[8 + n_cast]
    a_ext = refs[-1]
    for src, dst in zip(refs[8:8 + n_cast], refs[9 + n_cast:9 + 2 * n_cast]):
        dst[...] = src[...].astype(dst.dtype)
    tm = a_ref.shape[0]

    @pl.when(pl.program_id(1) == 0)
    def _():
        a_ext[:tm, :] = a_ref[...]
        a_ext[tm:, :] = halo_ref[0]

    a = a_ext[...]
    row = lax.broadcasted_iota(jnp.int32, (tm, wg_ref.shape[1]), 0)

    def conv(w_ref, cw_ref, cb_ref):
        u = _dot(a, w_ref[...].astype(BF16))
        um = u[:tm]
        next_row = u[tm:tm + 1]
        prev_row = u[tm + 2 * SUBLANES_F32 - 1:tm + 2 * SUBLANES_F32]
        u_dn = jnp.where(row == 0, prev_row, pltpu.roll(um, 1, 0))
        u_up = jnp.where(row == tm - 1, next_row, pltpu.roll(um, tm - 1, 0))
        return u_dn * cw_ref[0:1] + um * cw_ref[1:2] + u_up * cw_ref[2:3] + cb_ref[...]

    gate = conv(wg_ref, cwg_ref, cbg_ref)
    val = conv(wv_ref, cwv_ref, cbv_ref)
    o_ref[...] = (gate * jax.nn.sigmoid(gate) * val).astype(o_ref.dtype)


def _ffn_up(h2, halo, layer, w_up, conv_w, conv_b, casts=()):
    m, d = h2.shape
    f = w_up.shape[2] // 2
    tm = m // halo.shape[0]
    ext = halo.shape[1]
    tn = _tile(f, 256)
    nj = f // tn
    w_bytes = jnp.dtype(w_up.dtype).itemsize
    a_bufs = 1 if w_bytes == 4 else 2
    vmem = (2 * a_bufs * tm * d + 2 * (tm + ext) * d + (4 * w_bytes + 4) * d * tn + 4 * tm * tn
            + 12 * (tm + ext) * tn * 4)

    def cols(rows, half):
        return pl.BlockSpec((None, rows, tn), lambda i, j: (layer, 0, half * nj + j))

    steps = (m // tm * nj, lambda i, j: i * nj + j)
    cast_specs = [_cast_spec(w.shape, lyr, steps) for w, lyr in casts]
    vmem += sum(6 * 2 * s[1].block_shape[0] * s[1].block_shape[1] for s in cast_specs)
    out = pl.pallas_call(
        functools.partial(_ffn_up_kernel, n_cast=len(casts)),
        grid=(m // tm, nj),
        in_specs=[
            pl.BlockSpec((tm, d), lambda i, j: (i, 0), pipeline_mode=pl.Buffered(a_bufs)),
            pl.BlockSpec((1, ext, d), lambda i, j: (i, 0, 0)),
            cols(d, 0), cols(d, 1), cols(CONV_W, 0), cols(CONV_W, 1), cols(1, 0), cols(1, 1),
        ] + [s[0] for s in cast_specs],
        out_specs=[pl.BlockSpec((tm, tn), lambda i, j: (i, j))] + [s[1] for s in cast_specs],
        out_shape=[jax.ShapeDtypeStruct((m, f), BF16)] + [s[2] for s in cast_specs],
        scratch_shapes=[pltpu.VMEM((tm + ext, d), BF16)],
        compiler_params=_params(2, vmem),
        name="ffn_up",
    )(h2, halo, w_up, w_up, conv_w, conv_w, conv_b, conv_b, *[w for w, _ in casts])
    return out[0], out[1:]


def kernel(x, c, ctx, c_ctx, w_ada, b_ada, norm1_w, norm2_w, w_in, pool_w, pool_scale, q_norm_w, k_norm_w, rpb, fnet_w, w_gate, b_gate, w_br_pool, w_br_attn, w_br_fnet, w_o, w_up, conv_w, conv_b, w_down):
    n_layers, d, d_in = w_in.shape
    batch, s_len, _ = x.shape
    n_ctx = ctx.shape[1]
    assert batch == 1
    d_pool = w_br_pool.shape[1]
    d_attn = w_br_attn.shape[1]
    d_fnet = w_br_fnet.shape[1]
    n_heads = d_attn // HEAD_DIM
    assert d_in == d_pool + 3 * d_attn + d_fnet and s_len % GRID_W == 0 and d_fnet == d_pool
    q_lo, k_lo, v_lo, f_lo = d_pool, d_pool + d_attn, d_pool + 2 * d_attn, d_pool + 3 * d_attn

    pool_w_b = pool_w.astype(BF16)
    fnet_w_b = fnet_w.astype(BF16)
    b_gate_r = b_gate.reshape(n_layers, 1, -1)
    conv_b_r = conv_b.reshape(n_layers, 1, -1)
    ones_p = jnp.ones((d_pool,), F32)
    ones_vf = jnp.ones((d_attn + d_fnet,), F32)

    cs_chan = jnp.stack(_dft_tables(fnet_w.shape[-1])).astype(BF16)

    cond = jnp.concatenate([c.reshape(1, d), c_ctx.reshape(1, d), jnp.zeros((SUBLANES_F32 - 2, d), F32)], axis=0)
    mod = _ada(cond, w_ada, b_ada)

    bias = _nat_bias(rpb.reshape((n_layers * n_heads,) + rpb.shape[2:]), s_len // GRID_W)

    tm_ffn = min(s_len, 1024)
    blk = lambda col: col // HEAD_DIM
    tn_in = _tile(d_pool, 1024)
    assert d_pool % tn_in == 0 and d_attn % tn_in == 0
    qk_tiles = (q_lo // tn_in, v_lo // tn_in)

    xl = x.reshape(s_len, d)
    xc = ctx.reshape(n_ctx, d)
    w_in_b = w_in[:1].astype(BF16)
    for l in range(n_layers):
        last = l == n_layers - 1
        sh1, sc1, g1, sh2, sc2, g2 = [mod[l, 0:1, i * d:(i + 1) * d] for i in range(N_MOD)]
        csh1, csc1, cg1, csh2, csc2, cg2 = [mod[l, 1:2, i * d:(i + 1) * d] for i in range(N_MOD)]
        n1w = norm1_w[l].reshape(1, d)
        n2w = norm2_w[l].reshape(1, d)
        nw_tiles = jnp.concatenate(
            [ones_p, jnp.tile(q_norm_w[l], n_heads), jnp.tile(k_norm_w[l], n_heads), ones_vf]
        ).reshape(d_in // tn_in, 1, tn_in)
        pool_scale_l = pool_scale[l].reshape(1, -1)

        h = _norm_mod(xl, n1w, sh1, sc1)
        hc = _norm_mod(xc, n1w, csh1, csc1)
        p, w_bf16 = _in_proj(h, w_in_b, 0, nw_tiles, tn_in, 0, d_in // tn_in, *qk_tiles,
                             casts=[(w, l) for w in (w_gate, w_br_pool, w_br_attn, w_br_fnet, w_o)])
        w_gate_b, w_bp_b, w_ba_b, w_bf_b, w_o_b = [w[None] for w in w_bf16]
        b_gate_l = b_gate_r[l:l + 1]
        if last:
            pc, _ = _in_proj(hc, w_in_b, 0, nw_tiles, tn_in, k_lo // tn_in, (f_lo - k_lo) // tn_in, *qk_tiles)
            kc_blk, vc_blk = 0, blk(d_attn)
        else:
            pc, _ = _in_proj(hc, w_in_b, 0, nw_tiles, tn_in, 0, d_in // tn_in, *qk_tiles)
            kc_blk, vc_blk = blk(k_lo), blk(v_lo)
        y_attn = _nat_attention(p, pc, bias, l, n_heads, blk(q_lo), blk(k_lo), blk(v_lo), kc_blk, vc_blk, ATTN_HEADS)
        y_pool = _pool_mix(p, 0, pool_w_b[l], pool_scale_l)
        y_fnet = _fourier_mix(p, f_lo // d_fnet, cs_chan, fnet_w_b[l])
        m = _merge(h, y_pool, y_attn, y_fnet, 0, w_gate_b, b_gate_l, w_bp_b, w_ba_b, w_bf_b)
        xl = _resid_proj(m, w_o_b, 0, xl, g1, 1024, 1024)

        h2 = _norm_mod(xl, n2w, sh2, sc2)
        halo = _halo_norm(xl, n2w, sh2, sc2, tm_ffn)
        side = [(w_down, l)] + ([] if last else [(w_in, l + 1)])
        t, w_bf16 = _ffn_up(h2, halo, l, w_up, conv_w, conv_b_r, casts=side)
        w_down_b = w_bf16[0][None]
        if not last:
            w_in_b = w_bf16[1][None]
        xl = _resid_proj(t, w_down_b, 0, xl, g2, 512, 512)

        if not last:
            yc_attn = _ctx_attention(pc, n_heads, blk(q_lo), blk(k_lo), blk(v_lo))
            yc_pool = _pool_mix(pc, 0, pool_w_b[l], pool_scale_l)
            yc_fnet = _fourier_mix(pc, f_lo // d_fnet, cs_chan, fnet_w_b[l])
            mc = _merge(hc, yc_pool, yc_attn, yc_fnet, 0, w_gate_b, b_gate_l, w_bp_b, w_ba_b, w_bf_b)
            xc = _resid_proj(mc, w_o_b, 0, xc, cg1, 1024, 1024)
            hc2 = _norm_mod(xc, n2w, csh2, csc2)
            halo_c = _halo_norm(xc, n2w, csh2, csc2, n_ctx)
            tc, _ = _ffn_up(hc2, halo_c, l, w_up, conv_w, conv_b_r)
            xc = _resid_proj(tc, w_down_b, 0, xc, cg2, 512, 512)
    return xl.reshape(batch, s_len, d)
```

```python
import functools
import math

import numpy as np
import jax
import jax.numpy as jnp
from jax import lax
from jax.experimental import pallas as pl
from jax.experimental.pallas import tpu as pltpu

F32 = jnp.float32
BF16 = jnp.bfloat16

GRID_W = 64
HEAD_DIM = 128
WIN_ROWS = 8
WIN_COLS = 16
POOL_WINDOWS = (2, 4, 8, 16)
N_GROUPS = 4
N_MOD = 6
CONV_W = 3
EPS = 1e-6

LANES = 128
SUBLANES_F32 = 8
SUBLANES_BF16 = 16
VMEM_CAP_BYTES = 56 * 1024 * 1024

ATTN_ROWS = 4
ATTN_KEY_ROWS = ATTN_ROWS + WIN_ROWS
ATTN_HEADS = 8
FFT_STAGE1_COLS = 16
NEG_BIAS = -1e30
FFT_MIN_LEN = 1024


def _tile(n, pref):
    if n <= pref:
        return n
    t = pref - pref % LANES
    while t >= LANES:
        if n % t == 0:
            return t
        t -= LANES
    return n


def _params(n_axes, vmem_bytes):
    return pltpu.CompilerParams(
        dimension_semantics=("arbitrary",) * n_axes,
        vmem_limit_bytes=int(min(VMEM_CAP_BYTES, max(vmem_bytes, 16 * 1024 * 1024))),
    )


def _dot(a, b):
    return jnp.dot(a, b, preferred_element_type=F32)


def _dot_nt(a, b):
    return lax.dot_general(a, b, (((1,), (1,)), ((), ())), preferred_element_type=F32)


def _rms_mod(x, nw, sh, sc):
    ms = jnp.mean(x * x, axis=-1, keepdims=True)
    return (x * lax.rsqrt(ms + EPS) * nw) * (1.0 + sc) + sh


def _ada_kernel(s_ref, w_ref, b_ref, o_ref):
    s = s_ref[...]
    a = (s * jax.nn.sigmoid(s)).astype(BF16)
    o_ref[0] = _dot(a, w_ref[0].astype(BF16)) + b_ref[0]


def _ada(cond, w_ada, b_ada):
    n_layers, d, n = w_ada.shape
    tn = _tile(n, 1024)
    return pl.pallas_call(
        _ada_kernel,
        grid=(n_layers, n // tn),
        in_specs=[
            pl.BlockSpec((SUBLANES_F32, d), lambda l, j: (0, 0)),
            pl.BlockSpec((1, d, tn), lambda l, j: (l, 0, j)),
            pl.BlockSpec((1, 1, tn), lambda l, j: (l, 0, j)),
        ],
        out_specs=pl.BlockSpec((1, SUBLANES_F32, tn), lambda l, j: (l, 0, j)),
        out_shape=jax.ShapeDtypeStruct((n_layers, SUBLANES_F32, n), F32),
        compiler_params=_params(2, 2 * d * tn * 4 + 8 * 1024 * 1024),
        name="ada",
    )(cond, w_ada, b_ada.reshape(n_layers, 1, n))


def _norm_kernel(x_ref, nw_ref, sh_ref, sc_ref, o_ref):
    o_ref[...] = _rms_mod(x_ref[...], nw_ref[...], sh_ref[...], sc_ref[...]).astype(o_ref.dtype)


def _norm_mod(x, nw, sh, sc):
    m, d = x.shape
    tm = min(m, 512)
    vec = pl.BlockSpec((1, d), lambda i: (0, 0))
    return pl.pallas_call(
        _norm_kernel,
        grid=(m // tm,),
        in_specs=[pl.BlockSpec((tm, d), lambda i: (i, 0)), vec, vec, vec],
        out_specs=pl.BlockSpec((tm, d), lambda i: (i, 0)),
        out_shape=jax.ShapeDtypeStruct((m, d), BF16),
        compiler_params=_params(1, 6 * tm * d * 4),
        name="norm_mod",
    )(x, nw, sh, sc)


def _halo_kernel(xp_ref, xn_ref, nw_ref, sh_ref, sc_ref, o_ref):
    i = pl.program_id(0)
    n = pl.num_programs(0)
    hp = _rms_mod(xp_ref[...], nw_ref[...], sh_ref[...], sc_ref[...])
    hn = _rms_mod(xn_ref[...], nw_ref[...], sh_ref[...], sc_ref[...])
    hp = jnp.where(i > 0, hp, 0.0)
    hn = jnp.where(i < n - 1, hn, 0.0)
    o_ref[0] = jnp.concatenate([hn, hp], axis=0).astype(o_ref.dtype)


def _halo_norm(x, nw, sh, sc, tm):
    m, d = x.shape
    nblk = m // tm
    r = SUBLANES_F32
    per = tm // r
    last = m // r - 1
    vec = pl.BlockSpec((1, d), lambda i: (0, 0))
    return pl.pallas_call(
        _halo_kernel,
        grid=(nblk,),
        in_specs=[
            pl.BlockSpec((r, d), lambda i: (jnp.maximum(i * per - 1, 0), 0)),
            pl.BlockSpec((r, d), lambda i: (jnp.minimum((i + 1) * per, last), 0)),
            vec, vec, vec,
        ],
        out_specs=pl.BlockSpec((1, 2 * r, d), lambda i: (i, 0, 0)),
        out_shape=jax.ShapeDtypeStruct((nblk, 2 * r, d), BF16),
        compiler_params=_params(1, 16 * 1024 * 1024),
        name="halo_norm",
    )(x, x, nw, sh, sc)


def _cast_spec(shape, layer, block_of_step):
    _, rows, cols = shape
    n_steps, step_of = block_of_step
    rb = max(SUBLANES_BF16, rows // n_steps // SUBLANES_BF16 * SUBLANES_BF16)
    while rows % rb:
        rb += SUBLANES_BF16
    n_blk = rows // rb

    def blk(i, j):
        return step_of(i, j) * n_blk // n_steps

    return (pl.BlockSpec((None, rb, cols), lambda i, j: (layer, blk(i, j), 0)),
            pl.BlockSpec((rb, cols), lambda i, j: (blk(i, j), 0)),
            jax.ShapeDtypeStruct((rows, cols), BF16))


def _inproj_kernel(*refs, j_off, qk_lo, qk_hi, n_cast):
    a_ref, b_ref, nw_ref = refs[:3]
    o_ref = refs[3 + n_cast]
    for src, dst in zip(refs[3:3 + n_cast], refs[4 + n_cast:4 + 2 * n_cast]):
        dst[...] = src[...].astype(dst.dtype)
    acc = _dot(a_ref[...], b_ref[...].astype(BF16))
    j = pl.program_id(1) + j_off
    is_qk = jnp.logical_and(j >= qk_lo, j < qk_hi)

    @pl.when(is_qk)
    def _():
        for h in range(acc.shape[1] // HEAD_DIM):
            sl = slice(h * HEAD_DIM, (h + 1) * HEAD_DIM)
            xh = acc[:, sl]
            ms = jnp.mean(xh * xh, axis=-1, keepdims=True)
            o_ref[:, sl] = (xh * lax.rsqrt(ms + EPS) * nw_ref[0, :, sl]).astype(o_ref.dtype)

    @pl.when(jnp.logical_not(is_qk))
    def _():
        o_ref[...] = acc.astype(o_ref.dtype)


def _in_proj(h, w_in, layer, nw_tiles, tn, j_off, n_tiles, qk_lo, qk_hi, casts=()):
    m, d = h.shape
    tm = min(m, 1024)
    steps = (m // tm * n_tiles, lambda i, j: i * n_tiles + j)
    cast_specs = [_cast_spec(w.shape, layer, steps) for w in casts]
    cast_bytes = sum(6 * 2 * s[1].block_shape[0] * s[1].block_shape[1] for s in cast_specs)
    out = pl.pallas_call(
        functools.partial(_inproj_kernel, j_off=j_off, qk_lo=qk_lo, qk_hi=qk_hi, n_cast=len(casts)),
        grid=(m // tm, n_tiles),
        in_specs=[
            pl.BlockSpec((tm, d), lambda i, j: (i, 0)),
            pl.BlockSpec((None, d, tn), lambda i, j: (layer, 0, j + j_off)),
            pl.BlockSpec((1, 1, tn), lambda i, j: (j + j_off, 0, 0)),
        ] + [s[0] for s in cast_specs],
        out_specs=[pl.BlockSpec((tm, tn), lambda i, j: (i, j))] + [s[1] for s in cast_specs],
        out_shape=[jax.ShapeDtypeStruct((m, n_tiles * tn), BF16)] + [s[2] for s in cast_specs],
        compiler_params=_params(2, 4 * tm * d + 10 * d * tn + 4 * tm * tn + 4 * tm * tn * 4 + cast_bytes),
        name="in_proj",
    )(h, w_in, nw_tiles, *casts)
    return out[0], out[1:]


def _nat_kernel(q_ref, k_ref, v_ref, kc_ref, vc_ref, bias_ref, o_ref, *, grid_rows, scale, heads):
    b = pl.program_id(1)
    ks = jnp.clip(b * ATTN_ROWS - WIN_ROWS // 2, 0, grid_rows - ATTN_KEY_ROWS)
    start = pl.multiple_of(ks * GRID_W, GRID_W)
    nkeys = ATTN_KEY_ROWS * GRID_W
    for h in range(heads):
        sl = slice(h * HEAD_DIM, (h + 1) * HEAD_DIM)
        q = q_ref[:, sl]
        kw = k_ref[pl.ds(start, nkeys), sl]
        vw = v_ref[pl.ds(start, nkeys), sl]
        s = _dot_nt(q, kw) * scale + bias_ref[0, h]
        sc = _dot_nt(q, kc_ref[:, sl]) * scale
        m = jnp.maximum(jnp.max(s, axis=-1, keepdims=True), jnp.max(sc, axis=-1, keepdims=True))
        p = jnp.exp(s - m)
        pc = jnp.exp(sc - m)
        l = jnp.sum(p, axis=-1, keepdims=True) + jnp.sum(pc, axis=-1, keepdims=True)
        o = _dot(p.astype(BF16), vw) + _dot(pc.astype(BF16), vc_ref[:, sl])
        o_ref[:, sl] = (o / l).astype(o_ref.dtype)


def _nat_bias(rpb, grid_rows):
    n_heads = rpb.shape[0]
    nb = grid_rows // ATTN_ROWS
    c = np.arange(GRID_W)[:, None]
    kc = np.arange(GRID_W)[None, :]
    c_start = np.clip(c - WIN_COLS // 2, 0, GRID_W - WIN_COLS)
    col_ok = (kc >= c_start) & (kc < c_start + WIN_COLS)
    dc = kc - c + (WIN_COLS - 1)
    n_dc = 2 * WIN_COLS - 1
    expand = ((dc[None] == np.arange(n_dc)[:, None, None]) & col_ok[None]).astype(np.float32)
    blocks = jnp.einsum("hrd,dn->hrn", rpb, expand.reshape(n_dc, GRID_W * GRID_W),
                        precision=lax.Precision.HIGHEST)
    blocks = blocks.reshape(n_heads, 2 * WIN_ROWS - 1, GRID_W, GRID_W)
    blocks = jnp.where(col_ok, blocks, NEG_BIAS)
    masked = jnp.full((n_heads, GRID_W, GRID_W), NEG_BIAS, F32)
    variants = []
    for b in (0, 1, nb - 1):
        ks = int(np.clip(b * ATTN_ROWS - WIN_ROWS // 2, 0, grid_rows - ATTN_KEY_ROWS))
        q_rows = []
        for r_loc in range(ATTN_ROWS):
            r = b * ATTN_ROWS + r_loc
            r_start = int(np.clip(r - WIN_ROWS // 2, 0, grid_rows - WIN_ROWS))
            row = []
            for kr in range(ks, ks + ATTN_KEY_ROWS):
                ok = r_start <= kr < r_start + WIN_ROWS
                row.append(blocks[:, kr - r + (WIN_ROWS - 1)] if ok else masked)
            q_rows.append(jnp.concatenate(row, axis=-1))
        variants.append(jnp.concatenate(q_rows, axis=-2))
    return jnp.stack(variants)


def _nat_attention(p_lat, p_ctx, bias, layer, n_heads, q_blk, k_blk, v_blk, kc_blk, vc_blk, hs):
    s_len = p_lat.shape[0]
    n_ctx = p_ctx.shape[0]
    grid_rows = s_len // GRID_W
    nb = grid_rows // ATTN_ROWS
    nq = ATTN_ROWS * GRID_W
    nk = ATTN_KEY_ROWS * GRID_W
    assert grid_rows % ATTN_ROWS == 0 and nb >= 3 and grid_rows >= ATTN_KEY_ROWS
    while n_heads % hs or any(blk % hs for blk in (q_blk, k_blk, v_blk, kc_blk, vc_blk)):
        hs //= 2
    width = hs * HEAD_DIM

    def variant(b):
        return jnp.where(b == 0, 0, jnp.where(b == nb - 1, 2, 1))

    def cols(rows, blk):
        mode = pl.Buffered(1 if (rows == s_len and hs > 2) else 2)
        return pl.BlockSpec((rows, width), lambda h, b: (0, blk // hs + h), pipeline_mode=mode)

    return pl.pallas_call(
        functools.partial(_nat_kernel, grid_rows=grid_rows, scale=HEAD_DIM ** -0.5, heads=hs),
        grid=(n_heads // hs, nb),
        in_specs=[
            pl.BlockSpec((nq, width), lambda h, b: (b, q_blk // hs + h)),
            cols(s_len, k_blk), cols(s_len, v_blk), cols(n_ctx, kc_blk), cols(n_ctx, vc_blk),
            pl.BlockSpec((1, hs, nq, nk), lambda h, b: (variant(b), layer * (n_heads // hs) + h, 0, 0)),
        ],
        out_specs=pl.BlockSpec((nq, width), lambda h, b: (b, h)),
        out_shape=jax.ShapeDtypeStruct((s_len, n_heads * HEAD_DIM), BF16),
        compiler_params=_params(2, 8 * s_len * width + hs * (2 * nq * nk * 4 + 16 * nq * (nk + n_ctx) * 4)),
        name="nat_attention",
    )(p_lat, p_lat, p_lat, p_ctx, p_ctx, bias)


def _ctx_attn_kernel(q_ref, k_ref, v_ref, o_ref, *, scale):
    s = _dot_nt(q_ref[...], k_ref[...]) * scale
    m = jnp.max(s, axis=-1, keepdims=True)
    p = jnp.exp(s - m)
    l = jnp.sum(p, axis=-1, keepdims=True)
    o_ref[...] = (_dot(p.astype(BF16), v_ref[...]) / l).astype(o_ref.dtype)


def _ctx_attention(p_ctx, n_heads, q_blk, k_blk, v_blk):
    n = p_ctx.shape[0]
    return pl.pallas_call(
        functools.partial(_ctx_attn_kernel, scale=HEAD_DIM ** -0.5),
        grid=(n_heads,),
        in_specs=[
            pl.BlockSpec((n, HEAD_DIM), lambda h: (0, q_blk + h)),
            pl.BlockSpec((n, HEAD_DIM), lambda h: (0, k_blk + h)),
            pl.BlockSpec((n, HEAD_DIM), lambda h: (0, v_blk + h)),
        ],
        out_specs=pl.BlockSpec((n, HEAD_DIM), lambda h: (0, h)),
        out_shape=jax.ShapeDtypeStruct((n, n_heads * HEAD_DIM), BF16),
        compiler_params=_params(1, 16 * 1024 * 1024),
        name="ctx_attention",
    )(p_ctx, p_ctx, p_ctx)


def _pool_kernel(up_ref, um_ref, un_ref, pw_ref, ps_ref, o_ref, *, seq_len, gc):
    i = pl.program_id(0)
    n = pl.num_programs(0)
    tp = um_ref.shape[0]
    halo = up_ref.shape[0]
    um = um_ref[...]
    up = jnp.where(i > 0, up_ref[...], jnp.zeros_like(up_ref[...]))
    un = jnp.where(i < n - 1, un_ref[...], jnp.zeros_like(un_ref[...]))
    ue = jnp.concatenate([up, um, un], axis=0)
    trow = lax.broadcasted_iota(jnp.int32, (tp, tp + 2 * halo), 0)
    jcol = lax.broadcasted_iota(jnp.int32, (tp, tp + 2 * halo), 1)
    off = jcol - trow - halo
    t_abs = i * tp + lax.broadcasted_iota(jnp.int32, (tp, gc), 0)
    for g, w in enumerate(POOL_WINDOWS):
        lo_off = -(w // 2)
        hi_off = w - 1 - w // 2
        band = jnp.where(off >= lo_off, jnp.where(off <= hi_off, 1.0, 0.0), 0.0).astype(BF16)
        sl = slice(g * gc, (g + 1) * gc)
        wsum = _dot(band, ue[:, sl])
        lo = jnp.maximum(t_abs + lo_off, 0)
        hi = jnp.minimum(t_abs + hi_off, seq_len - 1)
        cnt = (hi - lo + 1).astype(F32)
        d = (wsum / cnt - um[:, sl].astype(F32)).astype(BF16)
        o_ref[:, sl] = (_dot(d, pw_ref[g]) * ps_ref[:, sl]).astype(o_ref.dtype)


def _pool_mix(p, col_blk, pool_w, pool_scale):
    m = p.shape[0]
    gc = pool_w.shape[-1]
    width = N_GROUPS * gc
    tp = min(m, 512)
    halo = SUBLANES_BF16
    per = tp // halo
    last = m // halo - 1
    return pl.pallas_call(
        functools.partial(_pool_kernel, seq_len=m, gc=gc),
        grid=(m // tp,),
        in_specs=[
            pl.BlockSpec((halo, width), lambda i: (jnp.maximum(i * per - 1, 0), col_blk)),
            pl.BlockSpec((tp, width), lambda i: (i, col_blk)),
            pl.BlockSpec((halo, width), lambda i: (jnp.minimum((i + 1) * per, last), col_blk)),
            pl.BlockSpec((N_GROUPS, gc, gc), lambda i: (0, 0, 0)),
            pl.BlockSpec((1, width), lambda i: (0, 0)),
        ],
        out_specs=pl.BlockSpec((tp, width), lambda i: (i, 0)),
        out_shape=jax.ShapeDtypeStruct((m, width), BF16),
        compiler_params=_params(1, 32 * 1024 * 1024),
        name="pool_mix",
    )(p, p, p, pool_w, pool_scale)


def _cos_sin(num, den):
    ang = (num % den).astype(F32) * (2.0 * math.pi / den)
    return jnp.cos(ang), jnp.sin(ang)


def _dft_tables(n):
    idx = jnp.arange(n, dtype=jnp.int32)
    return _cos_sin(idx[:, None] * idx[None, :], n)


def _fnet_chan_kernel(u_ref, cs_ref, o_ref, *, gc):
    for g in range(N_GROUPS):
        sl = slice(g * gc, (g + 1) * gc)
        o_ref[:, sl] = _dot(u_ref[:, sl], cs_ref[0]).astype(o_ref.dtype)


def _fnet_pos_kernel(t_ref, ab_ref, fw_ref, o_ref, *, scale, gc):
    acc = _dot(t_ref[...], ab_ref[...])
    for g in range(N_GROUPS):
        sl = slice(g * gc, (g + 1) * gc)
        f = (acc[:, sl] * scale).astype(BF16)
        o_ref[:, sl] = _dot(f, fw_ref[g]).astype(o_ref.dtype)


def _fourier_mix_dense(p, col_blk, cs_chan, fnet_w):
    m = p.shape[0]
    gc = fnet_w.shape[-1]
    width = N_GROUPS * gc
    cpos, spos = _dft_tables(m)
    t_pos = jnp.concatenate([cpos, -spos], axis=1).astype(BF16)
    ab = pl.pallas_call(
        functools.partial(_fnet_chan_kernel, gc=gc),
        grid=(2,),
        in_specs=[
            pl.BlockSpec((m, width), lambda c: (0, col_blk)),
            pl.BlockSpec((1, gc, gc), lambda c: (c, 0, 0)),
        ],
        out_specs=pl.BlockSpec((m, width), lambda c: (c, 0)),
        out_shape=jax.ShapeDtypeStruct((2 * m, width), BF16),
        compiler_params=_params(1, 32 * 1024 * 1024),
        name="fnet_chan",
    )(p, cs_chan)
    return pl.pallas_call(
        functools.partial(_fnet_pos_kernel, scale=1.0 / math.sqrt(m * gc), gc=gc),
        grid=(1,),
        in_specs=[
            pl.BlockSpec((m, 2 * m), lambda i: (0, 0)),
            pl.BlockSpec((2 * m, width), lambda i: (0, 0)),
            pl.BlockSpec((N_GROUPS, gc, gc), lambda i: (0, 0, 0)),
        ],
        out_specs=pl.BlockSpec((m, width), lambda i: (0, 0)),
        out_shape=jax.ShapeDtypeStruct((m, width), BF16),
        compiler_params=_params(1, 32 * 1024 * 1024),
        name="fnet_pos",
    )(t_pos, ab, fnet_w)


def _fft_stage1_kernel(f1_ref, u_ref, tc_ref, ts_ref, zr_ref, zi_ref, *, n1):
    width = zr_ref.shape[1]
    reps = width // LANES
    for t in range(u_ref.shape[1] // width):
        y = _dot(f1_ref[...], u_ref[:, t * width:(t + 1) * width])
        yr, yi = y[:n1], y[n1:]
        tc = jnp.tile(tc_ref[:, t * LANES:(t + 1) * LANES], (1, reps))
        ts = jnp.tile(ts_ref[:, t * LANES:(t + 1) * LANES], (1, reps))
        zr_ref[t * n1:(t + 1) * n1, :] = (yr * tc + yi * ts).astype(zr_ref.dtype)
        zi_ref[t * n1:(t + 1) * n1, :] = (yi * tc - yr * ts).astype(zi_ref.dtype)


def _fft_stage2_kernel(fa_ref, fb_ref, zr_ref, zi_ref, cc_ref, sc_ref, fw_ref, o_ref, *, n2, gc, chunks, scale):
    width = N_GROUPS * gc
    p = _dot(fa_ref[...], zr_ref[...]) + _dot(fb_ref[...], zi_ref[...])
    for g in range(N_GROUPS):
        cols = [slice(ch * width + g * gc, ch * width + (g + 1) * gc) for ch in range(chunks)]
        pr = jnp.concatenate([p[:n2, c] for c in cols], axis=0).astype(BF16)
        pi = jnp.concatenate([p[n2:, c] for c in cols], axis=0).astype(BF16)
        f = ((_dot(pr, cc_ref[...]) + _dot(pi, sc_ref[...])) * scale).astype(BF16)
        y = _dot(f, fw_ref[g]).astype(o_ref.dtype)
        for ch, c in enumerate(cols):
            o_ref[:, c] = y[ch * n2:(ch + 1) * n2]


def _fourier_mix_fft(p, col_blk, cs_chan, fnet_w):
    m, d_in = p.shape
    gc = fnet_w.shape[-1]
    width = N_GROUPS * gc
    n2 = LANES
    n1 = m // n2
    i1 = jnp.arange(n1, dtype=jnp.int32)
    i2 = jnp.arange(n2, dtype=jnp.int32)
    c1, s1 = _cos_sin(i1[:, None] * i1[None, :], n1)
    f1 = jnp.concatenate([c1, -s1], axis=0).astype(BF16)
    tc, ts = _cos_sin(i1[:, None] * i2[None, :], m)
    tc = jnp.repeat(tc, LANES, axis=1)
    ts = jnp.repeat(ts, LANES, axis=1)
    c2, s2 = _cos_sin(i2[:, None] * i2[None, :], n2)
    fa = jnp.concatenate([c2, -s2], axis=0).astype(BF16)
    fb = jnp.concatenate([s2, c2], axis=0).astype(BF16)
    u = p[:, col_blk * width:(col_blk + 1) * width].reshape(n1, n2 * width)
    z_shape = jax.ShapeDtypeStruct((n2 * n1, width), BF16)
    sb = FFT_STAGE1_COLS
    zr, zi = pl.pallas_call(
        functools.partial(_fft_stage1_kernel, n1=n1),
        grid=(n2 // sb,),
        in_specs=[
            pl.BlockSpec((2 * n1, n1), lambda j: (0, 0)),
            pl.BlockSpec((n1, sb * width), lambda j: (0, j)),
            pl.BlockSpec((n1, sb * LANES), lambda j: (0, j)),
            pl.BlockSpec((n1, sb * LANES), lambda j: (0, j)),
        ],
        out_specs=[pl.BlockSpec((sb * n1, width), lambda j: (j, 0))] * 2,
        out_shape=[z_shape, z_shape],
        compiler_params=_params(1, 16 * 1024 * 1024),
        name="fft_stage1",
    )(f1, u, tc, ts)
    chunks = min(n1, 8)
    tn = chunks * width
    y = pl.pallas_call(
        functools.partial(_fft_stage2_kernel, n2=n2, gc=gc, chunks=chunks, scale=1.0 / math.sqrt(m * gc)),
        grid=(n1 // chunks,),
        in_specs=[
            pl.BlockSpec((2 * n2, n2), lambda j: (0, 0)),
            pl.BlockSpec((2 * n2, n2), lambda j: (0, 0)),
            pl.BlockSpec((n2, tn), lambda j: (0, j)),
            pl.BlockSpec((n2, tn), lambda j: (0, j)),
            pl.BlockSpec((None, gc, gc), lambda j: (0, 0, 0)),
            pl.BlockSpec((None, gc, gc), lambda j: (1, 0, 0)),
            pl.BlockSpec((N_GROUPS, gc, gc), lambda j: (0, 0, 0)),
        ],
        out_specs=pl.BlockSpec((n2, tn), lambda j: (0, j)),
        out_shape=jax.ShapeDtypeStruct((n2, n1 * width), BF16),
        compiler_params=_params(1, 40 * 1024 * 1024),
        name="fft_stage2",
    )(fa, fb, zr.reshape(n2, n1 * width), zi.reshape(n2, n1 * width), cs_chan, cs_chan, fnet_w)
    return y.reshape(m, width)


def _fourier_mix(p, col_blk, cs_chan, fnet_w):
    if p.shape[0] >= FFT_MIN_LEN and p.shape[0] % (LANES * SUBLANES_F32) == 0:
        return _fourier_mix_fft(p, col_blk, cs_chan, fnet_w)
    return _fourier_mix_dense(p, col_blk, cs_chan, fnet_w)


def _merge_kernel(h_ref, yp_ref, ya_ref, yf_ref, wg0, wg1, wg2, bg0, bg1, bg2, wp, wa, wf, o_ref):
    h = h_ref[...]

    def gate(w_ref, b_ref):
        return jax.nn.sigmoid(_dot(h, w_ref[...]) + b_ref[...])

    m = gate(wg0, bg0) * _dot(yp_ref[...], wp[...])
    m = m + gate(wg1, bg1) * _dot(ya_ref[...], wa[...])
    m = m + gate(wg2, bg2) * _dot(yf_ref[...], wf[...])
    o_ref[...] = m.astype(o_ref.dtype)


def _merge(h, y_pool, y_attn, y_fnet, layer, w_gate, b_gate, w_bp, w_ba, w_bf):
    m, d = h.shape
    dp, da, df = y_pool.shape[1], y_attn.shape[1], y_fnet.shape[1]
    tm = min(m, 1024)
    tn = _tile(d, 256)
    nj = d // tn

    def act(width):
        bufs = 1 if width == d else 2
        return pl.BlockSpec((tm, width), lambda i, j: (i, 0), pipeline_mode=pl.Buffered(bufs))

    def gate_w(b):
        return pl.BlockSpec((None, d, tn), lambda i, j: (layer, 0, b * nj + j))

    def gate_b(b):
        return pl.BlockSpec((None, 1, tn), lambda i, j: (layer, 0, b * nj + j))

    def br_w(k):
        return pl.BlockSpec((None, k, tn), lambda i, j: (layer, 0, j))

    vmem = 4 * tm * (d + dp + da + df) + 4 * tn * (3 * d + dp + da + df) + 12 * tm * tn * 4
    return pl.pallas_call(
        _merge_kernel,
        grid=(m // tm, nj),
        in_specs=[act(d), act(dp), act(da), act(df), gate_w(0), gate_w(1), gate_w(2),
                  gate_b(0), gate_b(1), gate_b(2), br_w(dp), br_w(da), br_w(df)],
        out_specs=pl.BlockSpec((tm, tn), lambda i, j: (i, j)),
        out_shape=jax.ShapeDtypeStruct((m, d), BF16),
        compiler_params=_params(2, vmem),
        name="merge",
    )(h, y_pool, y_attn, y_fnet, w_gate, w_gate, w_gate, b_gate, b_gate, b_gate, w_bp, w_ba, w_bf)


def _resid_kernel(a_ref, w_ref, x_ref, g_ref, o_ref):
    o_ref[...] = x_ref[...] + g_ref[...] * _dot(a_ref[...], w_ref[...])


def _resid_proj(a, w, layer, x, g, tm_pref, tn_pref):
    m, kk = a.shape
    n = w.shape[2]
    tm = min(m, tm_pref)
    tn = _tile(n, tn_pref)
    return pl.pallas_call(
        _resid_kernel,
        grid=(m // tm, n // tn),
        in_specs=[
            pl.BlockSpec((tm, kk), lambda i, j: (i, 0)),
            pl.BlockSpec((None, kk, tn), lambda i, j: (layer, 0, j)),
            pl.BlockSpec((tm, tn), lambda i, j: (i, j)),
            pl.BlockSpec((1, tn), lambda i, j: (0, j)),
        ],
        out_specs=pl.BlockSpec((tm, tn), lambda i, j: (i, j)),
        out_shape=jax.ShapeDtypeStruct((m, n), F32),
        compiler_params=_params(2, 4 * tm * kk + 4 * kk * tn + 7 * tm * tn * 4),
        name="resid_proj",
    )(a, w, x, g)


def _ffn_up_kernel(*refs, n_cast):
    a_ref, halo_ref, wg_ref, wv_ref, cwg_ref, cwv_ref, cbg_ref, cbv_ref = refs[:8]
    o_ref = refs[8 + n_cast]
    a_ext = refs[-1]
    for src, dst in zip(refs[8:8 + n_cast], refs[9 + n_cast:9 + 2 * n_cast]):
        dst[...] = src[...].astype(dst.dtype)
    tm = a_ref.shape[0]

    @pl.when(pl.program_id(1) == 0)
    def _():
        a_ext[:tm, :] = a_ref[...]
        a_ext[tm:, :] = halo_ref[0]

    a = a_ext[...]
    row = lax.broadcasted_iota(jnp.int32, (tm, wg_ref.shape[1]), 0)

    def conv(w_ref, cw_ref, cb_ref):
        u = _dot(a, w_ref[...].astype(BF16))
        um = u[:tm]
        next_row = u[tm:tm + 1]
        prev_row = u[tm + 2 * SUBLANES_F32 - 1:tm + 2 * SUBLANES_F32]
        u_dn = jnp.where(row == 0, prev_row, pltpu.roll(um, 1, 0))
        u_up = jnp.where(row == tm - 1, next_row, pltpu.roll(um, tm - 1, 0))
        return u_dn * cw_ref[0:1] + um * cw_ref[1:2] + u_up * cw_ref[2:3] + cb_ref[...]

    gate = conv(wg_ref, cwg_ref, cbg_ref)
    val = conv(wv_ref, cwv_ref, cbv_ref)
    o_ref[...] = (gate * jax.nn.sigmoid(gate) * val).astype(o_ref.dtype)


def _ffn_up(h2, halo, layer, w_up, conv_w, conv_b, casts=()):
    m, d = h2.shape
    f = w_up.shape[2] // 2
    tm = m // halo.shape[0]
    ext = halo.shape[1]
    tn = _tile(f, 256)
    nj = f // tn
    w_bytes = jnp.dtype(w_up.dtype).itemsize
    a_bufs = 1 if w_bytes == 4 else 2
    vmem = (2 * a_bufs * tm * d + 2 * (tm + ext) * d + (4 * w_bytes + 4) * d * tn + 4 * tm * tn
            + 12 * (tm + ext) * tn * 4)

    def cols(rows, half):
        return pl.BlockSpec((None, rows, tn), lambda i, j: (layer, 0, half * nj + j))

    steps = (m // tm * nj, lambda i, j: i * nj + j)
    cast_specs = [_cast_spec(w.shape, layer, steps) for w in casts]
    vmem += sum(6 * 2 * s[1].block_shape[0] * s[1].block_shape[1] for s in cast_specs)
    out = pl.pallas_call(
        functools.partial(_ffn_up_kernel, n_cast=len(casts)),
        grid=(m // tm, nj),
        in_specs=[
            pl.BlockSpec((tm, d), lambda i, j: (i, 0), pipeline_mode=pl.Buffered(a_bufs)),
            pl.BlockSpec((1, ext, d), lambda i, j: (i, 0, 0)),
            cols(d, 0), cols(d, 1), cols(CONV_W, 0), cols(CONV_W, 1), cols(1, 0), cols(1, 1),
        ] + [s[0] for s in cast_specs],
        out_specs=[pl.BlockSpec((tm, tn), lambda i, j: (i, j))] + [s[1] for s in cast_specs],
        out_shape=[jax.ShapeDtypeStruct((m, f), BF16)] + [s[2] for s in cast_specs],
        scratch_shapes=[pltpu.VMEM((tm + ext, d), BF16)],
        compiler_params=_params(2, vmem),
        name="ffn_up",
    )(h2, halo, w_up, w_up, conv_w, conv_w, conv_b, conv_b, *casts)
    return out[0], out[1:]


def kernel(x, c, ctx, c_ctx, w_ada, b_ada, norm1_w, norm2_w, w_in, pool_w, pool_scale, q_norm_w, k_norm_w, rpb, fnet_w, w_gate, b_gate, w_br_pool, w_br_attn, w_br_fnet, w_o, w_up, conv_w, conv_b, w_down):
    n_layers, d, d_in = w_in.shape
    batch, s_len, _ = x.shape
    n_ctx = ctx.shape[1]
    assert batch == 1
    d_pool = w_br_pool.shape[1]
    d_attn = w_br_attn.shape[1]
    d_fnet = w_br_fnet.shape[1]
    n_heads = d_attn // HEAD_DIM
    assert d_in == d_pool + 3 * d_attn + d_fnet and s_len % GRID_W == 0 and d_fnet == d_pool
    q_lo, k_lo, v_lo, f_lo = d_pool, d_pool + d_attn, d_pool + 2 * d_attn, d_pool + 3 * d_attn

    pool_w_b = pool_w.astype(BF16)
    fnet_w_b = fnet_w.astype(BF16)
    b_gate_r = b_gate.reshape(n_layers, 1, -1)
    conv_b_r = conv_b.reshape(n_layers, 1, -1)
    ones_p = jnp.ones((d_pool,), F32)
    ones_vf = jnp.ones((d_attn + d_fnet,), F32)

    cs_chan = jnp.stack(_dft_tables(fnet_w.shape[-1])).astype(BF16)

    cond = jnp.concatenate([c.reshape(1, d), c_ctx.reshape(1, d), jnp.zeros((SUBLANES_F32 - 2, d), F32)], axis=0)
    mod = _ada(cond, w_ada, b_ada)

    bias = _nat_bias(rpb.reshape((n_layers * n_heads,) + rpb.shape[2:]), s_len // GRID_W)

    tm_ffn = min(s_len, 1024)
    blk = lambda col: col // HEAD_DIM
    tn_in = _tile(d_pool, 512)
    assert d_pool % tn_in == 0 and d_attn % tn_in == 0
    qk_tiles = (q_lo // tn_in, v_lo // tn_in)

    xl = x.reshape(s_len, d)
    xc = ctx.reshape(n_ctx, d)
    for l in range(n_layers):
        last = l == n_layers - 1
        sh1, sc1, g1, sh2, sc2, g2 = [mod[l, 0:1, i * d:(i + 1) * d] for i in range(N_MOD)]
        csh1, csc1, cg1, csh2, csc2, cg2 = [mod[l, 1:2, i * d:(i + 1) * d] for i in range(N_MOD)]
        n1w = norm1_w[l].reshape(1, d)
        n2w = norm2_w[l].reshape(1, d)
        nw_tiles = jnp.concatenate(
            [ones_p, jnp.tile(q_norm_w[l], n_heads), jnp.tile(k_norm_w[l], n_heads), ones_vf]
        ).reshape(d_in // tn_in, 1, tn_in)
        pool_scale_l = pool_scale[l].reshape(1, -1)

        h = _norm_mod(xl, n1w, sh1, sc1)
        hc = _norm_mod(xc, n1w, csh1, csc1)
        p, w_bf16 = _in_proj(h, w_in, l, nw_tiles, tn_in, 0, d_in // tn_in, *qk_tiles,
                             casts=(w_gate, w_br_pool, w_br_attn, w_br_fnet, w_o))
        w_gate_b, w_bp_b, w_ba_b, w_bf_b, w_o_b = [w[None] for w in w_bf16]
        b_gate_l = b_gate_r[l:l + 1]
        if last:
            pc, _ = _in_proj(hc, w_in, l, nw_tiles, tn_in, k_lo // tn_in, (f_lo - k_lo) // tn_in, *qk_tiles)
            kc_blk, vc_blk = 0, blk(d_attn)
        else:
            pc, _ = _in_proj(hc, w_in, l, nw_tiles, tn_in, 0, d_in // tn_in, *qk_tiles)
            kc_blk, vc_blk = blk(k_lo), blk(v_lo)
        y_attn = _nat_attention(p, pc, bias, l, n_heads, blk(q_lo), blk(k_lo), blk(v_lo), kc_blk, vc_blk, ATTN_HEADS)
        y_pool = _pool_mix(p, 0, pool_w_b[l], pool_scale_l)
        y_fnet = _fourier_mix(p, f_lo // d_fnet, cs_chan, fnet_w_b[l])
        m = _merge(h, y_pool, y_attn, y_fnet, 0, w_gate_b, b_gate_l, w_bp_b, w_ba_b, w_bf_b)
        xl = _resid_proj(m, w_o_b, 0, xl, g1, 1024, 1024)

        h2 = _norm_mod(xl, n2w, sh2, sc2)
        halo = _halo_norm(xl, n2w, sh2, sc2, tm_ffn)
        t, (w_down_l,) = _ffn_up(h2, halo, l, w_up, conv_w, conv_b_r, casts=(w_down,))
        w_down_b = w_down_l[None]
        xl = _resid_proj(t, w_down_b, 0, xl, g2, 512, 512)

        if not last:
            yc_attn = _ctx_attention(pc, n_heads, blk(q_lo), blk(k_lo), blk(v_lo))
            yc_pool = _pool_mix(pc, 0, pool_w_b[l], pool_scale_l)
            yc_fnet = _fourier_mix(pc, f_lo // d_fnet, cs_chan, fnet_w_b[l])
            mc = _merge(hc, yc_pool, yc_attn, yc_fnet, 0, w_gate_b, b_gate_l, w_bp_b, w_ba_b, w_bf_b)
            xc = _resid_proj(mc, w_o_b, 0, xc, cg1, 1024, 1024)
            hc2 = _norm_mod(xc, n2w, csh2, csc2)
            halo_c = _halo_norm(xc, n2w, csh2, csc2, n_ctx)
            tc, _ = _ffn_up(hc2, halo_c, l, w_up, conv_w, conv_b_r)
            xc = _resid_proj(tc, w_down_b, 0, xc, cg2, 512, 512)
    return xl.reshape(batch, s_len, d)
```

```python
import functools
import math

import numpy as np
import jax
import jax.numpy as jnp
from jax import lax
from jax.experimental import pallas as pl
from jax.experimental.pallas import tpu as pltpu

F32 = jnp.float32
BF16 = jnp.bfloat16

GRID_W = 64
HEAD_DIM = 128
WIN_ROWS = 8
WIN_COLS = 16
POOL_WINDOWS = (2, 4, 8, 16)
N_GROUPS = 4
N_MOD = 6
CONV_W = 3
EPS = 1e-6

LANES = 128
SUBLANES_F32 = 8
SUBLANES_BF16 = 16
VMEM_CAP_BYTES = 56 * 1024 * 1024

ATTN_ROWS = 4
ATTN_KEY_ROWS = ATTN_ROWS + WIN_ROWS
ATTN_HEADS = 8
FFT_STAGE1_COLS = 16
EPILOGUE_ROWS = 256
NEG_BIAS = -1e30
FFT_MIN_LEN = 1024


def _tile(n, pref):
    if n <= pref:
        return n
    t = pref - pref % LANES
    while t >= LANES:
        if n % t == 0:
            return t
        t -= LANES
    return n


def _params(n_axes, vmem_bytes):
    return pltpu.CompilerParams(
        dimension_semantics=("arbitrary",) * n_axes,
        vmem_limit_bytes=int(min(VMEM_CAP_BYTES, max(vmem_bytes, 16 * 1024 * 1024))),
    )


def _dot(a, b):
    return jnp.dot(a, b, preferred_element_type=F32)


def _dot_nt(a, b):
    return lax.dot_general(a, b, (((1,), (1,)), ((), ())), preferred_element_type=F32)


def _rms_mod(x, nw, sh, sc):
    ms = jnp.mean(x * x, axis=-1, keepdims=True)
    return (x * lax.rsqrt(ms + EPS) * nw) * (1.0 + sc) + sh


def _ada_kernel(s_ref, w_ref, b_ref, o_ref):
    s = s_ref[...]
    a = (s * jax.nn.sigmoid(s)).astype(BF16)
    o_ref[0] = _dot(a, w_ref[0].astype(BF16)) + b_ref[0]


def _ada(cond, w_ada, b_ada):
    n_layers, d, n = w_ada.shape
    tn = _tile(n, 1024)
    return pl.pallas_call(
        _ada_kernel,
        grid=(n_layers, n // tn),
        in_specs=[
            pl.BlockSpec((SUBLANES_F32, d), lambda l, j: (0, 0)),
            pl.BlockSpec((1, d, tn), lambda l, j: (l, 0, j)),
            pl.BlockSpec((1, 1, tn), lambda l, j: (l, 0, j)),
        ],
        out_specs=pl.BlockSpec((1, SUBLANES_F32, tn), lambda l, j: (l, 0, j)),
        out_shape=jax.ShapeDtypeStruct((n_layers, SUBLANES_F32, n), F32),
        compiler_params=_params(2, 2 * d * tn * 4 + 8 * 1024 * 1024),
        name="ada",
    )(cond, w_ada, b_ada.reshape(n_layers, 1, n))


def _norm_kernel(x_ref, nw_ref, sh_ref, sc_ref, o_ref):
    o_ref[...] = _rms_mod(x_ref[...], nw_ref[...], sh_ref[...], sc_ref[...]).astype(o_ref.dtype)


def _norm_mod(x, nw, sh, sc):
    m, d = x.shape
    tm = min(m, 512)
    vec = pl.BlockSpec((1, d), lambda i: (0, 0))
    return pl.pallas_call(
        _norm_kernel,
        grid=(m // tm,),
        in_specs=[pl.BlockSpec((tm, d), lambda i: (i, 0)), vec, vec, vec],
        out_specs=pl.BlockSpec((tm, d), lambda i: (i, 0)),
        out_shape=jax.ShapeDtypeStruct((m, d), BF16),
        compiler_params=_params(1, 6 * tm * d * 4),
        name="norm_mod",
    )(x, nw, sh, sc)


def _halo_kernel(xp_ref, xn_ref, nw_ref, sh_ref, sc_ref, o_ref):
    i = pl.program_id(0)
    n = pl.num_programs(0)
    hp = _rms_mod(xp_ref[...], nw_ref[...], sh_ref[...], sc_ref[...])
    hn = _rms_mod(xn_ref[...], nw_ref[...], sh_ref[...], sc_ref[...])
    hp = jnp.where(i > 0, hp, 0.0)
    hn = jnp.where(i < n - 1, hn, 0.0)
    o_ref[0] = jnp.concatenate([hn, hp], axis=0).astype(o_ref.dtype)


def _halo_norm(x, nw, sh, sc, tm):
    m, d = x.shape
    nblk = m // tm
    r = SUBLANES_F32
    per = tm // r
    last = m // r - 1
    vec = pl.BlockSpec((1, d), lambda i: (0, 0))
    return pl.pallas_call(
        _halo_kernel,
        grid=(nblk,),
        in_specs=[
            pl.BlockSpec((r, d), lambda i: (jnp.maximum(i * per - 1, 0), 0)),
            pl.BlockSpec((r, d), lambda i: (jnp.minimum((i + 1) * per, last), 0)),
            vec, vec, vec,
        ],
        out_specs=pl.BlockSpec((1, 2 * r, d), lambda i: (i, 0, 0)),
        out_shape=jax.ShapeDtypeStruct((nblk, 2 * r, d), BF16),
        compiler_params=_params(1, 16 * 1024 * 1024),
        name="halo_norm",
    )(x, x, nw, sh, sc)


def _cast_spec(shape, layer, block_of_step):
    _, rows, cols = shape
    n_steps, step_of = block_of_step
    rb = max(SUBLANES_BF16, rows // n_steps // SUBLANES_BF16 * SUBLANES_BF16)
    while rows % rb:
        rb += SUBLANES_BF16
    n_blk = rows // rb

    def blk(i, j):
        return step_of(i, j) * n_blk // n_steps

    return (pl.BlockSpec((None, rb, cols), lambda i, j: (layer, blk(i, j), 0)),
            pl.BlockSpec((rb, cols), lambda i, j: (blk(i, j), 0)),
            jax.ShapeDtypeStruct((rows, cols), BF16))


def _inproj_kernel(*refs, j_off, qk_lo, qk_hi, n_cast):
    a_ref, b_ref, nw_ref = refs[:3]
    o_ref = refs[3 + n_cast]
    acc0, acc1 = refs[4 + 2 * n_cast:]
    for src, dst in zip(refs[3:3 + n_cast], refs[4 + n_cast:4 + 2 * n_cast]):
        dst[...] = src[...].astype(dst.dtype)
    tm = a_ref.shape[0]
    j = pl.program_id(1)
    nj = pl.num_programs(1) - 1
    tile = j - 1 + j_off
    is_qk = jnp.logical_and(tile >= qk_lo, tile < qk_hi)
    rc = min(tm, EPILOGUE_ROWS)

    def matmul(acc):
        acc[...] = _dot(a_ref[...], b_ref[...].astype(BF16))

    def epilogue(acc):
        for r0 in range(0, tm, rc):
            for h in range(o_ref.shape[1] // HEAD_DIM):
                sl = slice(h * HEAD_DIM, (h + 1) * HEAD_DIM)
                xh = acc[r0:r0 + rc, sl]
                ms = jnp.mean(xh * xh, axis=-1, keepdims=True)
                normed = xh * lax.rsqrt(ms + EPS) * nw_ref[0, :, sl]
                o_ref[r0:r0 + rc, sl] = jnp.where(is_qk, normed, xh).astype(o_ref.dtype)

    odd = lax.rem(j, 2) == 1
    mid = jnp.logical_and(j > 0, j < nj)

    @pl.when(j == 0)
    def _():
        matmul(acc0)

    @pl.when(jnp.logical_and(mid, odd))
    def _():
        epilogue(acc0)
        matmul(acc1)

    @pl.when(jnp.logical_and(mid, jnp.logical_not(odd)))
    def _():
        epilogue(acc1)
        matmul(acc0)

    @pl.when(jnp.logical_and(j == nj, odd))
    def _():
        epilogue(acc0)

    @pl.when(jnp.logical_and(j == nj, jnp.logical_not(odd)))
    def _():
        epilogue(acc1)


def _in_proj(h, w_in, layer, nw_tiles, tn, j_off, n_tiles, qk_lo, qk_hi, casts=()):
    m, d = h.shape
    tm = min(m, 1024)
    acc = pltpu.VMEM((tm, tn), F32)
    steps = (m // tm * n_tiles, lambda i, j: i * n_tiles + jnp.minimum(j, n_tiles - 1))
    cast_specs = [_cast_spec(w.shape, layer, steps) for w in casts]
    cast_bytes = sum(6 * 2 * s[1].block_shape[0] * s[1].block_shape[1] for s in cast_specs)
    out = pl.pallas_call(
        functools.partial(_inproj_kernel, j_off=j_off, qk_lo=qk_lo, qk_hi=qk_hi, n_cast=len(casts)),
        grid=(m // tm, n_tiles + 1),
        in_specs=[
            pl.BlockSpec((tm, d), lambda i, j: (i, 0)),
            pl.BlockSpec((None, d, tn), lambda i, j: (layer, 0, jnp.minimum(j, n_tiles - 1) + j_off)),
            pl.BlockSpec((1, 1, tn), lambda i, j: (jnp.maximum(j - 1, 0) + j_off, 0, 0)),
        ] + [s[0] for s in cast_specs],
        out_specs=[pl.BlockSpec((tm, tn), lambda i, j: (i, jnp.maximum(j - 1, 0)))] + [s[1] for s in cast_specs],
        out_shape=[jax.ShapeDtypeStruct((m, n_tiles * tn), BF16)] + [s[2] for s in cast_specs],
        scratch_shapes=[acc, acc],
        compiler_params=_params(2, 4 * tm * d + 10 * d * tn + 4 * tm * tn + 6 * tm * tn * 4 + cast_bytes),
        name="in_proj",
    )(h, w_in, nw_tiles, *casts)
    return out[0], out[1:]


def _nat_kernel(q_ref, k_ref, v_ref, kc_ref, vc_ref, bias_ref, o_ref, *, grid_rows, scale, heads):
    b = pl.program_id(1)
    ks = jnp.clip(b * ATTN_ROWS - WIN_ROWS // 2, 0, grid_rows - ATTN_KEY_ROWS)
    start = pl.multiple_of(ks * GRID_W, GRID_W)
    nkeys = ATTN_KEY_ROWS * GRID_W
    for h in range(heads):
        sl = slice(h * HEAD_DIM, (h + 1) * HEAD_DIM)
        q = q_ref[:, sl]
        kw = k_ref[pl.ds(start, nkeys), sl]
        vw = v_ref[pl.ds(start, nkeys), sl]
        s = _dot_nt(q, kw) * scale + bias_ref[0, h]
        sc = _dot_nt(q, kc_ref[:, sl]) * scale
        m = jnp.maximum(jnp.max(s, axis=-1, keepdims=True), jnp.max(sc, axis=-1, keepdims=True))
        p = jnp.exp(s - m)
        pc = jnp.exp(sc - m)
        l = jnp.sum(p, axis=-1, keepdims=True) + jnp.sum(pc, axis=-1, keepdims=True)
        o = _dot(p.astype(BF16), vw) + _dot(pc.astype(BF16), vc_ref[:, sl])
        o_ref[:, sl] = (o / l).astype(o_ref.dtype)


def _nat_bias(rpb, grid_rows):
    n_heads = rpb.shape[0]
    nb = grid_rows // ATTN_ROWS
    c = np.arange(GRID_W)[:, None]
    kc = np.arange(GRID_W)[None, :]
    c_start = np.clip(c - WIN_COLS // 2, 0, GRID_W - WIN_COLS)
    col_ok = (kc >= c_start) & (kc < c_start + WIN_COLS)
    dc = kc - c + (WIN_COLS - 1)
    n_dc = 2 * WIN_COLS - 1
    expand = ((dc[None] == np.arange(n_dc)[:, None, None]) & col_ok[None]).astype(np.float32)
    blocks = jnp.einsum("hrd,dn->hrn", rpb, expand.reshape(n_dc, GRID_W * GRID_W),
                        precision=lax.Precision.HIGHEST)
    blocks = blocks.reshape(n_heads, 2 * WIN_ROWS - 1, GRID_W, GRID_W)
    blocks = jnp.where(col_ok, blocks, NEG_BIAS)
    masked = jnp.full((n_heads, GRID_W, GRID_W), NEG_BIAS, F32)
    variants = []
    for b in (0, 1, nb - 1):
        ks = int(np.clip(b * ATTN_ROWS - WIN_ROWS // 2, 0, grid_rows - ATTN_KEY_ROWS))
        q_rows = []
        for r_loc in range(ATTN_ROWS):
            r = b * ATTN_ROWS + r_loc
            r_start = int(np.clip(r - WIN_ROWS // 2, 0, grid_rows - WIN_ROWS))
            row = []
            for kr in range(ks, ks + ATTN_KEY_ROWS):
                ok = r_start <= kr < r_start + WIN_ROWS
                row.append(blocks[:, kr - r + (WIN_ROWS - 1)] if ok else masked)
            q_rows.append(jnp.concatenate(row, axis=-1))
        variants.append(jnp.concatenate(q_rows, axis=-2))
    return jnp.stack(variants)


def _nat_attention(p_lat, p_ctx, bias, layer, n_heads, q_blk, k_blk, v_blk, kc_blk, vc_blk, hs):
    s_len = p_lat.shape[0]
    n_ctx = p_ctx.shape[0]
    grid_rows = s_len // GRID_W
    nb = grid_rows // ATTN_ROWS
    nq = ATTN_ROWS * GRID_W
    nk = ATTN_KEY_ROWS * GRID_W
    assert grid_rows % ATTN_ROWS == 0 and nb >= 3 and grid_rows >= ATTN_KEY_ROWS
    while n_heads % hs or any(blk % hs for blk in (q_blk, k_blk, v_blk, kc_blk, vc_blk)):
        hs //= 2
    width = hs * HEAD_DIM

    def variant(b):
        return jnp.where(b == 0, 0, jnp.where(b == nb - 1, 2, 1))

    def cols(rows, blk):
        mode = pl.Buffered(1 if (rows == s_len and hs > 2) else 2)
        return pl.BlockSpec((rows, width), lambda h, b: (0, blk // hs + h), pipeline_mode=mode)

    return pl.pallas_call(
        functools.partial(_nat_kernel, grid_rows=grid_rows, scale=HEAD_DIM ** -0.5, heads=hs),
        grid=(n_heads // hs, nb),
        in_specs=[
            pl.BlockSpec((nq, width), lambda h, b: (b, q_blk // hs + h)),
            cols(s_len, k_blk), cols(s_len, v_blk), cols(n_ctx, kc_blk), cols(n_ctx, vc_blk),
            pl.BlockSpec((1, hs, nq, nk), lambda h, b: (variant(b), layer * (n_heads // hs) + h, 0, 0)),
        ],
        out_specs=pl.BlockSpec((nq, width), lambda h, b: (b, h)),
        out_shape=jax.ShapeDtypeStruct((s_len, n_heads * HEAD_DIM), BF16),
        compiler_params=_params(2, 8 * s_len * width + hs * (2 * nq * nk * 4 + 16 * nq * (nk + n_ctx) * 4)),
        name="nat_attention",
    )(p_lat, p_lat, p_lat, p_ctx, p_ctx, bias)


def _ctx_attn_kernel(q_ref, k_ref, v_ref, o_ref, *, scale):
    s = _dot_nt(q_ref[...], k_ref[...]) * scale
    m = jnp.max(s, axis=-1, keepdims=True)
    p = jnp.exp(s - m)
    l = jnp.sum(p, axis=-1, keepdims=True)
    o_ref[...] = (_dot(p.astype(BF16), v_ref[...]) / l).astype(o_ref.dtype)


def _ctx_attention(p_ctx, n_heads, q_blk, k_blk, v_blk):
    n = p_ctx.shape[0]
    return pl.pallas_call(
        functools.partial(_ctx_attn_kernel, scale=HEAD_DIM ** -0.5),
        grid=(n_heads,),
        in_specs=[
            pl.BlockSpec((n, HEAD_DIM), lambda h: (0, q_blk + h)),
            pl.BlockSpec((n, HEAD_DIM), lambda h: (0, k_blk + h)),
            pl.BlockSpec((n, HEAD_DIM), lambda h: (0, v_blk + h)),
        ],
        out_specs=pl.BlockSpec((n, HEAD_DIM), lambda h: (0, h)),
        out_shape=jax.ShapeDtypeStruct((n, n_heads * HEAD_DIM), BF16),
        compiler_params=_params(1, 16 * 1024 * 1024),
        name="ctx_attention",
    )(p_ctx, p_ctx, p_ctx)


def _pool_kernel(up_ref, um_ref, un_ref, pw_ref, ps_ref, o_ref, *, seq_len, gc):
    i = pl.program_id(0)
    n = pl.num_programs(0)
    tp = um_ref.shape[0]
    halo = up_ref.shape[0]
    um = um_ref[...]
    up = jnp.where(i > 0, up_ref[...], jnp.zeros_like(up_ref[...]))
    un = jnp.where(i < n - 1, un_ref[...], jnp.zeros_like(un_ref[...]))
    ue = jnp.concatenate([up, um, un], axis=0)
    trow = lax.broadcasted_iota(jnp.int32, (tp, tp + 2 * halo), 0)
    jcol = lax.broadcasted_iota(jnp.int32, (tp, tp + 2 * halo), 1)
    off = jcol - trow - halo
    t_abs = i * tp + lax.broadcasted_iota(jnp.int32, (tp, gc), 0)
    for g, w in enumerate(POOL_WINDOWS):
        lo_off = -(w // 2)
        hi_off = w - 1 - w // 2
        band = jnp.where(off >= lo_off, jnp.where(off <= hi_off, 1.0, 0.0), 0.0).astype(BF16)
        sl = slice(g * gc, (g + 1) * gc)
        wsum = _dot(band, ue[:, sl])
        lo = jnp.maximum(t_abs + lo_off, 0)
        hi = jnp.minimum(t_abs + hi_off, seq_len - 1)
        cnt = (hi - lo + 1).astype(F32)
        d = (wsum / cnt - um[:, sl].astype(F32)).astype(BF16)
        o_ref[:, sl] = (_dot(d, pw_ref[g]) * ps_ref[:, sl]).astype(o_ref.dtype)


def _pool_mix(p, col_blk, pool_w, pool_scale):
    m = p.shape[0]
    gc = pool_w.shape[-1]
    width = N_GROUPS * gc
    tp = min(m, 512)
    halo = SUBLANES_BF16
    per = tp // halo
    last = m // halo - 1
    return pl.pallas_call(
        functools.partial(_pool_kernel, seq_len=m, gc=gc),
        grid=(m // tp,),
        in_specs=[
            pl.BlockSpec((halo, width), lambda i: (jnp.maximum(i * per - 1, 0), col_blk)),
            pl.BlockSpec((tp, width), lambda i: (i, col_blk)),
            pl.BlockSpec((halo, width), lambda i: (jnp.minimum((i + 1) * per, last), col_blk)),
            pl.BlockSpec((N_GROUPS, gc, gc), lambda i: (0, 0, 0)),
            pl.BlockSpec((1, width), lambda i: (0, 0)),
        ],
        out_specs=pl.BlockSpec((tp, width), lambda i: (i, 0)),
        out_shape=jax.ShapeDtypeStruct((m, width), BF16),
        compiler_params=_params(1, 32 * 1024 * 1024),
        name="pool_mix",
    )(p, p, p, pool_w, pool_scale)


def _cos_sin(num, den):
    ang = (num % den).astype(F32) * (2.0 * math.pi / den)
    return jnp.cos(ang), jnp.sin(ang)


def _dft_tables(n):
    idx = jnp.arange(n, dtype=jnp.int32)
    return _cos_sin(idx[:, None] * idx[None, :], n)


def _fnet_chan_kernel(u_ref, cs_ref, o_ref, *, gc):
    for g in range(N_GROUPS):
        sl = slice(g * gc, (g + 1) * gc)
        o_ref[:, sl] = _dot(u_ref[:, sl], cs_ref[0]).astype(o_ref.dtype)


def _fnet_pos_kernel(t_ref, ab_ref, fw_ref, o_ref, *, scale, gc):
    acc = _dot(t_ref[...], ab_ref[...])
    for g in range(N_GROUPS):
        sl = slice(g * gc, (g + 1) * gc)
        f = (acc[:, sl] * scale).astype(BF16)
        o_ref[:, sl] = _dot(f, fw_ref[g]).astype(o_ref.dtype)


def _fourier_mix_dense(p, col_blk, cs_chan, fnet_w):
    m = p.shape[0]
    gc = fnet_w.shape[-1]
    width = N_GROUPS * gc
    cpos, spos = _dft_tables(m)
    t_pos = jnp.concatenate([cpos, -spos], axis=1).astype(BF16)
    ab = pl.pallas_call(
        functools.partial(_fnet_chan_kernel, gc=gc),
        grid=(2,),
        in_specs=[
            pl.BlockSpec((m, width), lambda c: (0, col_blk)),
            pl.BlockSpec((1, gc, gc), lambda c: (c, 0, 0)),
        ],
        out_specs=pl.BlockSpec((m, width), lambda c: (c, 0)),
        out_shape=jax.ShapeDtypeStruct((2 * m, width), BF16),
        compiler_params=_params(1, 32 * 1024 * 1024),
        name="fnet_chan",
    )(p, cs_chan)
    return pl.pallas_call(
        functools.partial(_fnet_pos_kernel, scale=1.0 / math.sqrt(m * gc), gc=gc),
        grid=(1,),
        in_specs=[
            pl.BlockSpec((m, 2 * m), lambda i: (0, 0)),
            pl.BlockSpec((2 * m, width), lambda i: (0, 0)),
            pl.BlockSpec((N_GROUPS, gc, gc), lambda i: (0, 0, 0)),
        ],
        out_specs=pl.BlockSpec((m, width), lambda i: (0, 0)),
        out_shape=jax.ShapeDtypeStruct((m, width), BF16),
        compiler_params=_params(1, 32 * 1024 * 1024),
        name="fnet_pos",
    )(t_pos, ab, fnet_w)


def _fft_stage1_kernel(f1_ref, u_ref, tc_ref, ts_ref, zr_ref, zi_ref, *, n1):
    width = zr_ref.shape[1]
    reps = width // LANES
    for t in range(u_ref.shape[1] // width):
        y = _dot(f1_ref[...], u_ref[:, t * width:(t + 1) * width])
        yr, yi = y[:n1], y[n1:]
        tc = jnp.tile(tc_ref[:, t * LANES:(t + 1) * LANES], (1, reps))
        ts = jnp.tile(ts_ref[:, t * LANES:(t + 1) * LANES], (1, reps))
        zr_ref[t * n1:(t + 1) * n1, :] = (yr * tc + yi * ts).astype(zr_ref.dtype)
        zi_ref[t * n1:(t + 1) * n1, :] = (yi * tc - yr * ts).astype(zi_ref.dtype)


def _fft_stage2_kernel(fa_ref, fb_ref, zr_ref, zi_ref, cc_ref, sc_ref, fw_ref, o_ref, *, n2, gc, chunks, scale):
    width = N_GROUPS * gc
    p = _dot(fa_ref[...], zr_ref[...]) + _dot(fb_ref[...], zi_ref[...])
    for g in range(N_GROUPS):
        cols = [slice(ch * width + g * gc, ch * width + (g + 1) * gc) for ch in range(chunks)]
        pr = jnp.concatenate([p[:n2, c] for c in cols], axis=0).astype(BF16)
        pi = jnp.concatenate([p[n2:, c] for c in cols], axis=0).astype(BF16)
        f = ((_dot(pr, cc_ref[...]) + _dot(pi, sc_ref[...])) * scale).astype(BF16)
        y = _dot(f, fw_ref[g]).astype(o_ref.dtype)
        for ch, c in enumerate(cols):
            o_ref[:, c] = y[ch * n2:(ch + 1) * n2]


def _fourier_mix_fft(p, col_blk, cs_chan, fnet_w):
    m, d_in = p.shape
    gc = fnet_w.shape[-1]
    width = N_GROUPS * gc
    n2 = LANES
    n1 = m // n2
    i1 = jnp.arange(n1, dtype=jnp.int32)
    i2 = jnp.arange(n2, dtype=jnp.int32)
    c1, s1 = _cos_sin(i1[:, None] * i1[None, :], n1)
    f1 = jnp.concatenate([c1, -s1], axis=0).astype(BF16)
    tc, ts = _cos_sin(i1[:, None] * i2[None, :], m)
    tc = jnp.repeat(tc, LANES, axis=1)
    ts = jnp.repeat(ts, LANES, axis=1)
    c2, s2 = _cos_sin(i2[:, None] * i2[None, :], n2)
    fa = jnp.concatenate([c2, -s2], axis=0).astype(BF16)
    fb = jnp.concatenate([s2, c2], axis=0).astype(BF16)
    u = p[:, col_blk * width:(col_blk + 1) * width].reshape(n1, n2 * width)
    z_shape = jax.ShapeDtypeStruct((n2 * n1, width), BF16)
    sb = FFT_STAGE1_COLS
    zr, zi = pl.pallas_call(
        functools.partial(_fft_stage1_kernel, n1=n1),
        grid=(n2 // sb,),
        in_specs=[
            pl.BlockSpec((2 * n1, n1), lambda j: (0, 0)),
            pl.BlockSpec((n1, sb * width), lambda j: (0, j)),
            pl.BlockSpec((n1, sb * LANES), lambda j: (0, j)),
            pl.BlockSpec((n1, sb * LANES), lambda j: (0, j)),
        ],
        out_specs=[pl.BlockSpec((sb * n1, width), lambda j: (j, 0))] * 2,
        out_shape=[z_shape, z_shape],
        compiler_params=_params(1, 16 * 1024 * 1024),
        name="fft_stage1",
    )(f1, u, tc, ts)
    chunks = min(n1, 8)
    tn = chunks * width
    y = pl.pallas_call(
        functools.partial(_fft_stage2_kernel, n2=n2, gc=gc, chunks=chunks, scale=1.0 / math.sqrt(m * gc)),
        grid=(n1 // chunks,),
        in_specs=[
            pl.BlockSpec((2 * n2, n2), lambda j: (0, 0)),
            pl.BlockSpec((2 * n2, n2), lambda j: (0, 0)),
            pl.BlockSpec((n2, tn), lambda j: (0, j)),
            pl.BlockSpec((n2, tn), lambda j: (0, j)),
            pl.BlockSpec((None, gc, gc), lambda j: (0, 0, 0)),
            pl.BlockSpec((None, gc, gc), lambda j: (1, 0, 0)),
            pl.BlockSpec((N_GROUPS, gc, gc), lambda j: (0, 0, 0)),
        ],
        out_specs=pl.BlockSpec((n2, tn), lambda j: (0, j)),
        out_shape=jax.ShapeDtypeStruct((n2, n1 * width), BF16),
        compiler_params=_params(1, 40 * 1024 * 1024),
        name="fft_stage2",
    )(fa, fb, zr.reshape(n2, n1 * width), zi.reshape(n2, n1 * width), cs_chan, cs_chan, fnet_w)
    return y.reshape(m, width)


def _fourier_mix(p, col_blk, cs_chan, fnet_w):
    if p.shape[0] >= FFT_MIN_LEN and p.shape[0] % (LANES * SUBLANES_F32) == 0:
        return _fourier_mix_fft(p, col_blk, cs_chan, fnet_w)
    return _fourier_mix_dense(p, col_blk, cs_chan, fnet_w)


def _merge_kernel(h_ref, yp_ref, ya_ref, yf_ref, wg0, wg1, wg2, bg0, bg1, bg2, wp, wa, wf, o_ref):
    h = h_ref[...]

    def gate(w_ref, b_ref):
        return jax.nn.sigmoid(_dot(h, w_ref[...]) + b_ref[...])

    m = gate(wg0, bg0) * _dot(yp_ref[...], wp[...])
    m = m + gate(wg1, bg1) * _dot(ya_ref[...], wa[...])
    m = m + gate(wg2, bg2) * _dot(yf_ref[...], wf[...])
    o_ref[...] = m.astype(o_ref.dtype)


def _merge(h, y_pool, y_attn, y_fnet, layer, w_gate, b_gate, w_bp, w_ba, w_bf):
    m, d = h.shape
    dp, da, df = y_pool.shape[1], y_attn.shape[1], y_fnet.shape[1]
    tm = min(m, 1024)
    tn = _tile(d, 256)
    nj = d // tn

    def act(width):
        bufs = 1 if width == d else 2
        return pl.BlockSpec((tm, width), lambda i, j: (i, 0), pipeline_mode=pl.Buffered(bufs))

    def gate_w(b):
        return pl.BlockSpec((None, d, tn), lambda i, j: (layer, 0, b * nj + j))

    def gate_b(b):
        return pl.BlockSpec((None, 1, tn), lambda i, j: (layer, 0, b * nj + j))

    def br_w(k):
        return pl.BlockSpec((None, k, tn), lambda i, j: (layer, 0, j))

    vmem = 4 * tm * (d + dp + da + df) + 4 * tn * (3 * d + dp + da + df) + 12 * tm * tn * 4
    return pl.pallas_call(
        _merge_kernel,
        grid=(m // tm, nj),
        in_specs=[act(d), act(dp), act(da), act(df), gate_w(0), gate_w(1), gate_w(2),
                  gate_b(0), gate_b(1), gate_b(2), br_w(dp), br_w(da), br_w(df)],
        out_specs=pl.BlockSpec((tm, tn), lambda i, j: (i, j)),
        out_shape=jax.ShapeDtypeStruct((m, d), BF16),
        compiler_params=_params(2, vmem),
        name="merge",
    )(h, y_pool, y_attn, y_fnet, w_gate, w_gate, w_gate, b_gate, b_gate, b_gate, w_bp, w_ba, w_bf)


def _resid_kernel(a_ref, w_ref, x_ref, g_ref, o_ref):
    o_ref[...] = x_ref[...] + g_ref[...] * _dot(a_ref[...], w_ref[...])


def _resid_proj(a, w, layer, x, g, tm_pref, tn_pref):
    m, kk = a.shape
    n = w.shape[2]
    tm = min(m, tm_pref)
    tn = _tile(n, tn_pref)
    return pl.pallas_call(
        _resid_kernel,
        grid=(m // tm, n // tn),
        in_specs=[
            pl.BlockSpec((tm, kk), lambda i, j: (i, 0)),
            pl.BlockSpec((None, kk, tn), lambda i, j: (layer, 0, j)),
            pl.BlockSpec((tm, tn), lambda i, j: (i, j)),
            pl.BlockSpec((1, tn), lambda i, j: (0, j)),
        ],
        out_specs=pl.BlockSpec((tm, tn), lambda i, j: (i, j)),
        out_shape=jax.ShapeDtypeStruct((m, n), F32),
        compiler_params=_params(2, 4 * tm * kk + 4 * kk * tn + 7 * tm * tn * 4),
        name="resid_proj",
    )(a, w, x, g)


def _ffn_up_kernel(*refs, n_cast):
    a_ref, halo_ref, wg_ref, wv_ref, cwg_ref, cwv_ref, cbg_ref, cbv_ref = refs[:8]
    o_ref = refs[8 + n_cast]
    a_ext = refs[-1]
    for src, dst in zip(refs[8:8 + n_cast], refs[9 + n_cast:9 + 2 * n_cast]):
        dst[...] = src[...].astype(dst.dtype)
    tm = a_ref.shape[0]

    @pl.when(pl.program_id(1) == 0)
    def _():
        a_ext[:tm, :] = a_ref[...]
        a_ext[tm:, :] = halo_ref[0]

    a = a_ext[...]
    row = lax.broadcasted_iota(jnp.int32, (tm, wg_ref.shape[1]), 0)

    def conv(w_ref, cw_ref, cb_ref):
        u = _dot(a, w_ref[...].astype(BF16))
        um = u[:tm]
        next_row = u[tm:tm + 1]
        prev_row = u[tm + 2 * SUBLANES_F32 - 1:tm + 2 * SUBLANES_F32]
        u_dn = jnp.where(row == 0, prev_row, pltpu.roll(um, 1, 0))
        u_up = jnp.where(row == tm - 1, next_row, pltpu.roll(um, tm - 1, 0))
        return u_dn * cw_ref[0:1] + um * cw_ref[1:2] + u_up * cw_ref[2:3] + cb_ref[...]

    gate = conv(wg_ref, cwg_ref, cbg_ref)
    val = conv(wv_ref, cwv_ref, cbv_ref)
    o_ref[...] = (gate * jax.nn.sigmoid(gate) * val).astype(o_ref.dtype)


def _ffn_up(h2, halo, layer, w_up, conv_w, conv_b, casts=()):
    m, d = h2.shape
    f = w_up.shape[2] // 2
    tm = m // halo.shape[0]
    ext = halo.shape[1]
    tn = _tile(f, 256)
    nj = f // tn
    w_bytes = jnp.dtype(w_up.dtype).itemsize
    a_bufs = 1 if w_bytes == 4 else 2
    vmem = (2 * a_bufs * tm * d + 2 * (tm + ext) * d + (4 * w_bytes + 4) * d * tn + 4 * tm * tn
            + 12 * (tm + ext) * tn * 4)

    def cols(rows, half):
        return pl.BlockSpec((None, rows, tn), lambda i, j: (layer, 0, half * nj + j))

    steps = (m // tm * nj, lambda i, j: i * nj + j)
    cast_specs = [_cast_spec(w.shape, layer, steps) for w in casts]
    vmem += sum(6 * 2 * s[1].block_shape[0] * s[1].block_shape[1] for s in cast_specs)
    out = pl.pallas_call(
        functools.partial(_ffn_up_kernel, n_cast=len(casts)),
        grid=(m // tm, nj),
        in_specs=[
            pl.BlockSpec((tm, d), lambda i, j: (i, 0), pipeline_mode=pl.Buffered(a_bufs)),
            pl.BlockSpec((1, ext, d), lambda i, j: (i, 0, 0)),
            cols(d, 0), cols(d, 1), cols(CONV_W, 0), cols(CONV_W, 1), cols(1, 0), cols(1, 1),
        ] + [s[0] for s in cast_specs],
        out_specs=[pl.BlockSpec((tm, tn), lambda i, j: (i, j))] + [s[1] for s in cast_specs],
        out_shape=[jax.ShapeDtypeStruct((m, f), BF16)] + [s[2] for s in cast_specs],
        scratch_shapes=[pltpu.VMEM((tm + ext, d), BF16)],
        compiler_params=_params(2, vmem),
        name="ffn_up",
    )(h2, halo, w_up, w_up, conv_w, conv_w, conv_b, conv_b, *casts)
    return out[0], out[1:]


def _ffn_mm_kernel(*refs, n_cast):
    a_ref, w_ref = refs[:2]
    o_ref = refs[2 + n_cast]
    for src, dst in zip(refs[2:2 + n_cast], refs[3 + n_cast:3 + 2 * n_cast]):
        dst[...] = src[...].astype(dst.dtype)
    o_ref[...] = _dot(a_ref[...], w_ref[...].astype(BF16)).astype(o_ref.dtype)


def _ffn_act_kernel(pg_ref, mg_ref, ng_ref, pv_ref, mv_ref, nv_ref, cwg_ref, cwv_ref, cbg_ref, cbv_ref, o_ref):
    i = pl.program_id(0)
    n = pl.num_programs(0)
    tr = mg_ref.shape[0]
    hr = pg_ref.shape[0]
    row = lax.broadcasted_iota(jnp.int32, mg_ref.shape, 0)

    def conv(p_ref, m_ref, n_ref, cw_ref, cb_ref):
        u = m_ref[...].astype(F32)
        prev_row = jnp.where(i > 0, p_ref[hr - 1:hr, :].astype(F32), 0.0)
        next_row = jnp.where(i < n - 1, n_ref[0:1, :].astype(F32), 0.0)
        u_dn = jnp.where(row == 0, prev_row, pltpu.roll(u, 1, 0))
        u_up = jnp.where(row == tr - 1, next_row, pltpu.roll(u, tr - 1, 0))
        return u_dn * cw_ref[0:1] + u * cw_ref[1:2] + u_up * cw_ref[2:3] + cb_ref[...]

    gate = conv(pg_ref, mg_ref, ng_ref, cwg_ref, cbg_ref)
    val = conv(pv_ref, mv_ref, nv_ref, cwv_ref, cbv_ref)
    o_ref[...] = (gate * jax.nn.sigmoid(gate) * val).astype(o_ref.dtype)


def _ffn_split(h2, layer, w_up, conv_w, conv_b, casts=()):
    m, d = h2.shape
    n2 = w_up.shape[2]
    f = n2 // 2
    tm = min(m, 1024)
    tn = _tile(n2, 512)
    steps = (m // tm * (n2 // tn), lambda i, j: i * (n2 // tn) + j)
    cast_specs = [_cast_spec(w.shape, layer, steps) for w in casts]
    cast_bytes = sum(6 * 2 * s[1].block_shape[0] * s[1].block_shape[1] for s in cast_specs)
    out = pl.pallas_call(
        functools.partial(_ffn_mm_kernel, n_cast=len(casts)),
        grid=(m // tm, n2 // tn),
        in_specs=[
            pl.BlockSpec((tm, d), lambda i, j: (i, 0)),
            pl.BlockSpec((None, d, tn), lambda i, j: (layer, 0, j)),
        ] + [s[0] for s in cast_specs],
        out_specs=[pl.BlockSpec((tm, tn), lambda i, j: (i, j))] + [s[1] for s in cast_specs],
        out_shape=[jax.ShapeDtypeStruct((m, n2), BF16)] + [s[2] for s in cast_specs],
        compiler_params=_params(2, 4 * tm * d + 10 * d * tn + 4 * tm * tn + 2 * tm * tn * 4 + cast_bytes),
        name="ffn_mm",
    )(h2, w_up, *casts)
    u = out[0]
    tr = min(m, 64)
    hr = SUBLANES_BF16
    per = tr // hr
    last = m // hr - 1

    def rows(kind, half):
        if kind == 0:
            return pl.BlockSpec((hr, f), lambda i: (jnp.maximum(i * per - 1, 0), half))
        if kind == 1:
            return pl.BlockSpec((tr, f), lambda i: (i, half))
        return pl.BlockSpec((hr, f), lambda i: (jnp.minimum((i + 1) * per, last), half))

    def par(r, half):
        return pl.BlockSpec((None, r, f), lambda i: (layer, 0, half))

    t = pl.pallas_call(
        _ffn_act_kernel,
        grid=(m // tr,),
        in_specs=[rows(0, 0), rows(1, 0), rows(2, 0), rows(0, 1), rows(1, 1), rows(2, 1),
                  par(CONV_W, 0), par(CONV_W, 1), par(1, 0), par(1, 1)],
        out_specs=pl.BlockSpec((tr, f), lambda i: (i, 0)),
        out_shape=jax.ShapeDtypeStruct((m, f), BF16),
        compiler_params=_params(1, 48 * tr * f * 4),
        name="ffn_act",
    )(u, u, u, u, u, u, conv_w, conv_w, conv_b, conv_b)
    return t, out[1:]


def kernel(x, c, ctx, c_ctx, w_ada, b_ada, norm1_w, norm2_w, w_in, pool_w, pool_scale, q_norm_w, k_norm_w, rpb, fnet_w, w_gate, b_gate, w_br_pool, w_br_attn, w_br_fnet, w_o, w_up, conv_w, conv_b, w_down):
    n_layers, d, d_in = w_in.shape
    batch, s_len, _ = x.shape
    n_ctx = ctx.shape[1]
    assert batch == 1
    d_pool = w_br_pool.shape[1]
    d_attn = w_br_attn.shape[1]
    d_fnet = w_br_fnet.shape[1]
    n_heads = d_attn // HEAD_DIM
    assert d_in == d_pool + 3 * d_attn + d_fnet and s_len % GRID_W == 0 and d_fnet == d_pool
    q_lo, k_lo, v_lo, f_lo = d_pool, d_pool + d_attn, d_pool + 2 * d_attn, d_pool + 3 * d_attn

    pool_w_b = pool_w.astype(BF16)
    fnet_w_b = fnet_w.astype(BF16)
    b_gate_r = b_gate.reshape(n_layers, 1, -1)
    conv_b_r = conv_b.reshape(n_layers, 1, -1)
    ones_p = jnp.ones((d_pool,), F32)
    ones_vf = jnp.ones((d_attn + d_fnet,), F32)

    cs_chan = jnp.stack(_dft_tables(fnet_w.shape[-1])).astype(BF16)

    cond = jnp.concatenate([c.reshape(1, d), c_ctx.reshape(1, d), jnp.zeros((SUBLANES_F32 - 2, d), F32)], axis=0)
    mod = _ada(cond, w_ada, b_ada)

    bias = _nat_bias(rpb.reshape((n_layers * n_heads,) + rpb.shape[2:]), s_len // GRID_W)

    tm_ffn = min(s_len, 1024)
    blk = lambda col: col // HEAD_DIM
    tn_in = _tile(d_pool, 512)
    assert d_pool % tn_in == 0 and d_attn % tn_in == 0
    qk_tiles = (q_lo // tn_in, v_lo // tn_in)

    xl = x.reshape(s_len, d)
    xc = ctx.reshape(n_ctx, d)
    for l in range(n_layers):
        last = l == n_layers - 1
        sh1, sc1, g1, sh2, sc2, g2 = [mod[l, 0:1, i * d:(i + 1) * d] for i in range(N_MOD)]
        csh1, csc1, cg1, csh2, csc2, cg2 = [mod[l, 1:2, i * d:(i + 1) * d] for i in range(N_MOD)]
        n1w = norm1_w[l].reshape(1, d)
        n2w = norm2_w[l].reshape(1, d)
        nw_tiles = jnp.concatenate(
            [ones_p, jnp.tile(q_norm_w[l], n_heads), jnp.tile(k_norm_w[l], n_heads), ones_vf]
        ).reshape(d_in // tn_in, 1, tn_in)
        pool_scale_l = pool_scale[l].reshape(1, -1)

        h = _norm_mod(xl, n1w, sh1, sc1)
        hc = _norm_mod(xc, n1w, csh1, csc1)
        p, w_bf16 = _in_proj(h, w_in, l, nw_tiles, tn_in, 0, d_in // tn_in, *qk_tiles,
                             casts=(w_gate, w_br_pool, w_br_attn, w_br_fnet, w_o))
        w_gate_b, w_bp_b, w_ba_b, w_bf_b, w_o_b = [w[None] for w in w_bf16]
        b_gate_l = b_gate_r[l:l + 1]
        if last:
            pc, _ = _in_proj(hc, w_in, l, nw_tiles, tn_in, k_lo // tn_in, (f_lo - k_lo) // tn_in, *qk_tiles)
            kc_blk, vc_blk = 0, blk(d_attn)
        else:
            pc, _ = _in_proj(hc, w_in, l, nw_tiles, tn_in, 0, d_in // tn_in, *qk_tiles)
            kc_blk, vc_blk = blk(k_lo), blk(v_lo)
        y_attn = _nat_attention(p, pc, bias, l, n_heads, blk(q_lo), blk(k_lo), blk(v_lo), kc_blk, vc_blk, ATTN_HEADS)
        y_pool = _pool_mix(p, 0, pool_w_b[l], pool_scale_l)
        y_fnet = _fourier_mix(p, f_lo // d_fnet, cs_chan, fnet_w_b[l])
        m = _merge(h, y_pool, y_attn, y_fnet, 0, w_gate_b, b_gate_l, w_bp_b, w_ba_b, w_bf_b)
        xl = _resid_proj(m, w_o_b, 0, xl, g1, 1024, 1024)

        h2 = _norm_mod(xl, n2w, sh2, sc2)
        t, (w_down_l,) = _ffn_split(h2, l, w_up, conv_w, conv_b_r, casts=(w_down,))
        w_down_b = w_down_l[None]
        xl = _resid_proj(t, w_down_b, 0, xl, g2, 512, 512)

        if not last:
            yc_attn = _ctx_attention(pc, n_heads, blk(q_lo), blk(k_lo), blk(v_lo))
            yc_pool = _pool_mix(pc, 0, pool_w_b[l], pool_scale_l)
            yc_fnet = _fourier_mix(pc, f_lo // d_fnet, cs_chan, fnet_w_b[l])
            mc = _merge(hc, yc_pool, yc_attn, yc_fnet, 0, w_gate_b, b_gate_l, w_bp_b, w_ba_b, w_bf_b)
            xc = _resid_proj(mc, w_o_b, 0, xc, cg1, 1024, 1024)
            hc2 = _norm_mod(xc, n2w, csh2, csc2)
            tc, _ = _ffn_split(hc2, l, w_up, conv_w, conv_b_r)
            xc = _resid_proj(tc, w_down_b, 0, xc, cg2, 512, 512)
    return xl.reshape(batch, s_len, d)
```

```python
import functools
import math

import numpy as np
import jax
import jax.numpy as jnp
from jax import lax
from jax.experimental import pallas as pl
from jax.experimental.pallas import tpu as pltpu

F32 = jnp.float32
BF16 = jnp.bfloat16

GRID_W = 64
HEAD_DIM = 128
WIN_ROWS = 8
WIN_COLS = 16
POOL_WINDOWS = (2, 4, 8, 16)
N_GROUPS = 4
N_MOD = 6
CONV_W = 3
EPS = 1e-6

LANES = 128
SUBLANES_F32 = 8
SUBLANES_BF16 = 16
VMEM_CAP_BYTES = 56 * 1024 * 1024

ATTN_ROWS = 4
ATTN_KEY_ROWS = ATTN_ROWS + WIN_ROWS
ATTN_HEADS = 8
FFT_STAGE1_COLS = 16
EPILOGUE_ROWS = 256
NEG_BIAS = -1e30
FFT_MIN_LEN = 1024


def _tile(n, pref):
    if n <= pref:
        return n
    t = pref - pref % LANES
    while t >= LANES:
        if n % t == 0:
            return t
        t -= LANES
    return n


def _params(n_axes, vmem_bytes):
    return pltpu.CompilerParams(
        dimension_semantics=("arbitrary",) * n_axes,
        vmem_limit_bytes=int(min(VMEM_CAP_BYTES, max(vmem_bytes, 16 * 1024 * 1024))),
    )


def _dot(a, b):
    return jnp.dot(a, b, preferred_element_type=F32)


def _dot_nt(a, b):
    return lax.dot_general(a, b, (((1,), (1,)), ((), ())), preferred_element_type=F32)


def _rms_mod(x, nw, sh, sc):
    ms = jnp.mean(x * x, axis=-1, keepdims=True)
    return (x * lax.rsqrt(ms + EPS) * nw) * (1.0 + sc) + sh


def _ada_kernel(s_ref, w_ref, b_ref, o_ref):
    s = s_ref[...]
    a = (s * jax.nn.sigmoid(s)).astype(BF16)
    o_ref[0] = _dot(a, w_ref[0].astype(BF16)) + b_ref[0]


def _ada(cond, w_ada, b_ada):
    n_layers, d, n = w_ada.shape
    tn = _tile(n, 1024)
    return pl.pallas_call(
        _ada_kernel,
        grid=(n_layers, n // tn),
        in_specs=[
            pl.BlockSpec((SUBLANES_F32, d), lambda l, j: (0, 0)),
            pl.BlockSpec((1, d, tn), lambda l, j: (l, 0, j)),
            pl.BlockSpec((1, 1, tn), lambda l, j: (l, 0, j)),
        ],
        out_specs=pl.BlockSpec((1, SUBLANES_F32, tn), lambda l, j: (l, 0, j)),
        out_shape=jax.ShapeDtypeStruct((n_layers, SUBLANES_F32, n), F32),
        compiler_params=_params(2, 2 * d * tn * 4 + 8 * 1024 * 1024),
        name="ada",
    )(cond, w_ada, b_ada.reshape(n_layers, 1, n))


def _norm_kernel(x_ref, nw_ref, sh_ref, sc_ref, o_ref):
    o_ref[...] = _rms_mod(x_ref[...], nw_ref[...], sh_ref[...], sc_ref[...]).astype(o_ref.dtype)


def _norm_mod(x, nw, sh, sc):
    m, d = x.shape
    tm = min(m, 512)
    vec = pl.BlockSpec((1, d), lambda i: (0, 0))
    return pl.pallas_call(
        _norm_kernel,
        grid=(m // tm,),
        in_specs=[pl.BlockSpec((tm, d), lambda i: (i, 0)), vec, vec, vec],
        out_specs=pl.BlockSpec((tm, d), lambda i: (i, 0)),
        out_shape=jax.ShapeDtypeStruct((m, d), BF16),
        compiler_params=_params(1, 6 * tm * d * 4),
        name="norm_mod",
    )(x, nw, sh, sc)


def _halo_kernel(xp_ref, xn_ref, nw_ref, sh_ref, sc_ref, o_ref):
    i = pl.program_id(0)
    n = pl.num_programs(0)
    hp = _rms_mod(xp_ref[...], nw_ref[...], sh_ref[...], sc_ref[...])
    hn = _rms_mod(xn_ref[...], nw_ref[...], sh_ref[...], sc_ref[...])
    hp = jnp.where(i > 0, hp, 0.0)
    hn = jnp.where(i < n - 1, hn, 0.0)
    o_ref[0] = jnp.concatenate([hn, hp], axis=0).astype(o_ref.dtype)


def _halo_norm(x, nw, sh, sc, tm):
    m, d = x.shape
    nblk = m // tm
    r = SUBLANES_F32
    per = tm // r
    last = m // r - 1
    vec = pl.BlockSpec((1, d), lambda i: (0, 0))
    return pl.pallas_call(
        _halo_kernel,
        grid=(nblk,),
        in_specs=[
            pl.BlockSpec((r, d), lambda i: (jnp.maximum(i * per - 1, 0), 0)),
            pl.BlockSpec((r, d), lambda i: (jnp.minimum((i + 1) * per, last), 0)),
            vec, vec, vec,
        ],
        out_specs=pl.BlockSpec((1, 2 * r, d), lambda i: (i, 0, 0)),
        out_shape=jax.ShapeDtypeStruct((nblk, 2 * r, d), BF16),
        compiler_params=_params(1, 16 * 1024 * 1024),
        name="halo_norm",
    )(x, x, nw, sh, sc)


def _cast_spec(shape, layer, block_of_step):
    _, rows, cols = shape
    n_steps, step_of = block_of_step
    rb = max(SUBLANES_BF16, rows // n_steps // SUBLANES_BF16 * SUBLANES_BF16)
    while rows % rb:
        rb += SUBLANES_BF16
    n_blk = rows // rb

    def blk(i, j):
        return step_of(i, j) * n_blk // n_steps

    return (pl.BlockSpec((None, rb, cols), lambda i, j: (layer, blk(i, j), 0)),
            pl.BlockSpec((rb, cols), lambda i, j: (blk(i, j), 0)),
            jax.ShapeDtypeStruct((rows, cols), BF16))


def _inproj_kernel(*refs, j_off, qk_lo, qk_hi, n_cast):
    a_ref, b_ref, nw_ref = refs[:3]
    o_ref = refs[3 + n_cast]
    acc0, acc1 = refs[4 + 2 * n_cast:]
    for src, dst in zip(refs[3:3 + n_cast], refs[4 + n_cast:4 + 2 * n_cast]):
        dst[...] = src[...].astype(dst.dtype)
    tm = a_ref.shape[0]
    j = pl.program_id(1)
    nj = pl.num_programs(1) - 1
    tile = j - 1 + j_off
    is_qk = jnp.logical_and(tile >= qk_lo, tile < qk_hi)
    rc = min(tm, EPILOGUE_ROWS)

    def matmul(acc):
        acc[...] = _dot(a_ref[...], b_ref[...].astype(BF16))

    def epilogue(acc):
        for r0 in range(0, tm, rc):
            for h in range(o_ref.shape[1] // HEAD_DIM):
                sl = slice(h * HEAD_DIM, (h + 1) * HEAD_DIM)
                xh = acc[r0:r0 + rc, sl]
                ms = jnp.mean(xh * xh, axis=-1, keepdims=True)
                normed = xh * lax.rsqrt(ms + EPS) * nw_ref[0, :, sl]
                o_ref[r0:r0 + rc, sl] = jnp.where(is_qk, normed, xh).astype(o_ref.dtype)

    odd = lax.rem(j, 2) == 1
    mid = jnp.logical_and(j > 0, j < nj)

    @pl.when(j == 0)
    def _():
        matmul(acc0)

    @pl.when(jnp.logical_and(mid, odd))
    def _():
        epilogue(acc0)
        matmul(acc1)

    @pl.when(jnp.logical_and(mid, jnp.logical_not(odd)))
    def _():
        epilogue(acc1)
        matmul(acc0)

    @pl.when(jnp.logical_and(j == nj, odd))
    def _():
        epilogue(acc0)

    @pl.when(jnp.logical_and(j == nj, jnp.logical_not(odd)))
    def _():
        epilogue(acc1)


def _in_proj(h, w_in, layer, nw_tiles, tn, j_off, n_tiles, qk_lo, qk_hi, casts=()):
    m, d = h.shape
    tm = min(m, 1024)
    acc = pltpu.VMEM((tm, tn), F32)
    steps = (m // tm * n_tiles, lambda i, j: i * n_tiles + jnp.minimum(j, n_tiles - 1))
    cast_specs = [_cast_spec(w.shape, layer, steps) for w in casts]
    cast_bytes = sum(6 * 2 * s[1].block_shape[0] * s[1].block_shape[1] for s in cast_specs)
    out = pl.pallas_call(
        functools.partial(_inproj_kernel, j_off=j_off, qk_lo=qk_lo, qk_hi=qk_hi, n_cast=len(casts)),
        grid=(m // tm, n_tiles + 1),
        in_specs=[
            pl.BlockSpec((tm, d), lambda i, j: (i, 0)),
            pl.BlockSpec((None, d, tn), lambda i, j: (layer, 0, jnp.minimum(j, n_tiles - 1) + j_off)),
            pl.BlockSpec((1, 1, tn), lambda i, j: (jnp.maximum(j - 1, 0) + j_off, 0, 0)),
        ] + [s[0] for s in cast_specs],
        out_specs=[pl.BlockSpec((tm, tn), lambda i, j: (i, jnp.maximum(j - 1, 0)))] + [s[1] for s in cast_specs],
        out_shape=[jax.ShapeDtypeStruct((m, n_tiles * tn), BF16)] + [s[2] for s in cast_specs],
        scratch_shapes=[acc, acc],
        compiler_params=_params(2, 4 * tm * d + 10 * d * tn + 4 * tm * tn + 6 * tm * tn * 4 + cast_bytes),
        name="in_proj",
    )(h, w_in, nw_tiles, *casts)
    return out[0], out[1:]


def _nat_kernel(q_ref, k_ref, v_ref, kc_ref, vc_ref, bias_ref, o_ref, *, grid_rows, scale, heads):
    b = pl.program_id(1)
    ks = jnp.clip(b * ATTN_ROWS - WIN_ROWS // 2, 0, grid_rows - ATTN_KEY_ROWS)
    start = pl.multiple_of(ks * GRID_W, GRID_W)
    nkeys = ATTN_KEY_ROWS * GRID_W
    for h in range(heads):
        sl = slice(h * HEAD_DIM, (h + 1) * HEAD_DIM)
        q = q_ref[:, sl]
        kall = jnp.concatenate([k_ref[pl.ds(start, nkeys), sl], kc_ref[:, sl]], axis=0)
        vall = jnp.concatenate([v_ref[pl.ds(start, nkeys), sl], vc_ref[:, sl]], axis=0)
        s = _dot_nt(q, kall) * scale
        s = jnp.concatenate([s[:, :nkeys] + bias_ref[0, h], s[:, nkeys:]], axis=1)
        m = jnp.max(s, axis=-1, keepdims=True)
        p = jnp.exp(s - m)
        l = jnp.sum(p, axis=-1, keepdims=True)
        o = _dot(p.astype(BF16), vall)
        o_ref[:, sl] = (o / l).astype(o_ref.dtype)


def _nat_bias(rpb, grid_rows):
    n_heads = rpb.shape[0]
    nb = grid_rows // ATTN_ROWS
    c = np.arange(GRID_W)[:, None]
    kc = np.arange(GRID_W)[None, :]
    c_start = np.clip(c - WIN_COLS // 2, 0, GRID_W - WIN_COLS)
    col_ok = (kc >= c_start) & (kc < c_start + WIN_COLS)
    dc = kc - c + (WIN_COLS - 1)
    n_dc = 2 * WIN_COLS - 1
    expand = ((dc[None] == np.arange(n_dc)[:, None, None]) & col_ok[None]).astype(np.float32)
    blocks = jnp.einsum("hrd,dn->hrn", rpb, expand.reshape(n_dc, GRID_W * GRID_W),
                        precision=lax.Precision.HIGHEST)
    blocks = blocks.reshape(n_heads, 2 * WIN_ROWS - 1, GRID_W, GRID_W)
    blocks = jnp.where(col_ok, blocks, NEG_BIAS)
    masked = jnp.full((n_heads, GRID_W, GRID_W), NEG_BIAS, F32)
    variants = []
    for b in (0, 1, nb - 1):
        ks = int(np.clip(b * ATTN_ROWS - WIN_ROWS // 2, 0, grid_rows - ATTN_KEY_ROWS))
        q_rows = []
        for r_loc in range(ATTN_ROWS):
            r = b * ATTN_ROWS + r_loc
            r_start = int(np.clip(r - WIN_ROWS // 2, 0, grid_rows - WIN_ROWS))
            row = []
            for kr in range(ks, ks + ATTN_KEY_ROWS):
                ok = r_start <= kr < r_start + WIN_ROWS
                row.append(blocks[:, kr - r + (WIN_ROWS - 1)] if ok else masked)
            q_rows.append(jnp.concatenate(row, axis=-1))
        variants.append(jnp.concatenate(q_rows, axis=-2))
    return jnp.stack(variants)


def _nat_attention(p_lat, p_ctx, bias, layer, n_heads, q_blk, k_blk, v_blk, kc_blk, vc_blk, hs):
    s_len = p_lat.shape[0]
    n_ctx = p_ctx.shape[0]
    grid_rows = s_len // GRID_W
    nb = grid_rows // ATTN_ROWS
    nq = ATTN_ROWS * GRID_W
    nk = ATTN_KEY_ROWS * GRID_W
    assert grid_rows % ATTN_ROWS == 0 and nb >= 3 and grid_rows >= ATTN_KEY_ROWS
    while n_heads % hs or any(blk % hs for blk in (q_blk, k_blk, v_blk, kc_blk, vc_blk)):
        hs //= 2
    width = hs * HEAD_DIM

    def variant(b):
        return jnp.where(b == 0, 0, jnp.where(b == nb - 1, 2, 1))

    def cols(rows, blk):
        mode = pl.Buffered(1 if (rows == s_len and hs > 2) else 2)
        return pl.BlockSpec((rows, width), lambda h, b: (0, blk // hs + h), pipeline_mode=mode)

    return pl.pallas_call(
        functools.partial(_nat_kernel, grid_rows=grid_rows, scale=HEAD_DIM ** -0.5, heads=hs),
        grid=(n_heads // hs, nb),
        in_specs=[
            pl.BlockSpec((nq, width), lambda h, b: (b, q_blk // hs + h)),
            cols(s_len, k_blk), cols(s_len, v_blk), cols(n_ctx, kc_blk), cols(n_ctx, vc_blk),
            pl.BlockSpec((1, hs, nq, nk), lambda h, b: (variant(b), layer * (n_heads // hs) + h, 0, 0)),
        ],
        out_specs=pl.BlockSpec((nq, width), lambda h, b: (b, h)),
        out_shape=jax.ShapeDtypeStruct((s_len, n_heads * HEAD_DIM), BF16),
        compiler_params=_params(2, 8 * s_len * width + hs * (2 * nq * nk * 4 + 16 * nq * (nk + n_ctx) * 4)),
        name="nat_attention",
    )(p_lat, p_lat, p_lat, p_ctx, p_ctx, bias)


def _ctx_attn_kernel(q_ref, k_ref, v_ref, o_ref, *, scale):
    s = _dot_nt(q_ref[...], k_ref[...]) * scale
    m = jnp.max(s, axis=-1, keepdims=True)
    p = jnp.exp(s - m)
    l = jnp.sum(p, axis=-1, keepdims=True)
    o_ref[...] = (_dot(p.astype(BF16), v_ref[...]) / l).astype(o_ref.dtype)


def _ctx_attention(p_ctx, n_heads, q_blk, k_blk, v_blk):
    n = p_ctx.shape[0]
    return pl.pallas_call(
        functools.partial(_ctx_attn_kernel, scale=HEAD_DIM ** -0.5),
        grid=(n_heads,),
        in_specs=[
            pl.BlockSpec((n, HEAD_DIM), lambda h: (0, q_blk + h)),
            pl.BlockSpec((n, HEAD_DIM), lambda h: (0, k_blk + h)),
            pl.BlockSpec((n, HEAD_DIM), lambda h: (0, v_blk + h)),
        ],
        out_specs=pl.BlockSpec((n, HEAD_DIM), lambda h: (0, h)),
        out_shape=jax.ShapeDtypeStruct((n, n_heads * HEAD_DIM), BF16),
        compiler_params=_params(1, 16 * 1024 * 1024),
        name="ctx_attention",
    )(p_ctx, p_ctx, p_ctx)


def _pool_kernel(up_ref, um_ref, un_ref, pw_ref, ps_ref, o_ref, *, seq_len, gc):
    i = pl.program_id(0)
    n = pl.num_programs(0)
    tp = um_ref.shape[0]
    halo = up_ref.shape[0]
    um = um_ref[...]
    up = jnp.where(i > 0, up_ref[...], jnp.zeros_like(up_ref[...]))
    un = jnp.where(i < n - 1, un_ref[...], jnp.zeros_like(un_ref[...]))
    ue = jnp.concatenate([up, um, un], axis=0)
    trow = lax.broadcasted_iota(jnp.int32, (tp, tp + 2 * halo), 0)
    jcol = lax.broadcasted_iota(jnp.int32, (tp, tp + 2 * halo), 1)
    off = jcol - trow - halo
    t_abs = i * tp + lax.broadcasted_iota(jnp.int32, (tp, gc), 0)
    for g, w in enumerate(POOL_WINDOWS):
        lo_off = -(w // 2)
        hi_off = w - 1 - w // 2
        band = jnp.where(off >= lo_off, jnp.where(off <= hi_off, 1.0, 0.0), 0.0).astype(BF16)
        sl = slice(g * gc, (g + 1) * gc)
        wsum = _dot(band, ue[:, sl])
        lo = jnp.maximum(t_abs + lo_off, 0)
        hi = jnp.minimum(t_abs + hi_off, seq_len - 1)
        cnt = (hi - lo + 1).astype(F32)
        d = (wsum / cnt - um[:, sl].astype(F32)).astype(BF16)
        o_ref[:, sl] = (_dot(d, pw_ref[g]) * ps_ref[:, sl]).astype(o_ref.dtype)


def _pool_mix(p, col_blk, pool_w, pool_scale):
    m = p.shape[0]
    gc = pool_w.shape[-1]
    width = N_GROUPS * gc
    tp = min(m, 512)
    halo = SUBLANES_BF16
    per = tp // halo
    last = m // halo - 1
    return pl.pallas_call(
        functools.partial(_pool_kernel, seq_len=m, gc=gc),
        grid=(m // tp,),
        in_specs=[
            pl.BlockSpec((halo, width), lambda i: (jnp.maximum(i * per - 1, 0), col_blk)),
            pl.BlockSpec((tp, width), lambda i: (i, col_blk)),
            pl.BlockSpec((halo, width), lambda i: (jnp.minimum((i + 1) * per, last), col_blk)),
            pl.BlockSpec((N_GROUPS, gc, gc), lambda i: (0, 0, 0)),
            pl.BlockSpec((1, width), lambda i: (0, 0)),
        ],
        out_specs=pl.BlockSpec((tp, width), lambda i: (i, 0)),
        out_shape=jax.ShapeDtypeStruct((m, width), BF16),
        compiler_params=_params(1, 32 * 1024 * 1024),
        name="pool_mix",
    )(p, p, p, pool_w, pool_scale)


def _cos_sin(num, den):
    ang = (num % den).astype(F32) * (2.0 * math.pi / den)
    return jnp.cos(ang), jnp.sin(ang)


def _dft_tables(n):
    idx = jnp.arange(n, dtype=jnp.int32)
    return _cos_sin(idx[:, None] * idx[None, :], n)


def _fnet_chan_kernel(u_ref, cs_ref, o_ref, *, gc):
    for g in range(N_GROUPS):
        sl = slice(g * gc, (g + 1) * gc)
        o_ref[:, sl] = _dot(u_ref[:, sl], cs_ref[0]).astype(o_ref.dtype)


def _fnet_pos_kernel(t_ref, ab_ref, fw_ref, o_ref, *, scale, gc):
    acc = _dot(t_ref[...], ab_ref[...])
    for g in range(N_GROUPS):
        sl = slice(g * gc, (g + 1) * gc)
        f = (acc[:, sl] * scale).astype(BF16)
        o_ref[:, sl] = _dot(f, fw_ref[g]).astype(o_ref.dtype)


def _fourier_mix_dense(p, col_blk, cs_chan, fnet_w):
    m = p.shape[0]
    gc = fnet_w.shape[-1]
    width = N_GROUPS * gc
    cpos, spos = _dft_tables(m)
    t_pos = jnp.concatenate([cpos, -spos], axis=1).astype(BF16)
    ab = pl.pallas_call(
        functools.partial(_fnet_chan_kernel, gc=gc),
        grid=(2,),
        in_specs=[
            pl.BlockSpec((m, width), lambda c: (0, col_blk)),
            pl.BlockSpec((1, gc, gc), lambda c: (c, 0, 0)),
        ],
        out_specs=pl.BlockSpec((m, width), lambda c: (c, 0)),
        out_shape=jax.ShapeDtypeStruct((2 * m, width), BF16),
        compiler_params=_params(1, 32 * 1024 * 1024),
        name="fnet_chan",
    )(p, cs_chan)
    return pl.pallas_call(
        functools.partial(_fnet_pos_kernel, scale=1.0 / math.sqrt(m * gc), gc=gc),
        grid=(1,),
        in_specs=[
            pl.BlockSpec((m, 2 * m), lambda i: (0, 0)),
            pl.BlockSpec((2 * m, width), lambda i: (0, 0)),
            pl.BlockSpec((N_GROUPS, gc, gc), lambda i: (0, 0, 0)),
        ],
        out_specs=pl.BlockSpec((m, width), lambda i: (0, 0)),
        out_shape=jax.ShapeDtypeStruct((m, width), BF16),
        compiler_params=_params(1, 32 * 1024 * 1024),
        name="fnet_pos",
    )(t_pos, ab, fnet_w)


def _fft_stage1_kernel(f1_ref, u_ref, tc_ref, ts_ref, zr_ref, zi_ref, *, n1):
    width = zr_ref.shape[1]
    reps = width // LANES
    for t in range(u_ref.shape[1] // width):
        y = _dot(f1_ref[...], u_ref[:, t * width:(t + 1) * width])
        yr, yi = y[:n1], y[n1:]
        tc = jnp.tile(tc_ref[:, t * LANES:(t + 1) * LANES], (1, reps))
        ts = jnp.tile(ts_ref[:, t * LANES:(t + 1) * LANES], (1, reps))
        zr_ref[t * n1:(t + 1) * n1, :] = (yr * tc + yi * ts).astype(zr_ref.dtype)
        zi_ref[t * n1:(t + 1) * n1, :] = (yi * tc - yr * ts).astype(zi_ref.dtype)


def _fft_stage2_kernel(fa_ref, fb_ref, zr_ref, zi_ref, cc_ref, sc_ref, fw_ref, o_ref, *, n2, gc, chunks, scale):
    width = N_GROUPS * gc
    p = _dot(fa_ref[...], zr_ref[...]) + _dot(fb_ref[...], zi_ref[...])
    for g in range(N_GROUPS):
        cols = [slice(ch * width + g * gc, ch * width + (g + 1) * gc) for ch in range(chunks)]
        pr = jnp.concatenate([p[:n2, c] for c in cols], axis=0).astype(BF16)
        pi = jnp.concatenate([p[n2:, c] for c in cols], axis=0).astype(BF16)
        f = ((_dot(pr, cc_ref[...]) + _dot(pi, sc_ref[...])) * scale).astype(BF16)
        y = _dot(f, fw_ref[g]).astype(o_ref.dtype)
        for ch, c in enumerate(cols):
            o_ref[:, c] = y[ch * n2:(ch + 1) * n2]


def _fourier_mix_fft(p, col_blk, cs_chan, fnet_w):
    m, d_in = p.shape
    gc = fnet_w.shape[-1]
    width = N_GROUPS * gc
    n2 = LANES
    n1 = m // n2
    i1 = jnp.arange(n1, dtype=jnp.int32)
    i2 = jnp.arange(n2, dtype=jnp.int32)
    c1, s1 = _cos_sin(i1[:, None] * i1[None, :], n1)
    f1 = jnp.concatenate([c1, -s1], axis=0).astype(BF16)
    tc, ts = _cos_sin(i1[:, None] * i2[None, :], m)
    tc = jnp.repeat(tc, LANES, axis=1)
    ts = jnp.repeat(ts, LANES, axis=1)
    c2, s2 = _cos_sin(i2[:, None] * i2[None, :], n2)
    fa = jnp.concatenate([c2, -s2], axis=0).astype(BF16)
    fb = jnp.concatenate([s2, c2], axis=0).astype(BF16)
    u = p[:, col_blk * width:(col_blk + 1) * width].reshape(n1, n2 * width)
    z_shape = jax.ShapeDtypeStruct((n2 * n1, width), BF16)
    sb = FFT_STAGE1_COLS
    zr, zi = pl.pallas_call(
        functools.partial(_fft_stage1_kernel, n1=n1),
        grid=(n2 // sb,),
        in_specs=[
            pl.BlockSpec((2 * n1, n1), lambda j: (0, 0)),
            pl.BlockSpec((n1, sb * width), lambda j: (0, j)),
            pl.BlockSpec((n1, sb * LANES), lambda j: (0, j)),
            pl.BlockSpec((n1, sb * LANES), lambda j: (0, j)),
        ],
        out_specs=[pl.BlockSpec((sb * n1, width), lambda j: (j, 0))] * 2,
        out_shape=[z_shape, z_shape],
        compiler_params=_params(1, 16 * 1024 * 1024),
        name="fft_stage1",
    )(f1, u, tc, ts)
    chunks = min(n1, 8)
    tn = chunks * width
    y = pl.pallas_call(
        functools.partial(_fft_stage2_kernel, n2=n2, gc=gc, chunks=chunks, scale=1.0 / math.sqrt(m * gc)),
        grid=(n1 // chunks,),
        in_specs=[
            pl.BlockSpec((2 * n2, n2), lambda j: (0, 0)),
            pl.BlockSpec((2 * n2, n2), lambda j: (0, 0)),
            pl.BlockSpec((n2, tn), lambda j: (0, j)),
            pl.BlockSpec((n2, tn), lambda j: (0, j)),
            pl.BlockSpec((None, gc, gc), lambda j: (0, 0, 0)),
            pl.BlockSpec((None, gc, gc), lambda j: (1, 0, 0)),
            pl.BlockSpec((N_GROUPS, gc, gc), lambda j: (0, 0, 0)),
        ],
        out_specs=pl.BlockSpec((n2, tn), lambda j: (0, j)),
        out_shape=jax.ShapeDtypeStruct((n2, n1 * width), BF16),
        compiler_params=_params(1, 40 * 1024 * 1024),
        name="fft_stage2",
    )(fa, fb, zr.reshape(n2, n1 * width), zi.reshape(n2, n1 * width), cs_chan, cs_chan, fnet_w)
    return y.reshape(m, width)


def _fourier_mix(p, col_blk, cs_chan, fnet_w):
    if p.shape[0] >= FFT_MIN_LEN and p.shape[0] % (LANES * SUBLANES_F32) == 0:
        return _fourier_mix_fft(p, col_blk, cs_chan, fnet_w)
    return _fourier_mix_dense(p, col_blk, cs_chan, fnet_w)


def _merge_kernel(h_ref, yp_ref, ya_ref, yf_ref, wg0, wg1, wg2, bg0, bg1, bg2, wp, wa, wf, o_ref):
    h = h_ref[...]

    def gate(w_ref, b_ref):
        return jax.nn.sigmoid(_dot(h, w_ref[...]) + b_ref[...])

    m = gate(wg0, bg0) * _dot(yp_ref[...], wp[...])
    m = m + gate(wg1, bg1) * _dot(ya_ref[...], wa[...])
    m = m + gate(wg2, bg2) * _dot(yf_ref[...], wf[...])
    o_ref[...] = m.astype(o_ref.dtype)


def _merge(h, y_pool, y_attn, y_fnet, layer, w_gate, b_gate, w_bp, w_ba, w_bf):
    m, d = h.shape
    dp, da, df = y_pool.shape[1], y_attn.shape[1], y_fnet.shape[1]
    tm = min(m, 1024)
    tn = _tile(d, 256)
    nj = d // tn

    def act(width):
        bufs = 1 if width == d else 2
        return pl.BlockSpec((tm, width), lambda i, j: (i, 0), pipeline_mode=pl.Buffered(bufs))

    def gate_w(b):
        return pl.BlockSpec((None, d, tn), lambda i, j: (layer, 0, b * nj + j))

    def gate_b(b):
        return pl.BlockSpec((None, 1, tn), lambda i, j: (layer, 0, b * nj + j))

    def br_w(k):
        return pl.BlockSpec((None, k, tn), lambda i, j: (layer, 0, j))

    vmem = 4 * tm * (d + dp + da + df) + 4 * tn * (3 * d + dp + da + df) + 12 * tm * tn * 4
    return pl.pallas_call(
        _merge_kernel,
        grid=(m // tm, nj),
        in_specs=[act(d), act(dp), act(da), act(df), gate_w(0), gate_w(1), gate_w(2),
                  gate_b(0), gate_b(1), gate_b(2), br_w(dp), br_w(da), br_w(df)],
        out_specs=pl.BlockSpec((tm, tn), lambda i, j: (i, j)),
        out_shape=jax.ShapeDtypeStruct((m, d), BF16),
        compiler_params=_params(2, vmem),
        name="merge",
    )(h, y_pool, y_attn, y_fnet, w_gate, w_gate, w_gate, b_gate, b_gate, b_gate, w_bp, w_ba, w_bf)


def _resid_kernel(a_ref, w_ref, x_ref, g_ref, o_ref):
    o_ref[...] = x_ref[...] + g_ref[...] * _dot(a_ref[...], w_ref[...])


def _resid_proj(a, w, layer, x, g, tm_pref, tn_pref):
    m, kk = a.shape
    n = w.shape[2]
    tm = min(m, tm_pref)
    tn = _tile(n, tn_pref)
    return pl.pallas_call(
        _resid_kernel,
        grid=(m // tm, n // tn),
        in_specs=[
            pl.BlockSpec((tm, kk), lambda i, j: (i, 0)),
            pl.BlockSpec((None, kk, tn), lambda i, j: (layer, 0, j)),
            pl.BlockSpec((tm, tn), lambda i, j: (i, j)),
            pl.BlockSpec((1, tn), lambda i, j: (0, j)),
        ],
        out_specs=pl.BlockSpec((tm, tn), lambda i, j: (i, j)),
        out_shape=jax.ShapeDtypeStruct((m, n), F32),
        compiler_params=_params(2, 4 * tm * kk + 4 * kk * tn + 7 * tm * tn * 4),
        name="resid_proj",
    )(a, w, x, g)


def _ffn_up_kernel(*refs, n_cast):
    a_ref, halo_ref, wg_ref, wv_ref, cwg_ref, cwv_ref, cbg_ref, cbv_ref = refs[:8]
    o_ref = refs[8 + n_cast]
    a_ext = refs[-1]
    for src, dst in zip(refs[8:8 + n_cast], refs[9 + n_cast:9 + 2 * n_cast]):
        dst[...] = src[...].astype(dst.dtype)
    tm = a_ref.shape[0]

    @pl.when(pl.program_id(1) == 0)
    def _():
        a_ext[:tm, :] = a_ref[...]
        a_ext[tm:, :] = halo_ref[0]

    a = a_ext[...]
    row = lax.broadcasted_iota(jnp.int32, (tm, wg_ref.shape[1]), 0)

    def conv(w_ref, cw_ref, cb_ref):
        u = _dot(a, w_ref[...].astype(BF16))
        um = u[:tm]
        next_row = u[tm:tm + 1]
        prev_row = u[tm + 2 * SUBLANES_F32 - 1:tm + 2 * SUBLANES_F32]
        u_dn = jnp.where(row == 0, prev_row, pltpu.roll(um, 1, 0))
        u_up = jnp.where(row == tm - 1, next_row, pltpu.roll(um, tm - 1, 0))
        return u_dn * cw_ref[0:1] + um * cw_ref[1:2] + u_up * cw_ref[2:3] + cb_ref[...]

    gate = conv(wg_ref, cwg_ref, cbg_ref)
    val = conv(wv_ref, cwv_ref, cbv_ref)
    o_ref[...] = (gate * jax.nn.sigmoid(gate) * val).astype(o_ref.dtype)


def _ffn_up(h2, halo, layer, w_up, conv_w, conv_b, casts=()):
    m, d = h2.shape
    f = w_up.shape[2] // 2
    tm = m // halo.shape[0]
    ext = halo.shape[1]
    tn = _tile(f, 256)
    nj = f // tn
    w_bytes = jnp.dtype(w_up.dtype).itemsize
    a_bufs = 1 if w_bytes == 4 else 2
    vmem = (2 * a_bufs * tm * d + 2 * (tm + ext) * d + (4 * w_bytes + 4) * d * tn + 4 * tm * tn
            + 12 * (tm + ext) * tn * 4)

    def cols(rows, half):
        return pl.BlockSpec((None, rows, tn), lambda i, j: (layer, 0, half * nj + j))

    steps = (m // tm * nj, lambda i, j: i * nj + j)
    cast_specs = [_cast_spec(w.shape, layer, steps) for w in casts]
    vmem += sum(6 * 2 * s[1].block_shape[0] * s[1].block_shape[1] for s in cast_specs)
    out = pl.pallas_call(
        functools.partial(_ffn_up_kernel, n_cast=len(casts)),
        grid=(m // tm, nj),
        in_specs=[
            pl.BlockSpec((tm, d), lambda i, j: (i, 0), pipeline_mode=pl.Buffered(a_bufs)),
            pl.BlockSpec((1, ext, d), lambda i, j: (i, 0, 0)),
            cols(d, 0), cols(d, 1), cols(CONV_W, 0), cols(CONV_W, 1), cols(1, 0), cols(1, 1),
        ] + [s[0] for s in cast_specs],
        out_specs=[pl.BlockSpec((tm, tn), lambda i, j: (i, j))] + [s[1] for s in cast_specs],
        out_shape=[jax.ShapeDtypeStruct((m, f), BF16)] + [s[2] for s in cast_specs],
        scratch_shapes=[pltpu.VMEM((tm + ext, d), BF16)],
        compiler_params=_params(2, vmem),
        name="ffn_up",
    )(h2, halo, w_up, w_up, conv_w, conv_w, conv_b, conv_b, *casts)
    return out[0], out[1:]


def kernel(x, c, ctx, c_ctx, w_ada, b_ada, norm1_w, norm2_w, w_in, pool_w, pool_scale, q_norm_w, k_norm_w, rpb, fnet_w, w_gate, b_gate, w_br_pool, w_br_attn, w_br_fnet, w_o, w_up, conv_w, conv_b, w_down):
    n_layers, d, d_in = w_in.shape
    batch, s_len, _ = x.shape
    n_ctx = ctx.shape[1]
    assert batch == 1
    d_pool = w_br_pool.shape[1]
    d_attn = w_br_attn.shape[1]
    d_fnet = w_br_fnet.shape[1]
    n_heads = d_attn // HEAD_DIM
    assert d_in == d_pool + 3 * d_attn + d_fnet and s_len % GRID_W == 0 and d_fnet == d_pool
    q_lo, k_lo, v_lo, f_lo = d_pool, d_pool + d_attn, d_pool + 2 * d_attn, d_pool + 3 * d_attn

    pool_w_b = pool_w.astype(BF16)
    fnet_w_b = fnet_w.astype(BF16)
    b_gate_r = b_gate.reshape(n_layers, 1, -1)
    conv_b_r = conv_b.reshape(n_layers, 1, -1)
    ones_p = jnp.ones((d_pool,), F32)
    ones_vf = jnp.ones((d_attn + d_fnet,), F32)

    cs_chan = jnp.stack(_dft_tables(fnet_w.shape[-1])).astype(BF16)

    cond = jnp.concatenate([c.reshape(1, d), c_ctx.reshape(1, d), jnp.zeros((SUBLANES_F32 - 2, d), F32)], axis=0)
    mod = _ada(cond, w_ada, b_ada)

    bias = _nat_bias(rpb.reshape((n_layers * n_heads,) + rpb.shape[2:]), s_len // GRID_W)

    tm_ffn = min(s_len, 1024)
    blk = lambda col: col // HEAD_DIM
    tn_in = _tile(d_pool, 512)
    assert d_pool % tn_in == 0 and d_attn % tn_in == 0
    qk_tiles = (q_lo // tn_in, v_lo // tn_in)

    xl = x.reshape(s_len, d)
    xc = ctx.reshape(n_ctx, d)
    for l in range(n_layers):
        last = l == n_layers - 1
        sh1, sc1, g1, sh2, sc2, g2 = [mod[l, 0:1, i * d:(i + 1) * d] for i in range(N_MOD)]
        csh1, csc1, cg1, csh2, csc2, cg2 = [mod[l, 1:2, i * d:(i + 1) * d] for i in range(N_MOD)]
        n1w = norm1_w[l].reshape(1, d)
        n2w = norm2_w[l].reshape(1, d)
        nw_tiles = jnp.concatenate(
            [ones_p, jnp.tile(q_norm_w[l], n_heads), jnp.tile(k_norm_w[l], n_heads), ones_vf]
        ).reshape(d_in // tn_in, 1, tn_in)
        pool_scale_l = pool_scale[l].reshape(1, -1)

        h = _norm_mod(xl, n1w, sh1, sc1)
        hc = _norm_mod(xc, n1w, csh1, csc1)
        p, w_bf16 = _in_proj(h, w_in, l, nw_tiles, tn_in, 0, d_in // tn_in, *qk_tiles,
                             casts=(w_gate, w_br_pool, w_br_attn, w_br_fnet, w_o))
        w_gate_b, w_bp_b, w_ba_b, w_bf_b, w_o_b = [w[None] for w in w_bf16]
        b_gate_l = b_gate_r[l:l + 1]
        if last:
            pc, _ = _in_proj(hc, w_in, l, nw_tiles, tn_in, k_lo // tn_in, (f_lo - k_lo) // tn_in, *qk_tiles)
            kc_blk, vc_blk = 0, blk(d_attn)
        else:
            pc, _ = _in_proj(hc, w_in, l, nw_tiles, tn_in, 0, d_in // tn_in, *qk_tiles)
            kc_blk, vc_blk = blk(k_lo), blk(v_lo)
        y_attn = _nat_attention(p, pc, bias, l, n_heads, blk(q_lo), blk(k_lo), blk(v_lo), kc_blk, vc_blk, ATTN_HEADS)
        y_pool = _pool_mix(p, 0, pool_w_b[l], pool_scale_l)
        y_fnet = _fourier_mix(p, f_lo // d_fnet, cs_chan, fnet_w_b[l])
        m = _merge(h, y_pool, y_attn, y_fnet, 0, w_gate_b, b_gate_l, w_bp_b, w_ba_b, w_bf_b)
        xl = _resid_proj(m, w_o_b, 0, xl, g1, 1024, 1024)

        h2 = _norm_mod(xl, n2w, sh2, sc2)
        halo = _halo_norm(xl, n2w, sh2, sc2, tm_ffn)
        t, (w_down_l,) = _ffn_up(h2, halo, l, w_up, conv_w, conv_b_r, casts=(w_down,))
        w_down_b = w_down_l[None]
        xl = _resid_proj(t, w_down_b, 0, xl, g2, 512, 512)

        if not last:
            yc_attn = _ctx_attention(pc, n_heads, blk(q_lo), blk(k_lo), blk(v_lo))
            yc_pool = _pool_mix(pc, 0, pool_w_b[l], pool_scale_l)
            yc_fnet = _fourier_mix(pc, f_lo // d_fnet, cs_chan, fnet_w_b[l])
            mc = _merge(hc, yc_pool, yc_attn, yc_fnet, 0, w_gate_b, b_gate_l, w_bp_b, w_ba_b, w_bf_b)
            xc = _resid_proj(mc, w_o_b, 0, xc, cg1, 1024, 1024)
            hc2 = _norm_mod(xc, n2w, csh2, csc2)
            halo_c = _halo_norm(xc, n2w, csh2, csc2, n_ctx)
            tc, _ = _ffn_up(hc2, halo_c, l, w_up, conv_w, conv_b_r)
            xc = _resid_proj(tc, w_down_b, 0, xc, cg2, 512, 512)
    return xl.reshape(batch, s_len, d)
```
